```python
import math
import jax
import jax.numpy as jnp
from jax import lax
import numpy as np

D_MODEL = 1024
BATCH = 4
SEQ = 8192
DEPTH = 2
DEC_BATCH = 1
DEC_SEQ = 16384
PAST_LEN = 128

N_MIXERS = 2
N_ATT_LAYERS = (DEPTH + 1) // 2
N_SSM_LAYERS = DEPTH // 2
N_DIFF_HEADS = 8
HEAD_DIM = D_MODEL // N_DIFF_HEADS // 2
ROPE_THETA = 10000.0
Q_BLOCK = 128
SUBLN_EPS = 1e-5
GROUP_CH = 16
N_GROUPS = D_MODEL // GROUP_CH
STATE_DIM = 64
DT_MIN = 1e-3
DT_MAX = 1e-1
N_EXPERTS = 32
TOP_K = 4
D_FF = D_MODEL
SWIGLU_ALPHA = 1.702
SWIGLU_LIMIT = 7.0
ROW_BLOCK = 256
NORM_EPS = 1e-6

kernel_name = 'hybrid_diffattn_s5_moe_encoder'


def rms_norm(x, g, eps=NORM_EPS):
    xf = x.astype(jnp.float32)
    y = xf * lax.rsqrt(jnp.mean(xf * xf, axis=-1, keepdims=True) + eps)
    return (y * g.astype(jnp.float32)).astype(x.dtype)


def rotary(x):
    s = x.shape[1]
    inv_freq = ROPE_THETA ** (-jnp.arange(0, HEAD_DIM, 2, dtype=jnp.float32) / HEAD_DIM)
    ang = jnp.arange(s, dtype=jnp.float32)[:, None] * inv_freq[None, :]
    cos = jnp.cos(ang)[None, :, None, :]
    sin = jnp.sin(ang)[None, :, None, :]
    xf = x.astype(jnp.float32)
    x1, x2 = xf[..., :HEAD_DIM // 2], xf[..., HEAD_DIM // 2:]
    return jnp.concatenate([x1 * cos - x2 * sin, x2 * cos + x1 * sin], axis=-1).astype(x.dtype)


def diff_attention(h, w_qkv, w_o, lam_q1, lam_k1, lam_q2, lam_k2, subln_g, layer_idx):
    b, s, d = h.shape
    nh = N_DIFF_HEADS
    q, k, v = jnp.split(h @ w_qkv, 3, axis=-1)
    q = rotary(q.reshape(b, s, 2 * nh, HEAD_DIM)) * (HEAD_DIM ** -0.5)
    k = rotary(k.reshape(b, s, 2 * nh, HEAD_DIM))
    v = v.reshape(b, s, nh, 2 * HEAD_DIM)
    lambda_init = 0.8 - 0.6 * math.exp(-0.3 * layer_idx)
    f32 = jnp.float32
    lam = (jnp.exp(jnp.sum(lam_q1.astype(f32) * lam_k1.astype(f32)))
           - jnp.exp(jnp.sum(lam_q2.astype(f32) * lam_k2.astype(f32))) + lambda_init)
    q_blocks = jnp.swapaxes(q.reshape(b, s // Q_BLOCK, Q_BLOCK, 2 * nh, HEAD_DIM), 0, 1)

    def block(qb):
        scores = jnp.einsum('bqhd,bkhd->bhqk', qb, k).astype(f32)
        p = jax.nn.softmax(scores, axis=-1).reshape(b, nh, 2, Q_BLOCK, s)
        a = (p[:, :, 0] - lam * p[:, :, 1]).astype(v.dtype)
        return jnp.einsum('bhqk,bkhe->bqhe', a, v)

    o = lax.map(block, q_blocks)
    o = jnp.swapaxes(o, 0, 1).reshape(b, s, nh, 2 * HEAD_DIM)
    o = rms_norm(o, subln_g, SUBLN_EPS) * (1.0 - lambda_init)
    return o.reshape(b, s, d) @ w_o


def _ssm_combine(e1, e2):
    a1r, a1i, b1r, b1i = e1
    a2r, a2i, b2r, b2i = e2
    return (a2r * a1r - a2i * a1i, a2r * a1i + a2i * a1r,
            a2r * b1r - a2i * b1i + b2r, a2r * b1i + a2i * b1r + b2i)


def s5_mixer(h, a_re, a_im, log_dt, b_re, b_im, c_re, c_im, d_skip, w_glu):
    bsz, s, d = h.shape
    f32 = jnp.float32
    ar, ai = a_re.astype(f32), a_im.astype(f32)
    dt = jnp.exp(log_dt.astype(f32))[..., None]
    mag = jnp.exp(ar * dt)
    lr, li = mag * jnp.cos(ai * dt), mag * jnp.sin(ai * dt)
    den = ar * ar + ai * ai
    fr = ((lr - 1.0) * ar + li * ai) / den
    fi = (li * ar - (lr - 1.0) * ai) / den
    br, bi = b_re.astype(f32), b_im.astype(f32)
    bbr = fr[..., None] * br - fi[..., None] * bi
    bbi = fr[..., None] * bi + fi[..., None] * br
    cr, ci = c_re.astype(f32), c_im.astype(f32)

    def one_sequence(u):
        ys = []
        for dirn in range(2):
            bu_r = jnp.einsum('sgh,gph->sgp', u, bbr[dirn])
            bu_i = jnp.einsum('sgh,gph->sgp', u, bbi[dirn])
            lam_r = jnp.broadcast_to(lr[dirn], bu_r.shape)
            lam_i = jnp.broadcast_to(li[dirn], bu_i.shape)
            _, _, xr, xi = lax.associative_scan(_ssm_combine, (lam_r, lam_i, bu_r, bu_i),
                                                reverse=(dirn == 1), axis=0)
            ys.append(jnp.einsum('sgp,ghp->sgh', xr, cr[dirn]) - jnp.einsum('sgp,ghp->sgh', xi, ci[dirn]))
        return ys[0] + ys[1]

    u = h.astype(f32).reshape(bsz, s, N_GROUPS, GROUP_CH)
    y = lax.map(one_sequence, u).reshape(bsz, s, d)
    y = jax.nn.gelu(y + d_skip.astype(f32) * h.astype(f32))
    val, gate = jnp.split(y.astype(h.dtype) @ w_glu, 2, axis=-1)
    return val * jax.nn.sigmoid(gate)


def moe_ffn(h, w_router, b_router, w_gu, b_gu, w_down, b_down):
    bsz, s, d = h.shape
    t_count = bsz * s
    n_assign = t_count * TOP_K
    n_blocks = -(-n_assign // ROW_BLOCK) + N_EXPERTS
    tokens = h.reshape(t_count, d)
    logits = (tokens @ w_router + b_router).astype(jnp.float32)
    top_val, top_idx = lax.top_k(logits, TOP_K)
    gates = jax.nn.softmax(top_val, axis=-1).astype(h.dtype)
    e_flat = top_idx.reshape(-1).astype(jnp.int32)
    g_flat = gates.reshape(-1)
    tok_flat = jnp.arange(n_assign, dtype=jnp.int32) // TOP_K
    order = jnp.argsort(e_flat)
    e_sorted = e_flat[order]
    counts = jnp.zeros((N_EXPERTS,), jnp.int32).at[e_flat].add(1)
    padded = (counts + ROW_BLOCK - 1) // ROW_BLOCK * ROW_BLOCK
    start = jnp.cumsum(counts) - counts
    pad_end = jnp.cumsum(padded)
    pad_start = pad_end - padded
    dest = pad_start[e_sorted] + jnp.arange(n_assign, dtype=jnp.int32) - start[e_sorted]
    row_tok = jnp.full((n_blocks * ROW_BLOCK,), t_count, jnp.int32).at[dest].set(tok_flat[order])
    row_gate = jnp.zeros((n_blocks * ROW_BLOCK,), h.dtype).at[dest].set(g_flat[order])
    block_start = jnp.arange(n_blocks, dtype=jnp.int32) * ROW_BLOCK
    block_expert = jnp.minimum(jnp.searchsorted(pad_end, block_start, side='right'), N_EXPERTS - 1)
    tokens_pad = jnp.concatenate([tokens, jnp.zeros((1, d), tokens.dtype)], axis=0)

    def expert_block(args):
        tok, gate, e = args
        gu = tokens_pad[tok] @ w_gu[e] + b_gu[e]
        g_part, u_part = jnp.split(gu, 2, axis=-1)
        g_part = jnp.minimum(g_part, SWIGLU_LIMIT)
        u_part = jnp.clip(u_part, -SWIGLU_LIMIT, SWIGLU_LIMIT)
        act = (u_part + 1.0) * g_part * jax.nn.sigmoid(SWIGLU_ALPHA * g_part)
        return (act @ w_down[e] + b_down[e]) * gate[:, None]

    y_rows = lax.map(expert_block, (row_tok.reshape(n_blocks, ROW_BLOCK),
                                    row_gate.reshape(n_blocks, ROW_BLOCK), block_expert))
    out = jnp.zeros((t_count + 1, d), h.dtype).at[row_tok].add(y_rows.reshape(-1, d))
    return out[:t_count].reshape(bsz, s, d)


def encoder_trunk(x, c, norm1_g, norm2_g, final_g, w_ada, b_ada,
                  w_qkv, w_o, lam_q1, lam_k1, lam_q2, lam_k2, subln_g,
                  ssm_a_re, ssm_a_im, ssm_log_dt, ssm_b_re, ssm_b_im, ssm_c_re, ssm_c_im,
                  ssm_d, ssm_w_glu,
                  w_router, b_router, w_gu, b_gu, w_down, b_down):
    cond = jax.nn.silu(c)
    for i in range(DEPTH):
        mod = cond @ w_ada[i] + b_ada[i]
        sh1, sc1, gt1, sh2, sc2, gt2 = [m[:, None, :] for m in jnp.split(mod, 6, axis=-1)]
        h = rms_norm(x, norm1_g[i]) * (1.0 + sc1) + sh1
        j = i // N_MIXERS
        if i % N_MIXERS == 0:
            m = diff_attention(h, w_qkv[j], w_o[j], lam_q1[j], lam_k1[j], lam_q2[j], lam_k2[j],
                               subln_g[j], i)
        else:
            m = s5_mixer(h, ssm_a_re[j], ssm_a_im[j], ssm_log_dt[j], ssm_b_re[j], ssm_b_im[j],
                         ssm_c_re[j], ssm_c_im[j], ssm_d[j], ssm_w_glu[j])
        x = x + gt1 * m
        h = rms_norm(x, norm2_g[i]) * (1.0 + sc2) + sh2
        x = x + gt2 * moe_ffn(h, w_router[i], b_router[i], w_gu[i], b_gu[i], w_down[i], b_down[i])
    return rms_norm(x, final_g)


def setup_inputs(seed: int = 0) -> dict:
    key = jax.random.key(seed)
    ks = jax.random.split(key, 40)
    f32 = jnp.float32
    nrm = lambda k, shape, scale: scale * jax.random.normal(k, shape, f32)
    D, G, P, H16 = D_MODEL, N_GROUPS, STATE_DIM, GROUP_CH
    NA, NS, E, F = N_ATT_LAYERS, N_SSM_LAYERS, N_EXPERTS, D_FF
    return {
        'x_prompt': nrm(ks[0], (BATCH, SEQ, D), 1.0),
        'x_sample': nrm(ks[1], (DEC_BATCH, DEC_SEQ, D), 1.0),
        'c_prompt': nrm(ks[2], (BATCH, D), 1.0),
        'c_sample': nrm(ks[3], (DEC_BATCH, D), 1.0),
        'norm1_g': 1.0 + nrm(ks[4], (DEPTH, D), 0.02),
        'norm2_g': 1.0 + nrm(ks[5], (DEPTH, D), 0.02),
        'final_g': 1.0 + nrm(ks[6], (D,), 0.02),
        'w_ada': nrm(ks[7], (DEPTH, D, 6 * D), 0.5 * D ** -0.5),
        'b_ada': nrm(ks[8], (DEPTH, 6 * D), 0.02),
        'w_qkv': nrm(ks[9], (NA, D, 3 * D), D ** -0.5),
        'w_o': nrm(ks[10], (NA, D, D), D ** -0.5),
        'lam_q1': nrm(ks[11], (NA, HEAD_DIM), 0.1),
        'lam_k1': nrm(ks[12], (NA, HEAD_DIM), 0.1),
        'lam_q2': nrm(ks[13], (NA, HEAD_DIM), 0.1),
        'lam_k2': nrm(ks[14], (NA, HEAD_DIM), 0.1),
        'subln_g': 1.0 + nrm(ks[15], (NA, 2 * HEAD_DIM), 0.02),
        'ssm_a_re': -0.5 + nrm(ks[16], (NS, 2, G, P), 0.01),
        'ssm_a_im': jnp.pi * jnp.arange(P, dtype=f32) + nrm(ks[17], (NS, 2, G, P), 0.01),
        'ssm_log_dt': jax.random.uniform(ks[18], (NS, 2, G), f32, math.log(DT_MIN), math.log(DT_MAX)),
        'ssm_b_re': nrm(ks[19], (NS, 2, G, P, H16), (2 * H16) ** -0.5),
        'ssm_b_im': nrm(ks[20], (NS, 2, G, P, H16), (2 * H16) ** -0.5),
        'ssm_c_re': nrm(ks[21], (NS, 2, G, H16, P), P ** -0.5),
        'ssm_c_im': nrm(ks[22], (NS, 2, G, H16, P), P ** -0.5),
        'ssm_d': nrm(ks[23], (NS, D), 1.0),
        'ssm_w_glu': nrm(ks[24], (NS, D, 2 * D), D ** -0.5),
        'w_router': nrm(ks[25], (DEPTH, D, E), D ** -0.5),
        'b_router': nrm(ks[26], (DEPTH, E), 0.01),
        'w_gu': nrm(ks[27], (DEPTH, E, D, 2 * F), D ** -0.5),
        'b_gu': nrm(ks[28], (DEPTH, E, 2 * F), 0.01),
        'w_down': nrm(ks[29], (DEPTH, E, F, D), F ** -0.5),
        'b_down': nrm(ks[30], (DEPTH, E, D), 0.01),
    }


def reference(x_prompt, x_sample, c_prompt, c_sample, norm1_g, norm2_g, final_g, w_ada, b_ada,
              w_qkv, w_o, lam_q1, lam_k1, lam_q2, lam_k2, subln_g,
              ssm_a_re, ssm_a_im, ssm_log_dt, ssm_b_re, ssm_b_im, ssm_c_re, ssm_c_im, ssm_d, ssm_w_glu,
              w_router, b_router, w_gu, b_gu, w_down, b_down):
    y_prompt = encoder_trunk(x_prompt, c_prompt, norm1_g, norm2_g, final_g, w_ada, b_ada,
                             w_qkv, w_o, lam_q1, lam_k1, lam_q2, lam_k2, subln_g,
                             ssm_a_re, ssm_a_im, ssm_log_dt, ssm_b_re, ssm_b_im, ssm_c_re, ssm_c_im,
                             ssm_d, ssm_w_glu, w_router, b_router, w_gu, b_gu, w_down, b_down)
    y_sample = encoder_trunk(x_sample, c_sample, norm1_g, norm2_g, final_g, w_ada, b_ada,
                             w_qkv, w_o, lam_q1, lam_k1, lam_q2, lam_k2, subln_g,
                             ssm_a_re, ssm_a_im, ssm_log_dt, ssm_b_re, ssm_b_im, ssm_c_re, ssm_c_im,
                             ssm_d, ssm_w_glu, w_router, b_router, w_gu, b_gu, w_down, b_down)
    return (y_prompt, y_sample)
```

```python
import functools
import math

import jax
import jax.numpy as jnp
from jax import lax
from jax.experimental import pallas as pl
from jax.experimental.pallas import tpu as pltpu

F32 = jnp.float32
BF16 = jnp.bfloat16
HIGHEST = lax.Precision.HIGHEST

N_DIFF_HEADS = 8
HEAD_DIM = 64
ROPE_THETA = 10000.0
SUBLN_EPS = 1e-5
NORM_EPS = 1e-6
GROUP_CH = 16
STATE_DIM = 64
N_EXPERTS = 32
TOP_K = 4
SWIGLU_ALPHA = 1.702
SWIGLU_LIMIT = 7.0

LANES = 128
SUBLANES = 8
VMEM_LIMIT_BYTES = 56 * 1024 * 1024

TOKEN_TILE = 512
ATT_Q_TILE = 256
ATT_KV_TILE = 512
SSM_CHUNK = 16
SSM_GROUP_TILE = 8
SSM_ROW_TILE = 512
SSM_SCAN_TILE = 64
EXPERT_ROW_TILE = 512
FF_CHUNK = 512
COPY_ROWS = 2048


def _cparams(*sem):
    return pltpu.CompilerParams(dimension_semantics=sem, vmem_limit_bytes=VMEM_LIMIT_BYTES)


def _rms_mod(x, g, scale, shift):
    ms = jnp.mean(x * x, axis=-1, keepdims=True)
    return x * lax.rsqrt(ms + NORM_EPS) * g * (1.0 + scale) + shift


def _ada_kernel(c_ref, w_ref, b_ref, o_ref):
    c = c_ref[...]
    cond = c * jax.nn.sigmoid(c)
    o_ref[0] = jnp.dot(cond, w_ref[0], precision=HIGHEST, preferred_element_type=F32) + b_ref[0]


def _ada(c_pad, w_ada, b_ada):
    depth, d, n6 = w_ada.shape
    rows = c_pad.shape[0]
    tn = 1536 if n6 % 1536 == 0 else n6
    return pl.pallas_call(
        _ada_kernel,
        grid=(depth, n6 // tn),
        in_specs=[pl.BlockSpec((rows, d), lambda i, j: (0, 0)),
                  pl.BlockSpec((1, d, tn), lambda i, j: (i, 0, j)),
                  pl.BlockSpec((1, 1, tn), lambda i, j: (i, 0, j))],
        out_specs=pl.BlockSpec((1, rows, tn), lambda i, j: (i, 0, j)),
        out_shape=jax.ShapeDtypeStruct((depth, rows, n6), F32),
        compiler_params=_cparams("arbitrary", "arbitrary"),
        name="ada_mod",
    )(c_pad, w_ada, b_ada.reshape(depth, 1, n6))


def _qkv_kernel(pos_ref, x_ref, mod_ref, g_ref, w_ref, cos_ref, sin_ref, q_ref, k_ref, v_ref):
    del pos_ref
    d = x_ref.shape[1]
    h = _rms_mod(x_ref[...], g_ref[...], mod_ref[0, 1:2, :], mod_ref[0, 0:1, :])
    qkv = jnp.dot(h.astype(BF16), w_ref[...], preferred_element_type=F32)
    cos = cos_ref[...]
    sin = sin_ref[...]
    lane = lax.broadcasted_iota(jnp.int32, cos.shape, 1)
    first_half = (lane % HEAD_DIM) < (HEAD_DIM // 2)

    def rotary(t):
        partner = jnp.where(first_half,
                            pltpu.roll(t, LANES - HEAD_DIM // 2, 1),
                            pltpu.roll(t, HEAD_DIM // 2, 1))
        return t * cos + partner * sin

    for j in range(d // LANES):
        sl = slice(j * LANES, (j + 1) * LANES)
        q_ref[:, sl] = (rotary(qkv[:, sl]) * (HEAD_DIM ** -0.5)).astype(BF16)
        k_ref[:, sl] = rotary(qkv[:, d + j * LANES:d + (j + 1) * LANES]).astype(BF16)
    v_ref[...] = qkv[:, 2 * d:].astype(BF16)


def _qkv(x, modl, g, w_bf, cos_t, sin_t, pos_blk, seg_len, tm):
    t, d = x.shape
    per_seg = seg_len // tm
    tok = lambda i, p: (i, 0)
    out = jax.ShapeDtypeStruct((t, d), BF16)
    return pl.pallas_call(
        _qkv_kernel,
        grid_spec=pltpu.PrefetchScalarGridSpec(
            num_scalar_prefetch=1, grid=(t // tm,),
            in_specs=[pl.BlockSpec((tm, d), tok),
                      pl.BlockSpec((1, SUBLANES, d), lambda i, p: (i // per_seg, 0, 0)),
                      pl.BlockSpec((1, d), lambda i, p: (0, 0)),
                      pl.BlockSpec((d, 3 * d), lambda i, p: (0, 0)),
                      pl.BlockSpec((tm, LANES), lambda i, p: (p[i], 0)),
                      pl.BlockSpec((tm, LANES), lambda i, p: (p[i], 0))],
            out_specs=[pl.BlockSpec((tm, d), tok)] * 3),
        out_shape=[out, out, out],
        compiler_params=_cparams("arbitrary"),
        name="qkv_rotary",
    )(pos_blk, x, modl, g, w_bf, cos_t, sin_t)


def _flash_kernel(sa_ref, sb_ref, two_ref, q_ref, ka_ref, va_ref, kb_ref, vb_ref, lam_ref, g_ref,
                  o_ref, qs_ref, m_ref, l_ref, acc_ref, *, tq, tk, lambda_init):
    del sa_ref, sb_ref
    seg = pl.program_id(0)
    q = q_ref[...]
    lane = lax.broadcasted_iota(jnp.int32, q.shape, 1)
    zero = jnp.zeros_like(q)
    qs_ref[0:tq, :] = jnp.where(lane < HEAD_DIM, q, zero)
    qs_ref[tq:2 * tq, :] = jnp.where(lane >= HEAD_DIM, q, zero)
    m_ref[...] = jnp.full(m_ref.shape, -jnp.inf, F32)
    l_ref[...] = jnp.zeros(l_ref.shape, F32)
    acc_ref[...] = jnp.zeros(acc_ref.shape, F32)

    def sweep(k_ref, v_ref):
        def body(j, carry):
            off = pl.multiple_of(j * tk, tk)
            kb = k_ref[pl.ds(off, tk), :]
            vb = v_ref[pl.ds(off, tk), :]
            s = lax.dot_general(qs_ref[...], kb, (((1,), (1,)), ((), ())),
                                preferred_element_type=F32)
            m_prev = m_ref[...]
            m_new = jnp.maximum(m_prev, jnp.max(s, axis=-1, keepdims=True))
            p = jnp.exp(s - m_new)
            alpha = jnp.exp(m_prev - m_new)
            l_ref[...] = alpha * l_ref[...] + jnp.sum(p, axis=-1, keepdims=True)
            acc_ref[...] = alpha * acc_ref[...] + jnp.dot(p.astype(BF16), vb,
                                                          preferred_element_type=F32)
            m_ref[...] = m_new
            return carry
        lax.fori_loop(0, k_ref.shape[0] // tk, body, 0)

    sweep(ka_ref, va_ref)

    @pl.when(two_ref[seg] == 1)
    def _():
        sweep(kb_ref, vb_ref)

    o1 = acc_ref[0:tq, :] / l_ref[0:tq, :]
    o2 = acc_ref[tq:2 * tq, :] / l_ref[tq:2 * tq, :]
    lam = (jnp.exp(jnp.sum(lam_ref[0:1, :] * lam_ref[1:2, :], axis=-1, keepdims=True))
           - jnp.exp(jnp.sum(lam_ref[2:3, :] * lam_ref[3:4, :], axis=-1, keepdims=True))
           + lambda_init)
    o = o1 - lam * o2
    ms = jnp.mean(o * o, axis=-1, keepdims=True)
    o = o * lax.rsqrt(ms + SUBLN_EPS) * g_ref[...] * (1.0 - lambda_init)
    o_ref[...] = o.astype(BF16)


def _flash(q, k, v, lam_pack, subln_g, seg_a, seg_b, two, seg_len, lambda_init):
    t, d = q.shape
    nseg = t // seg_len
    tq = min(ATT_Q_TILE, seg_len)
    tk = min(ATT_KV_TILE, seg_len)
    nq = seg_len // tq
    width = 2 * HEAD_DIM
    kern = functools.partial(_flash_kernel, tq=tq, tk=tk, lambda_init=lambda_init)
    kv_a = pl.BlockSpec((seg_len, width), lambda s, h, i, sa, sb, tw: (sa[s], h))
    kv_b = pl.BlockSpec((seg_len, width), lambda s, h, i, sa, sb, tw: (sb[s], h))
    qo = pl.BlockSpec((tq, width), lambda s, h, i, sa, sb, tw: (s * nq + i, h))
    return pl.pallas_call(
        kern,
        grid_spec=pltpu.PrefetchScalarGridSpec(
            num_scalar_prefetch=3, grid=(nseg, N_DIFF_HEADS, nq),
            in_specs=[qo, kv_a, kv_a, kv_b, kv_b,
                      pl.BlockSpec((SUBLANES, width), lambda s, h, i, sa, sb, tw: (0, 0)),
                      pl.BlockSpec((1, width), lambda s, h, i, sa, sb, tw: (0, 0))],
            out_specs=qo,
            scratch_shapes=[pltpu.VMEM((2 * tq, width), BF16),
                            pltpu.VMEM((2 * tq, 1), F32),
                            pltpu.VMEM((2 * tq, 1), F32),
                            pltpu.VMEM((2 * tq, width), F32)]),
        out_shape=jax.ShapeDtypeStruct((t, d), BF16),
        compiler_params=_cparams("arbitrary", "arbitrary", "arbitrary"),
        name="diff_flash_attention",
    )(seg_a, seg_b, two, q, k, v, k, v, lam_pack, subln_g)


def _proj_res_kernel(a_ref, w_ref, x_ref, mod_ref, o_ref, *, gate_row):
    y = jnp.dot(a_ref[...], w_ref[...], preferred_element_type=F32)
    o_ref[...] = x_ref[...] + mod_ref[0, gate_row:gate_row + 1, :] * y


def _proj_res(a, w_bf, x, modl, seg_len, tm, gate_row):
    t, d = x.shape
    per_seg = seg_len // tm
    return pl.pallas_call(
        functools.partial(_proj_res_kernel, gate_row=gate_row),
        grid=(t // tm,),
        in_specs=[pl.BlockSpec((tm, a.shape[1]), lambda i: (i, 0)),
                  pl.BlockSpec(w_bf.shape, lambda i: (0, 0)),
                  pl.BlockSpec((tm, d), lambda i: (i, 0)),
                  pl.BlockSpec((1, SUBLANES, d), lambda i: (i // per_seg, 0, 0))],
        out_specs=pl.BlockSpec((tm, d), lambda i: (i, 0)),
        out_shape=jax.ShapeDtypeStruct((t, d), F32),
        compiler_params=_cparams("arbitrary"),
        name="proj_residual",
    )(a, w_bf, x, modl)


def _norm_bf16_kernel(x_ref, mod_ref, g_ref, o_ref):
    o_ref[...] = _rms_mod(x_ref[...], g_ref[...], mod_ref[0, 1:2, :], mod_ref[0, 0:1, :]).astype(BF16)


def _norm_bf16(x, modl, g, seg_len, tm):
    t, d = x.shape
    per_seg = seg_len // tm
    return pl.pallas_call(
        _norm_bf16_kernel,
        grid=(t // tm,),
        in_specs=[pl.BlockSpec((tm, d), lambda i: (i, 0)),
                  pl.BlockSpec((1, SUBLANES, d), lambda i: (i // per_seg, 0, 0)),
                  pl.BlockSpec((1, d), lambda i: (0, 0))],
        out_specs=pl.BlockSpec((tm, d), lambda i: (i, 0)),
        out_shape=jax.ShapeDtypeStruct((t, d), BF16),
        compiler_params=_cparams("arbitrary"),
        name="s5_prenorm",
    )(x, modl, g)


def _s5_tables(a_re, a_im, log_dt, b_re, b_im, c_re, c_im):
    f32 = F32
    L = SSM_CHUNK
    ar, ai = a_re.astype(f32), a_im.astype(f32)
    n_g, n_p = ar.shape[1], ar.shape[2]
    n_h = b_re.shape[-1]
    dt = jnp.exp(log_dt.astype(f32))[..., None]
    mag = jnp.exp(ar * dt)
    lr, li = mag * jnp.cos(ai * dt), mag * jnp.sin(ai * dt)
    den = ar * ar + ai * ai
    fr = ((lr - 1.0) * ar + li * ai) / den
    fi = (li * ar - (lr - 1.0) * ai) / den
    br, bi = b_re.astype(f32), b_im.astype(f32)
    bbr = fr[..., None] * br - fi[..., None] * bi
    bbi = fr[..., None] * bi + fi[..., None] * br
    cr, ci = c_re.astype(f32), c_im.astype(f32)
    kk = jnp.arange(L + 1, dtype=f32)
    pmag = jnp.exp(ar[..., None] * dt[..., None] * kk)
    pang = ai[..., None] * dt[..., None] * kk
    pr, pi = pmag * jnp.cos(pang), pmag * jnp.sin(pang)
    er = cr[..., None] * pr[:, :, None] - ci[..., None] * pi[:, :, None]
    ei = cr[..., None] * pi[:, :, None] + ci[..., None] * pr[:, :, None]
    klag = (jnp.einsum('dgopk,dgpi->dgkoi', er[..., :L], bbr, precision=HIGHEST)
            - jnp.einsum('dgopk,dgpi->dgkoi', ei[..., :L], bbi, precision=HIGHEST))
    kf, kb = klag[0], klag[1]
    kfull = jnp.concatenate([kb[:, 1:][:, ::-1], (kf[:, 0] + kb[:, 0])[:, None], kf[:, 1:]], axis=1)
    s_idx = jnp.arange(L)[:, None]
    t_idx = jnp.arange(L)[None, :]
    w = kfull[:, t_idx - s_idx + L - 1]
    w = jnp.transpose(w, (0, 1, 4, 2, 3)).reshape(n_g, L * n_h, L * n_h)
    def in_to_state(d, pw_idx):
        pwr = jnp.take(pr[d], pw_idx, axis=-1)
        pwi = jnp.take(pi[d], pw_idx, axis=-1)
        re = pwr[..., None] * bbr[d][:, :, None, :] - pwi[..., None] * bbi[d][:, :, None, :]
        im = pwr[..., None] * bbi[d][:, :, None, :] + pwi[..., None] * bbr[d][:, :, None, :]
        m = jnp.concatenate([re, im], axis=1)
        return jnp.transpose(m, (0, 2, 3, 1)).reshape(n_g, L * n_h, 2 * n_p)
    pcat = jnp.concatenate([in_to_state(0, L - 1 - jnp.arange(L)), in_to_state(1, jnp.arange(L))], axis=-1)
    def state_to_out(d, pw_idx):
        e_r = jnp.take(er[d], pw_idx, axis=-1)
        e_i = jnp.take(ei[d], pw_idx, axis=-1)
        m = jnp.concatenate([e_r, -e_i], axis=2)
        return jnp.transpose(m, (0, 2, 3, 1)).reshape(n_g, 2 * n_p, L * n_h)
    qcat = jnp.concatenate([state_to_out(0, jnp.arange(L) + 1), state_to_out(1, L - jnp.arange(L))], axis=1)

    def scan_mult(d):
        a_r, a_i = pr[d][..., L], pi[d][..., L]
        m1 = jnp.concatenate([a_r, a_r], axis=-1)
        m2 = jnp.concatenate([-a_i, a_i], axis=-1)
        lay = lambda m: jnp.transpose(m.reshape(n_g // SSM_GROUP_TILE, SSM_GROUP_TILE, 2 * n_p),
                                      (1, 0, 2)).reshape(SSM_GROUP_TILE, -1)
        return lay(m1), lay(m2)
    af1, af2 = scan_mult(0)
    ab1, ab2 = scan_mult(1)
    return w.astype(BF16), pcat.astype(BF16), qcat.astype(BF16), (af1, af2, ab1, ab2)


def _s5_states_kernel(u_ref, p_ref, sf_ref, sb_ref):
    half = sf_ref.shape[-1]
    for r in range(u_ref.shape[0]):
        s = jnp.dot(u_ref[r], p_ref[r], preferred_element_type=F32)
        sf_ref[:, r, :] = s[:, :half]
        sb_ref[:, r, :] = s[:, half:]


def _s5_states(ug, pcat, tc):
    n_g, nc, kdim = ug.shape
    gt = SSM_GROUP_TILE
    half = pcat.shape[-1] // 2
    out = jax.ShapeDtypeStruct((nc, gt, (n_g // gt) * half), F32)
    return pl.pallas_call(
        _s5_states_kernel,
        grid=(n_g // gt, nc // tc),
        in_specs=[pl.BlockSpec((gt, tc, kdim), lambda j, c: (j, c, 0)),
                  pl.BlockSpec((gt, kdim, 2 * half), lambda j, c: (j, 0, 0))],
        out_specs=[pl.BlockSpec((tc, gt, half), lambda j, c: (c, 0, j))] * 2,
        out_shape=[out, out],
        compiler_params=_cparams("arbitrary", "arbitrary"),
        name="s5_chunk_states",
    )(ug, pcat)


def _s5_scan_kernel(rf_ref, rb_ref, sf_ref, sb_ref, af1_ref, af2_ref, ab1_ref, ab2_ref,
                    xf_ref, xb_ref, cf_ref, cb_ref, *, tcs, nblk):
    i = pl.program_id(0)

    @pl.when(i == 0)
    def _():
        cf_ref[...] = jnp.zeros(cf_ref.shape, F32)
        cb_ref[...] = jnp.zeros(cb_ref.shape, F32)

    width = cf_ref.shape[-1]

    def swap(x):
        return jnp.concatenate(
            [pltpu.roll(x[:, j * LANES:(j + 1) * LANES], LANES // 2, 1) for j in range(width // LANES)],
            axis=1)

    af1, af2 = af1_ref[...], af2_ref[...]
    ab1, ab2 = ab1_ref[...], ab2_ref[...]

    def fwd(r, carry):
        carry = carry * (1 - rf_ref[i * tcs + r]).astype(F32)
        xf_ref[r] = carry
        return af1 * carry + af2 * swap(carry) + sf_ref[r]

    cf_ref[...] = lax.fori_loop(0, tcs, fwd, cf_ref[...])

    def bwd(rr, carry):
        r = tcs - 1 - rr
        carry = carry * (1 - rb_ref[(nblk - 1 - i) * tcs + r]).astype(F32)
        xb_ref[r] = carry
        return ab1 * carry + ab2 * swap(carry) + sb_ref[r]

    cb_ref[...] = lax.fori_loop(0, tcs, bwd, cb_ref[...])


def _s5_scan(sf, sb, mults, reset_f, reset_b, tcs):
    nc, gt, width = sf.shape
    nblk = nc // tcs
    fblk = pl.BlockSpec((tcs, gt, width), lambda i, a, b: (i, 0, 0))
    bblk = pl.BlockSpec((tcs, gt, width), lambda i, a, b: (nblk - 1 - i, 0, 0))
    tab = pl.BlockSpec((gt, width), lambda i, a, b: (0, 0))
    out = jax.ShapeDtypeStruct((nc, gt, width), F32)
    return pl.pallas_call(
        functools.partial(_s5_scan_kernel, tcs=tcs, nblk=nblk),
        grid_spec=pltpu.PrefetchScalarGridSpec(
            num_scalar_prefetch=2, grid=(nblk,),
            in_specs=[fblk, bblk, tab, tab, tab, tab],
            out_specs=[fblk, bblk],
            scratch_shapes=[pltpu.VMEM((gt, width), F32), pltpu.VMEM((gt, width), F32)]),
        out_shape=[out, out],
        compiler_params=_cparams("arbitrary"),
        name="s5_chunk_scan",
    )(reset_f, reset_b, sf, sb, *mults)


def _s5_out_kernel(u_ref, xf_ref, xb_ref, w_ref, q_ref, y_ref):
    for r in range(u_ref.shape[0]):
        xcat = jnp.concatenate([xf_ref[:, r, :], xb_ref[:, r, :]], axis=-1).astype(BF16)
        y_ref[r] = (jnp.dot(u_ref[r], w_ref[r], preferred_element_type=F32)
                    + jnp.dot(xcat, q_ref[r], preferred_element_type=F32))


def _s5_out(ug, xf, xb, w, qcat, tc):
    n_g, nc, kdim = ug.shape
    gt = SSM_GROUP_TILE
    half = xf.shape[-1] // (n_g // gt)
    xblk = pl.BlockSpec((tc, gt, half), lambda j, c: (c, 0, j))
    return pl.pallas_call(
        _s5_out_kernel,
        grid=(n_g // gt, nc // tc),
        in_specs=[pl.BlockSpec((gt, tc, kdim), lambda j, c: (j, c, 0)), xblk, xblk,
                  pl.BlockSpec((gt, kdim, kdim), lambda j, c: (j, 0, 0)),
                  pl.BlockSpec((gt, 2 * half, kdim), lambda j, c: (j, 0, 0))],
        out_specs=pl.BlockSpec((gt, tc, kdim), lambda j, c: (j, c, 0)),
        out_shape=jax.ShapeDtypeStruct((n_g, nc, kdim), F32),
        compiler_params=_cparams("arbitrary", "arbitrary"),
        name="s5_chunk_outputs",
    )(ug, xf, xb, w, qcat)


def _s5_glu_kernel(x_ref, y_ref, mod_ref, g_ref, dskip_ref, w_ref, o_ref):
    d = x_ref.shape[1]
    x = x_ref[...]
    h = _rms_mod(x, g_ref[...], mod_ref[0, 1:2, :], mod_ref[0, 0:1, :])
    z = jax.nn.gelu(y_ref[...] + dskip_ref[...] * h)
    vg = jnp.dot(z.astype(BF16), w_ref[...], preferred_element_type=F32)
    m = vg[:, :d] * jax.nn.sigmoid(vg[:, d:])
    o_ref[...] = x + mod_ref[0, 2:3, :] * m


def _s5_glu(x, y, modl, g, d_skip, w_bf, seg_len, tm):
    t, d = x.shape
    per_seg = seg_len // tm
    tok = pl.BlockSpec((tm, d), lambda i: (i, 0))
    vec = pl.BlockSpec((1, d), lambda i: (0, 0))
    return pl.pallas_call(
        _s5_glu_kernel,
        grid=(t // tm,),
        in_specs=[tok, tok, pl.BlockSpec((1, SUBLANES, d), lambda i: (i // per_seg, 0, 0)), vec, vec,
                  pl.BlockSpec((d, 2 * d), lambda i: (0, 0))],
        out_specs=tok,
        out_shape=jax.ShapeDtypeStruct((t, d), F32),
        compiler_params=_cparams("arbitrary"),
        name="s5_glu_residual",
    )(x, y, modl, g, d_skip, w_bf)


def _s5_layer(x, modl, g, params, seg_len, seq_first, seq_last, tm):
    a_re, a_im, log_dt, b_re, b_im, c_re, c_im, d_skip, w_glu = params
    t, d = x.shape
    L = SSM_CHUNK
    n_g = d // GROUP_CH
    nc = t // L
    w, pcat, qcat, mults = _s5_tables(a_re, a_im, log_dt, b_re, b_im, c_re, c_im)
    hb = _norm_bf16(x, modl, g, seg_len, tm)
    ug = hb.reshape(nc, L, n_g, GROUP_CH).transpose(2, 0, 1, 3).reshape(n_g, nc, L * GROUP_CH)
    tc = math.gcd(nc, SSM_ROW_TILE)
    sf, sb = _s5_states(ug, pcat, tc)
    chunks_per_seg = seg_len // L
    reset_f = jnp.repeat(jnp.asarray(seq_first, jnp.int32), chunks_per_seg) * (
        jnp.tile(jnp.arange(chunks_per_seg) == 0, len(seq_first))).astype(jnp.int32)
    reset_b = jnp.repeat(jnp.asarray(seq_last, jnp.int32), chunks_per_seg) * (
        jnp.tile(jnp.arange(chunks_per_seg) == chunks_per_seg - 1, len(seq_last))).astype(jnp.int32)
    xf, xb = _s5_scan(sf, sb, mults, reset_f, reset_b, math.gcd(nc, SSM_SCAN_TILE))
    yg = _s5_out(ug, xf, xb, w, qcat, tc)
    y = yg.reshape(n_g, nc, L, GROUP_CH).transpose(1, 2, 0, 3).reshape(t, d)
    return _s5_glu(x, y, modl, g, d_skip.reshape(1, d), w_glu.astype(BF16), seg_len, tm)


def _router_kernel(x_ref, mod_ref, g_ref, w_ref, b_ref, h_ref, idx_ref, gate_ref, rank_ref, cnt_ref,
                   carry_ref):
    tm = x_ref.shape[0]
    n_e = w_ref.shape[0]

    @pl.when(pl.program_id(0) == 0)
    def _():
        carry_ref[...] = jnp.zeros(carry_ref.shape, F32)

    h = _rms_mod(x_ref[...], g_ref[...], mod_ref[0, 4:5, :], mod_ref[0, 3:4, :])
    h_ref[...] = h
    logits = lax.dot_general(w_ref[...], h, (((1,), (1,)), ((), ())), precision=HIGHEST,
                             preferred_element_type=F32) + b_ref[...]
    e_iota = lax.broadcasted_iota(jnp.int32, (n_e, tm), 0)
    work = logits
    chosen = jnp.zeros((n_e, tm), F32)
    vals, idxs = [], []
    for _ in range(TOP_K):
        m = jnp.max(work, axis=0, keepdims=True)
        ix = jnp.min(jnp.where(work == m, e_iota, n_e), axis=0, keepdims=True)
        hit = e_iota == ix
        work = jnp.where(hit, -jnp.inf, work)
        chosen = jnp.where(hit, 1.0, chosen)
        vals.append(m)
        idxs.append(ix)
    v = jnp.concatenate(vals, axis=0)
    ex = jnp.exp(v - v[0:1])
    gate_ref[...] = ex / jnp.sum(ex, axis=0, keepdims=True)
    idx_ref[...] = jnp.concatenate(idxs, axis=0)
    row = lax.broadcasted_iota(jnp.int32, (tm, tm), 0)
    col = lax.broadcasted_iota(jnp.int32, (tm, tm), 1)
    before = (row < col).astype(BF16)
    cum = jnp.dot(chosen.astype(BF16), before, preferred_element_type=F32) + carry_ref[...]
    ranks = [jnp.sum(jnp.where(e_iota == ix, cum, 0.0), axis=0, keepdims=True) for ix in idxs]
    rank_ref[...] = jnp.concatenate(ranks, axis=0).astype(jnp.int32)
    carry_ref[...] = carry_ref[...] + jnp.sum(chosen, axis=1, keepdims=True)
    cnt_ref[...] = jnp.broadcast_to(carry_ref[...], cnt_ref.shape).astype(jnp.int32)


def _router(x, modl, g, w_router_t, b_router, seg_len, tm):
    t, d = x.shape
    n_e = w_router_t.shape[0]
    per_seg = seg_len // tm
    sel = pl.BlockSpec((TOP_K, tm), lambda i: (0, i))
    return pl.pallas_call(
        _router_kernel,
        grid=(t // tm,),
        in_specs=[pl.BlockSpec((tm, d), lambda i: (i, 0)),
                  pl.BlockSpec((1, SUBLANES, d), lambda i: (i // per_seg, 0, 0)),
                  pl.BlockSpec((1, d), lambda i: (0, 0)),
                  pl.BlockSpec((n_e, d), lambda i: (0, 0)),
                  pl.BlockSpec((n_e, 1), lambda i: (0, 0))],
        out_specs=[pl.BlockSpec((tm, d), lambda i: (i, 0)), sel, sel, sel,
                   pl.BlockSpec((n_e, LANES), lambda i: (0, 0))],
        out_shape=[jax.ShapeDtypeStruct((t, d), F32),
                   jax.ShapeDtypeStruct((TOP_K, t), jnp.int32),
                   jax.ShapeDtypeStruct((TOP_K, t), F32),
                   jax.ShapeDtypeStruct((TOP_K, t), jnp.int32),
                   jax.ShapeDtypeStruct((n_e, LANES), jnp.int32)],
        scratch_shapes=[pltpu.VMEM((n_e, 1), F32)],
        compiler_params=_cparams("arbitrary"),
        name="moe_router",
    )(x, modl, g, w_router_t, b_router)


def _row_copy_kernel(src_idx_ref, dst_idx_ref, src_hbm, *rest, n):
    dst_hbm, sem = rest[-2], rest[-1]

    def row_copy(s, d):
        return pltpu.make_async_copy(src_hbm.at[pl.ds(s, 1), :], dst_hbm.at[pl.ds(d, 1), :], sem)

    def issue(j, carry):
        row_copy(src_idx_ref[0, 0, j], dst_idx_ref[0, 0, j]).start()
        return carry

    lax.fori_loop(0, n, issue, 0)

    def drain(j, carry):
        row_copy(0, 0).wait()
        return carry

    lax.fori_loop(0, n, drain, 0)


def _row_copy(src, src_idx, dst_idx, dst_rows, dst_init=None):
    n_total = src_idx.shape[0]
    n = math.gcd(n_total, COPY_ROWS)
    steps = n_total // n
    idx_spec = pl.BlockSpec((1, 1, n), lambda i: (i, 0, 0), memory_space=pltpu.SMEM)
    any_spec = pl.BlockSpec(memory_space=pl.ANY)
    extra = () if dst_init is None else (dst_init,)
    return pl.pallas_call(
        functools.partial(_row_copy_kernel, n=n),
        grid=(steps,),
        in_specs=[idx_spec, idx_spec, any_spec] + [any_spec] * len(extra),
        out_specs=any_spec,
        out_shape=jax.ShapeDtypeStruct((dst_rows, src.shape[1]), src.dtype),
        scratch_shapes=[pltpu.SemaphoreType.DMA(())],
        input_output_aliases={3: 0} if extra else {},
        compiler_params=_cparams("arbitrary"),
        name="row_copy",
    )(src_idx.reshape(steps, 1, n), dst_idx.reshape(steps, 1, n), src, *extra)


def _expert_kernel(be_ref, bi_ref, nu_ref, x_ref, wgu_ref, bgu_ref, wd_ref, bd_ref, y_ref):
    del be_ref, bi_ref
    n_ff = wd_ref.shape[1]

    @pl.when(pl.program_id(0) < nu_ref[0])
    def _():
        x = x_ref[...].astype(BF16)
        acc = jnp.zeros(y_ref.shape, F32) + bd_ref[0]
        for c in range(n_ff // FF_CHUNK):
            lo = c * FF_CHUNK
            g_part = (jnp.dot(x, wgu_ref[0, :, lo:lo + FF_CHUNK], preferred_element_type=F32)
                      + bgu_ref[0, :, lo:lo + FF_CHUNK])
            u_part = (jnp.dot(x, wgu_ref[0, :, n_ff + lo:n_ff + lo + FF_CHUNK], preferred_element_type=F32)
                      + bgu_ref[0, :, n_ff + lo:n_ff + lo + FF_CHUNK])
            g_part = jnp.minimum(g_part, SWIGLU_LIMIT)
            u_part = jnp.clip(u_part, -SWIGLU_LIMIT, SWIGLU_LIMIT)
            act = (u_part + 1.0) * g_part * jax.nn.sigmoid(SWIGLU_ALPHA * g_part)
            acc = acc + jnp.dot(act.astype(BF16), wd_ref[0, lo:lo + FF_CHUNK, :],
                                preferred_element_type=F32)
        y_ref[...] = acc


def _experts(xs, blk_expert, blk_idx, n_used, w_gu_bf, b_gu, w_down_bf, b_down, tme):
    nr, d = xs.shape
    n_e, _, two_f = w_gu_bf.shape
    n_ff = two_f // 2
    nblk = nr // tme
    rows = pl.BlockSpec((tme, d), lambda b, be, bi, nu: (bi[b], 0))
    return pl.pallas_call(
        _expert_kernel,
        grid_spec=pltpu.PrefetchScalarGridSpec(
            num_scalar_prefetch=3, grid=(nblk,),
            in_specs=[rows,
                      pl.BlockSpec((1, d, two_f), lambda b, be, bi, nu: (be[b], 0, 0)),
                      pl.BlockSpec((1, 1, two_f), lambda b, be, bi, nu: (be[b], 0, 0)),
                      pl.BlockSpec((1, n_ff, d), lambda b, be, bi, nu: (be[b], 0, 0)),
                      pl.BlockSpec((1, 1, d), lambda b, be, bi, nu: (be[b], 0, 0))],
            out_specs=rows),
        out_shape=jax.ShapeDtypeStruct((nr, d), F32),
        compiler_params=_cparams("arbitrary"),
        name="moe_experts",
    )(blk_expert, blk_idx, n_used, xs, w_gu_bf, b_gu.reshape(n_e, 1, two_f), w_down_bf,
      b_down.reshape(n_e, 1, d))


def _combine_kernel(x_ref, yc_ref, gate_ref, mod_ref, g_ref, o_ref, *, final_norm):
    gates = gate_ref[...]
    moe = gates[:, 0:1] * yc_ref[0]
    for k in range(1, TOP_K):
        moe = moe + gates[:, k:k + 1] * yc_ref[k]
    x = x_ref[...] + mod_ref[0, 5:6, :] * moe
    if final_norm:
        ms = jnp.mean(x * x, axis=-1, keepdims=True)
        x = x * lax.rsqrt(ms + NORM_EPS) * g_ref[...]
    o_ref[...] = x


def _combine(x, yc, gates, modl, final_g, seg_len, tm, final_norm, row0=0, rows=None):
    t, d = x.shape
    rows = t if rows is None else rows
    per_seg = seg_len // tm
    b0 = row0 // tm
    return pl.pallas_call(
        functools.partial(_combine_kernel, final_norm=final_norm),
        grid=(rows // tm,),
        in_specs=[pl.BlockSpec((tm, d), lambda i: (i + b0, 0)),
                  pl.BlockSpec((TOP_K, tm, d), lambda i: (0, i + b0, 0)),
                  pl.BlockSpec((tm, TOP_K), lambda i: (i + b0, 0)),
                  pl.BlockSpec((1, SUBLANES, d), lambda i: ((i + b0) // per_seg, 0, 0)),
                  pl.BlockSpec((1, d), lambda i: (0, 0))],
        out_specs=pl.BlockSpec((tm, d), lambda i: (i, 0)),
        out_shape=jax.ShapeDtypeStruct((rows, d), F32),
        compiler_params=_cparams("arbitrary"),
        name="moe_combine",
    )(x, yc, gates, modl, final_g)


def _moe_dispatch_and_experts(x, modl, g, w_router, b_router, w_gu, b_gu, w_down, b_down, seg_len, tm):
    t, d = x.shape
    n_e = w_router.shape[1]
    tme = math.gcd(t * TOP_K, EXPERT_ROW_TILE)
    h, idx_t, gate_t, rank_t, cnt = _router(x, modl, g, w_router.T, b_router.reshape(n_e, 1), seg_len, tm)
    counts = cnt[:, 0]
    padded = (counts + tme - 1) // tme * tme
    pad_end = jnp.cumsum(padded)
    pad_start = pad_end - padded
    nblk = (t * TOP_K) // tme + n_e
    n_used = (pad_end[-1] // tme).astype(jnp.int32).reshape(1)
    blk_idx = jnp.minimum(jnp.arange(nblk, dtype=jnp.int32), n_used[0] - 1)
    blk_expert = jnp.minimum(jnp.searchsorted(pad_end, blk_idx * tme, side='right'), n_e - 1).astype(jnp.int32)
    dest = (pad_start[idx_t] + rank_t).astype(jnp.int32)
    dest_tk = dest.T.reshape(-1)
    tok_tk = jnp.repeat(jnp.arange(t, dtype=jnp.int32), TOP_K)
    slot_tk = (jnp.tile(jnp.arange(TOP_K, dtype=jnp.int32) * t, t) + tok_tk)
    xs = _row_copy(h, tok_tk, dest_tk, nblk * tme, jnp.zeros((nblk * tme, d), F32))
    ys = _experts(xs, blk_expert, blk_idx, n_used, w_gu.astype(BF16), b_gu, w_down.astype(BF16), b_down, tme)
    yc = _row_copy(ys, dest_tk, slot_tk, TOP_K * t)
    return yc.reshape(TOP_K, t, d), gate_t.T


def kernel(x_prompt, x_sample, c_prompt, c_sample, norm1_g, norm2_g, final_g, w_ada, b_ada, w_qkv, w_o,
           lam_q1, lam_k1, lam_q2, lam_k2, subln_g, ssm_a_re, ssm_a_im, ssm_log_dt, ssm_b_re, ssm_b_im,
           ssm_c_re, ssm_c_im, ssm_d, ssm_w_glu, w_router, b_router, w_gu, b_gu, w_down, b_down):
    n_b, seg_len, d = x_prompt.shape
    n_bs, s_len, _ = x_sample.shape
    depth = w_ada.shape[0]
    assert s_len % seg_len == 0 and s_len // seg_len in (1, 2)
    per_sample = s_len // seg_len
    n_seg = n_b + n_bs * per_sample
    t = n_seg * seg_len
    tm = math.gcd(seg_len, TOKEN_TILE)

    seg_seq = list(range(n_b)) + [n_b + j for j in range(n_bs) for _ in range(per_sample)]
    seg_pos = [0] * n_b + [r for _ in range(n_bs) for r in range(per_sample)]
    seg_a = list(range(n_b)) + [n_b + j * per_sample for j in range(n_bs) for _ in range(per_sample)]
    seg_b = list(range(n_b)) + [n_b + j * per_sample + per_sample - 1 for j in range(n_bs) for _ in range(per_sample)]
    two = [0] * n_b + [int(per_sample == 2)] * (n_bs * per_sample)
    seq_first = [1] * n_b + [int(r == 0) for _ in range(n_bs) for r in range(per_sample)]
    seq_last = [1] * n_b + [int(r == per_sample - 1) for _ in range(n_bs) for r in range(per_sample)]
    per_seg = seg_len // tm
    pos_blk = jnp.asarray([seg_pos[s] * per_seg + r for s in range(n_seg) for r in range(per_seg)], jnp.int32)

    x = jnp.concatenate([x_prompt.reshape(n_b * seg_len, d), x_sample.reshape(n_bs * s_len, d)], axis=0)

    n_c = n_b + n_bs
    c_rows = -(-n_c // SUBLANES) * SUBLANES
    c_all = jnp.concatenate([c_prompt, c_sample, jnp.zeros((c_rows - n_c, d), F32)], axis=0)
    mod = _ada(c_all, w_ada, b_ada)
    mod = mod[:, jnp.asarray(seg_seq)].reshape(depth, n_seg, 6, d)
    mod = jnp.concatenate([mod, jnp.zeros((depth, n_seg, SUBLANES - 6, d), F32)], axis=2)

    inv_freq = ROPE_THETA ** (-jnp.arange(0, HEAD_DIM, 2, dtype=F32) / HEAD_DIM)
    ang = jnp.arange(s_len, dtype=F32)[:, None] * inv_freq[None, :]
    cos_t = jnp.tile(jnp.cos(ang), (1, 2 * LANES // HEAD_DIM))
    sin_t = jnp.tile(jnp.concatenate([-jnp.sin(ang), jnp.sin(ang)], axis=1), (1, LANES // HEAD_DIM))

    out_p = out_s = None
    for i in range(depth):
        modl = mod[i]
        g1 = norm1_g[i].reshape(1, d)
        g2 = norm2_g[i].reshape(1, d)
        j = i // 2
        if i % 2 == 0:
            lambda_init = 0.8 - 0.6 * math.exp(-0.3 * i)
            q, k, v = _qkv(x, modl, g1, w_qkv[j].astype(BF16), cos_t, sin_t, pos_blk, seg_len, tm)
            lam_pack = jnp.zeros((SUBLANES, 2 * HEAD_DIM), F32).at[0:4, 0:HEAD_DIM].set(
                jnp.stack([lam_q1[j], lam_k1[j], lam_q2[j], lam_k2[j]]).astype(F32))
            o = _flash(q, k, v, lam_pack, subln_g[j].reshape(1, 2 * HEAD_DIM).astype(F32),
                       jnp.asarray(seg_a, jnp.int32), jnp.asarray(seg_b, jnp.int32),
                       jnp.asarray(two, jnp.int32), seg_len, lambda_init)
            x = _proj_res(o, w_o[j].astype(BF16), x, modl, seg_len, tm, gate_row=2)
        else:
            params = (ssm_a_re[j], ssm_a_im[j], ssm_log_dt[j], ssm_b_re[j], ssm_b_im[j], ssm_c_re[j],
                      ssm_c_im[j], ssm_d[j], ssm_w_glu[j])
            x = _s5_layer(x, modl, g1, params, seg_len, seq_first, seq_last, tm)
        yc, gates = _moe_dispatch_and_experts(x, modl, g2, w_router[i], b_router[i], w_gu[i], b_gu[i],
                                              w_down[i], b_down[i], seg_len, tm)
        fg = final_g.reshape(1, d)
        if i == depth - 1:
            out_p = _combine(x, yc, gates, modl, fg, seg_len, tm, True, 0, n_b * seg_len)
            out_s = _combine(x, yc, gates, modl, fg, seg_len, tm, True, n_b * seg_len, n_bs * s_len)
        else:
            x = _combine(x, yc, gates, modl, fg, seg_len, tm, False)
    return out_p.reshape(n_b, seg_len, d), out_s.reshape(n_bs, s_len, d)
```

```python
import functools
import math

import jax
import jax.numpy as jnp
from jax import lax
from jax.experimental import pallas as pl
from jax.experimental.pallas import tpu as pltpu

F32 = jnp.float32
BF16 = jnp.bfloat16
HIGHEST = lax.Precision.HIGHEST

N_DIFF_HEADS = 8
HEAD_DIM = 64
ROPE_THETA = 10000.0
SUBLN_EPS = 1e-5
NORM_EPS = 1e-6
GROUP_CH = 16
STATE_DIM = 64
N_EXPERTS = 32
TOP_K = 4
SWIGLU_ALPHA = 1.702
SWIGLU_LIMIT = 7.0

LANES = 128
SUBLANES = 8
VMEM_LIMIT_BYTES = 56 * 1024 * 1024

TOKEN_TILE = 512
ATT_Q_TILE = 256
ATT_KV_TILE = 512
SSM_CHUNK = 16
SSM_GROUP_TILE = 8
SSM_ROW_TILE = 512
SSM_SCAN_TILE = 64
EXPERT_ROW_TILE = 512
FF_CHUNK = 512
COPY_ROWS = 2048


def _cparams(*sem):
    return pltpu.CompilerParams(dimension_semantics=sem, vmem_limit_bytes=VMEM_LIMIT_BYTES)


def _rms_mod(x, g, scale, shift):
    ms = jnp.mean(x * x, axis=-1, keepdims=True)
    return x * lax.rsqrt(ms + NORM_EPS) * g * (1.0 + scale) + shift


def _ada_kernel(c_ref, w_ref, b_ref, o_ref):
    c = c_ref[...]
    cond = c * jax.nn.sigmoid(c)
    o_ref[0] = jnp.dot(cond, w_ref[0], precision=HIGHEST, preferred_element_type=F32) + b_ref[0]


def _ada(c_pad, w_ada, b_ada):
    depth, d, n6 = w_ada.shape
    rows = c_pad.shape[0]
    tn = 1536 if n6 % 1536 == 0 else n6
    return pl.pallas_call(
        _ada_kernel,
        grid=(depth, n6 // tn),
        in_specs=[pl.BlockSpec((rows, d), lambda i, j: (0, 0)),
                  pl.BlockSpec((1, d, tn), lambda i, j: (i, 0, j)),
                  pl.BlockSpec((1, 1, tn), lambda i, j: (i, 0, j))],
        out_specs=pl.BlockSpec((1, rows, tn), lambda i, j: (i, 0, j)),
        out_shape=jax.ShapeDtypeStruct((depth, rows, n6), F32),
        compiler_params=_cparams("arbitrary", "arbitrary"),
        name="ada_mod",
    )(c_pad, w_ada, b_ada.reshape(depth, 1, n6))


def _qkv_kernel(pos_ref, x_ref, mod_ref, g_ref, w_ref, cos_ref, sin_ref, q_ref, k_ref, v_ref):
    del pos_ref
    d = x_ref.shape[1]
    h = _rms_mod(x_ref[...], g_ref[...], mod_ref[0, 1:2, :], mod_ref[0, 0:1, :])
    qkv = jnp.dot(h.astype(BF16), w_ref[...], preferred_element_type=F32)
    cos = cos_ref[...]
    sin = sin_ref[...]
    lane = lax.broadcasted_iota(jnp.int32, cos.shape, 1)
    first_half = (lane % HEAD_DIM) < (HEAD_DIM // 2)

    def rotary(t):
        partner = jnp.where(first_half,
                            pltpu.roll(t, LANES - HEAD_DIM // 2, 1),
                            pltpu.roll(t, HEAD_DIM // 2, 1))
        return t * cos + partner * sin

    for j in range(d // LANES):
        sl = slice(j * LANES, (j + 1) * LANES)
        q_ref[:, sl] = (rotary(qkv[:, sl]) * (HEAD_DIM ** -0.5)).astype(BF16)
        k_ref[:, sl] = rotary(qkv[:, d + j * LANES:d + (j + 1) * LANES]).astype(BF16)
    v_ref[...] = qkv[:, 2 * d:].astype(BF16)


def _qkv(x, modl, g, w_bf, cos_t, sin_t, pos_blk, seg_len, tm):
    t, d = x.shape
    per_seg = seg_len // tm
    tok = lambda i, p: (i, 0)
    out = jax.ShapeDtypeStruct((t, d), BF16)
    return pl.pallas_call(
        _qkv_kernel,
        grid_spec=pltpu.PrefetchScalarGridSpec(
            num_scalar_prefetch=1, grid=(t // tm,),
            in_specs=[pl.BlockSpec((tm, d), tok),
                      pl.BlockSpec((1, SUBLANES, d), lambda i, p: (i // per_seg, 0, 0)),
                      pl.BlockSpec((1, d), lambda i, p: (0, 0)),
                      pl.BlockSpec((d, 3 * d), lambda i, p: (0, 0)),
                      pl.BlockSpec((tm, LANES), lambda i, p: (p[i], 0)),
                      pl.BlockSpec((tm, LANES), lambda i, p: (p[i], 0))],
            out_specs=[pl.BlockSpec((tm, d), tok)] * 3),
        out_shape=[out, out, out],
        compiler_params=_cparams("arbitrary"),
        name="qkv_rotary",
    )(pos_blk, x, modl, g, w_bf, cos_t, sin_t)


def _flash_kernel(sa_ref, sb_ref, two_ref, q_ref, ka_ref, va_ref, kb_ref, vb_ref, lam_ref, g_ref,
                  o_ref, qs_ref, m_ref, l_ref, acc_ref, *, tq, tk, lambda_init):
    del sa_ref, sb_ref
    seg = pl.program_id(0)
    q = q_ref[...]
    lane = lax.broadcasted_iota(jnp.int32, q.shape, 1)
    zero = jnp.zeros_like(q)
    qs_ref[0:tq, :] = jnp.where(lane < HEAD_DIM, q, zero)
    qs_ref[tq:2 * tq, :] = jnp.where(lane >= HEAD_DIM, q, zero)
    m_ref[...] = jnp.full(m_ref.shape, -jnp.inf, F32)
    l_ref[...] = jnp.zeros(l_ref.shape, F32)
    acc_ref[...] = jnp.zeros(acc_ref.shape, F32)

    def sweep(k_ref, v_ref):
        def body(j, carry):
            off = pl.multiple_of(j * tk, tk)
            kb = k_ref[pl.ds(off, tk), :]
            vb = v_ref[pl.ds(off, tk), :]
            s = lax.dot_general(qs_ref[...], kb, (((1,), (1,)), ((), ())),
                                preferred_element_type=F32)
            m_prev = m_ref[...]
            m_new = jnp.maximum(m_prev, jnp.max(s, axis=-1, keepdims=True))
            p = jnp.exp(s - m_new)
            alpha = jnp.exp(m_prev - m_new)
            l_ref[...] = alpha * l_ref[...] + jnp.sum(p, axis=-1, keepdims=True)
            acc_ref[...] = alpha * acc_ref[...] + jnp.dot(p.astype(BF16), vb,
                                                          preferred_element_type=F32)
            m_ref[...] = m_new
            return carry
        lax.fori_loop(0, k_ref.shape[0] // tk, body, 0)

    sweep(ka_ref, va_ref)

    @pl.when(two_ref[seg] == 1)
    def _():
        sweep(kb_ref, vb_ref)

    o1 = acc_ref[0:tq, :] / l_ref[0:tq, :]
    o2 = acc_ref[tq:2 * tq, :] / l_ref[tq:2 * tq, :]
    lam = (jnp.exp(jnp.sum(lam_ref[0:1, :] * lam_ref[1:2, :], axis=-1, keepdims=True))
           - jnp.exp(jnp.sum(lam_ref[2:3, :] * lam_ref[3:4, :], axis=-1, keepdims=True))
           + lambda_init)
    o = o1 - lam * o2
    ms = jnp.mean(o * o, axis=-1, keepdims=True)
    o = o * lax.rsqrt(ms + SUBLN_EPS) * g_ref[...] * (1.0 - lambda_init)
    o_ref[...] = o.astype(BF16)


def _flash(q, k, v, lam_pack, subln_g, seg_a, seg_b, two, seg_len, lambda_init):
    t, d = q.shape
    nseg = t // seg_len
    tq = min(ATT_Q_TILE, seg_len)
    tk = min(ATT_KV_TILE, seg_len)
    nq = seg_len // tq
    width = 2 * HEAD_DIM
    kern = functools.partial(_flash_kernel, tq=tq, tk=tk, lambda_init=lambda_init)
    kv_a = pl.BlockSpec((seg_len, width), lambda s, h, i, sa, sb, tw: (sa[s], h))
    kv_b = pl.BlockSpec((seg_len, width), lambda s, h, i, sa, sb, tw: (sb[s], h))
    qo = pl.BlockSpec((tq, width), lambda s, h, i, sa, sb, tw: (s * nq + i, h))
    return pl.pallas_call(
        kern,
        grid_spec=pltpu.PrefetchScalarGridSpec(
            num_scalar_prefetch=3, grid=(nseg, N_DIFF_HEADS, nq),
            in_specs=[qo, kv_a, kv_a, kv_b, kv_b,
                      pl.BlockSpec((SUBLANES, width), lambda s, h, i, sa, sb, tw: (0, 0)),
                      pl.BlockSpec((1, width), lambda s, h, i, sa, sb, tw: (0, 0))],
            out_specs=qo,
            scratch_shapes=[pltpu.VMEM((2 * tq, width), BF16),
                            pltpu.VMEM((2 * tq, 1), F32),
                            pltpu.VMEM((2 * tq, 1), F32),
                            pltpu.VMEM((2 * tq, width), F32)]),
        out_shape=jax.ShapeDtypeStruct((t, d), BF16),
        compiler_params=_cparams("arbitrary", "arbitrary", "arbitrary"),
        name="diff_flash_attention",
    )(seg_a, seg_b, two, q, k, v, k, v, lam_pack, subln_g)


def _proj_res_kernel(a_ref, w_ref, x_ref, mod_ref, o_ref, *, gate_row):
    y = jnp.dot(a_ref[...], w_ref[...], preferred_element_type=F32)
    o_ref[...] = x_ref[...] + mod_ref[0, gate_row:gate_row + 1, :] * y


def _proj_res(a, w_bf, x, modl, seg_len, tm, gate_row):
    t, d = x.shape
    per_seg = seg_len // tm
    return pl.pallas_call(
        functools.partial(_proj_res_kernel, gate_row=gate_row),
        grid=(t // tm,),
        in_specs=[pl.BlockSpec((tm, a.shape[1]), lambda i: (i, 0)),
                  pl.BlockSpec(w_bf.shape, lambda i: (0, 0)),
                  pl.BlockSpec((tm, d), lambda i: (i, 0)),
                  pl.BlockSpec((1, SUBLANES, d), lambda i: (i // per_seg, 0, 0))],
        out_specs=pl.BlockSpec((tm, d), lambda i: (i, 0)),
        out_shape=jax.ShapeDtypeStruct((t, d), F32),
        compiler_params=_cparams("arbitrary"),
        name="proj_residual",
    )(a, w_bf, x, modl)


def _norm_bf16_kernel(x_ref, mod_ref, g_ref, o_ref):
    o_ref[...] = _rms_mod(x_ref[...], g_ref[...], mod_ref[0, 1:2, :], mod_ref[0, 0:1, :]).astype(BF16)


def _norm_bf16(x, modl, g, seg_len, tm):
    t, d = x.shape
    per_seg = seg_len // tm
    return pl.pallas_call(
        _norm_bf16_kernel,
        grid=(t // tm,),
        in_specs=[pl.BlockSpec((tm, d), lambda i: (i, 0)),
                  pl.BlockSpec((1, SUBLANES, d), lambda i: (i // per_seg, 0, 0)),
                  pl.BlockSpec((1, d), lambda i: (0, 0))],
        out_specs=pl.BlockSpec((tm, d), lambda i: (i, 0)),
        out_shape=jax.ShapeDtypeStruct((t, d), BF16),
        compiler_params=_cparams("arbitrary"),
        name="s5_prenorm",
    )(x, modl, g)


def _s5_tables(a_re, a_im, log_dt, b_re, b_im, c_re, c_im):
    f32 = F32
    L = SSM_CHUNK
    ar, ai = a_re.astype(f32), a_im.astype(f32)
    n_g, n_p = ar.shape[1], ar.shape[2]
    n_h = b_re.shape[-1]
    dt = jnp.exp(log_dt.astype(f32))[..., None]
    mag = jnp.exp(ar * dt)
    lr, li = mag * jnp.cos(ai * dt), mag * jnp.sin(ai * dt)
    den = ar * ar + ai * ai
    fr = ((lr - 1.0) * ar + li * ai) / den
    fi = (li * ar - (lr - 1.0) * ai) / den
    br, bi = b_re.astype(f32), b_im.astype(f32)
    bbr = fr[..., None] * br - fi[..., None] * bi
    bbi = fr[..., None] * bi + fi[..., None] * br
    cr, ci = c_re.astype(f32), c_im.astype(f32)
    kk = jnp.arange(L + 1, dtype=f32)
    pmag = jnp.exp(ar[..., None] * dt[..., None] * kk)
    pang = ai[..., None] * dt[..., None] * kk
    pr, pi = pmag * jnp.cos(pang), pmag * jnp.sin(pang)
    er = cr[..., None] * pr[:, :, None] - ci[..., None] * pi[:, :, None]
    ei = cr[..., None] * pi[:, :, None] + ci[..., None] * pr[:, :, None]
    klag = (jnp.einsum('dgopk,dgpi->dgkoi', er[..., :L], bbr, precision=HIGHEST)
            - jnp.einsum('dgopk,dgpi->dgkoi', ei[..., :L], bbi, precision=HIGHEST))
    kf, kb = klag[0], klag[1]
    kfull = jnp.concatenate([kb[:, 1:][:, ::-1], (kf[:, 0] + kb[:, 0])[:, None], kf[:, 1:]], axis=1)
    s_idx = jnp.arange(L)[:, None]
    t_idx = jnp.arange(L)[None, :]
    w = kfull[:, t_idx - s_idx + L - 1]
    w = jnp.transpose(w, (0, 1, 4, 2, 3)).reshape(n_g, L * n_h, L * n_h)
    def in_to_state(d, pw_idx):
        pwr = jnp.take(pr[d], pw_idx, axis=-1)
        pwi = jnp.take(pi[d], pw_idx, axis=-1)
        re = pwr[..., None] * bbr[d][:, :, None, :] - pwi[..., None] * bbi[d][:, :, None, :]
        im = pwr[..., None] * bbi[d][:, :, None, :] + pwi[..., None] * bbr[d][:, :, None, :]
        m = jnp.concatenate([re, im], axis=1)
        return jnp.transpose(m, (0, 2, 3, 1)).reshape(n_g, L * n_h, 2 * n_p)
    pcat = jnp.concatenate([in_to_state(0, L - 1 - jnp.arange(L)), in_to_state(1, jnp.arange(L))], axis=-1)
    def state_to_out(d, pw_idx):
        e_r = jnp.take(er[d], pw_idx, axis=-1)
        e_i = jnp.take(ei[d], pw_idx, axis=-1)
        m = jnp.concatenate([e_r, -e_i], axis=2)
        return jnp.transpose(m, (0, 2, 3, 1)).reshape(n_g, 2 * n_p, L * n_h)
    qcat = jnp.concatenate([state_to_out(0, jnp.arange(L) + 1), state_to_out(1, L - jnp.arange(L))], axis=1)

    def scan_mult(d):
        a_r, a_i = pr[d][..., L], pi[d][..., L]
        m1 = jnp.concatenate([a_r, a_r], axis=-1)
        m2 = jnp.concatenate([-a_i, a_i], axis=-1)
        lay = lambda m: jnp.transpose(m.reshape(n_g // SSM_GROUP_TILE, SSM_GROUP_TILE, 2 * n_p),
                                      (1, 0, 2)).reshape(SSM_GROUP_TILE, -1)
        return lay(m1), lay(m2)
    af1, af2 = scan_mult(0)
    ab1, ab2 = scan_mult(1)
    return w.astype(BF16), pcat.astype(BF16), qcat.astype(BF16), (af1, af2, ab1, ab2)


def _s5_states_kernel(u_ref, p_ref, sf_ref, sb_ref):
    half = sf_ref.shape[-1]
    for r in range(u_ref.shape[0]):
        s = jnp.dot(u_ref[r], p_ref[r], preferred_element_type=F32)
        sf_ref[:, r, :] = s[:, :half]
        sb_ref[:, r, :] = s[:, half:]


def _s5_states(ug, pcat, tc):
    n_g, nc, kdim = ug.shape
    gt = SSM_GROUP_TILE
    half = pcat.shape[-1] // 2
    out = jax.ShapeDtypeStruct((nc, gt, (n_g // gt) * half), F32)
    return pl.pallas_call(
        _s5_states_kernel,
        grid=(n_g // gt, nc // tc),
        in_specs=[pl.BlockSpec((gt, tc, kdim), lambda j, c: (j, c, 0)),
                  pl.BlockSpec((gt, kdim, 2 * half), lambda j, c: (j, 0, 0))],
        out_specs=[pl.BlockSpec((tc, gt, half), lambda j, c: (c, 0, j))] * 2,
        out_shape=[out, out],
        compiler_params=_cparams("arbitrary", "arbitrary"),
        name="s5_chunk_states",
    )(ug, pcat)


def _s5_scan_kernel(rf_ref, rb_ref, sf_ref, sb_ref, af1_ref, af2_ref, ab1_ref, ab2_ref,
                    xf_ref, xb_ref, cf_ref, cb_ref, *, tcs, nblk):
    i = pl.program_id(0)

    @pl.when(i == 0)
    def _():
        cf_ref[...] = jnp.zeros(cf_ref.shape, F32)
        cb_ref[...] = jnp.zeros(cb_ref.shape, F32)

    width = cf_ref.shape[-1]

    def swap(x):
        return jnp.concatenate(
            [pltpu.roll(x[:, j * LANES:(j + 1) * LANES], LANES // 2, 1) for j in range(width // LANES)],
            axis=1)

    af1, af2 = af1_ref[...], af2_ref[...]
    ab1, ab2 = ab1_ref[...], ab2_ref[...]

    def fwd(r, carry):
        carry = carry * (1 - rf_ref[i * tcs + r]).astype(F32)
        xf_ref[r] = carry
        return af1 * carry + af2 * swap(carry) + sf_ref[r]

    cf_ref[...] = lax.fori_loop(0, tcs, fwd, cf_ref[...])

    def bwd(rr, carry):
        r = tcs - 1 - rr
        carry = carry * (1 - rb_ref[(nblk - 1 - i) * tcs + r]).astype(F32)
        xb_ref[r] = carry
        return ab1 * carry + ab2 * swap(carry) + sb_ref[r]

    cb_ref[...] = lax.fori_loop(0, tcs, bwd, cb_ref[...])


def _s5_scan(sf, sb, mults, reset_f, reset_b, tcs):
    nc, gt, width = sf.shape
    nblk = nc // tcs
    fblk = pl.BlockSpec((tcs, gt, width), lambda i, a, b: (i, 0, 0))
    bblk = pl.BlockSpec((tcs, gt, width), lambda i, a, b: (nblk - 1 - i, 0, 0))
    tab = pl.BlockSpec((gt, width), lambda i, a, b: (0, 0))
    out = jax.ShapeDtypeStruct((nc, gt, width), F32)
    return pl.pallas_call(
        functools.partial(_s5_scan_kernel, tcs=tcs, nblk=nblk),
        grid_spec=pltpu.PrefetchScalarGridSpec(
            num_scalar_prefetch=2, grid=(nblk,),
            in_specs=[fblk, bblk, tab, tab, tab, tab],
            out_specs=[fblk, bblk],
            scratch_shapes=[pltpu.VMEM((gt, width), F32), pltpu.VMEM((gt, width), F32)]),
        out_shape=[out, out],
        compiler_params=_cparams("arbitrary"),
        name="s5_chunk_scan",
    )(reset_f, reset_b, sf, sb, *mults)


def _s5_out_kernel(u_ref, xf_ref, xb_ref, w_ref, q_ref, y_ref):
    for r in range(u_ref.shape[0]):
        xcat = jnp.concatenate([xf_ref[:, r, :], xb_ref[:, r, :]], axis=-1).astype(BF16)
        y_ref[r] = (jnp.dot(u_ref[r], w_ref[r], preferred_element_type=F32)
                    + jnp.dot(xcat, q_ref[r], preferred_element_type=F32))


def _s5_out(ug, xf, xb, w, qcat, tc):
    n_g, nc, kdim = ug.shape
    gt = SSM_GROUP_TILE
    half = xf.shape[-1] // (n_g // gt)
    xblk = pl.BlockSpec((tc, gt, half), lambda j, c: (c, 0, j))
    return pl.pallas_call(
        _s5_out_kernel,
        grid=(n_g // gt, nc // tc),
        in_specs=[pl.BlockSpec((gt, tc, kdim), lambda j, c: (j, c, 0)), xblk, xblk,
                  pl.BlockSpec((gt, kdim, kdim), lambda j, c: (j, 0, 0)),
                  pl.BlockSpec((gt, 2 * half, kdim), lambda j, c: (j, 0, 0))],
        out_specs=pl.BlockSpec((gt, tc, kdim), lambda j, c: (j, c, 0)),
        out_shape=jax.ShapeDtypeStruct((n_g, nc, kdim), F32),
        compiler_params=_cparams("arbitrary", "arbitrary"),
        name="s5_chunk_outputs",
    )(ug, xf, xb, w, qcat)


def _s5_glu_kernel(x_ref, y_ref, mod_ref, g_ref, dskip_ref, w_ref, o_ref):
    d = x_ref.shape[1]
    x = x_ref[...]
    h = _rms_mod(x, g_ref[...], mod_ref[0, 1:2, :], mod_ref[0, 0:1, :])
    z = jax.nn.gelu(y_ref[...] + dskip_ref[...] * h)
    vg = jnp.dot(z.astype(BF16), w_ref[...], preferred_element_type=F32)
    m = vg[:, :d] * jax.nn.sigmoid(vg[:, d:])
    o_ref[...] = x + mod_ref[0, 2:3, :] * m


def _s5_glu(x, y, modl, g, d_skip, w_bf, seg_len, tm):
    t, d = x.shape
    per_seg = seg_len // tm
    tok = pl.BlockSpec((tm, d), lambda i: (i, 0))
    vec = pl.BlockSpec((1, d), lambda i: (0, 0))
    return pl.pallas_call(
        _s5_glu_kernel,
        grid=(t // tm,),
        in_specs=[tok, tok, pl.BlockSpec((1, SUBLANES, d), lambda i: (i // per_seg, 0, 0)), vec, vec,
                  pl.BlockSpec((d, 2 * d), lambda i: (0, 0))],
        out_specs=tok,
        out_shape=jax.ShapeDtypeStruct((t, d), F32),
        compiler_params=_cparams("arbitrary"),
        name="s5_glu_residual",
    )(x, y, modl, g, d_skip, w_bf)


def _s5_layer(x, modl, g, params, seg_len, seq_first, seq_last, tm):
    a_re, a_im, log_dt, b_re, b_im, c_re, c_im, d_skip, w_glu = params
    t, d = x.shape
    L = SSM_CHUNK
    n_g = d // GROUP_CH
    nc = t // L
    w, pcat, qcat, mults = _s5_tables(a_re, a_im, log_dt, b_re, b_im, c_re, c_im)
    hb = _norm_bf16(x, modl, g, seg_len, tm)
    ug = hb.reshape(nc, L, n_g, GROUP_CH).transpose(2, 0, 1, 3).reshape(n_g, nc, L * GROUP_CH)
    tc = math.gcd(nc, SSM_ROW_TILE)
    sf, sb = _s5_states(ug, pcat, tc)
    chunks_per_seg = seg_len // L
    reset_f = jnp.repeat(jnp.asarray(seq_first, jnp.int32), chunks_per_seg) * (
        jnp.tile(jnp.arange(chunks_per_seg) == 0, len(seq_first))).astype(jnp.int32)
    reset_b = jnp.repeat(jnp.asarray(seq_last, jnp.int32), chunks_per_seg) * (
        jnp.tile(jnp.arange(chunks_per_seg) == chunks_per_seg - 1, len(seq_last))).astype(jnp.int32)
    xf, xb = _s5_scan(sf, sb, mults, reset_f, reset_b, math.gcd(nc, SSM_SCAN_TILE))
    yg = _s5_out(ug, xf, xb, w, qcat, tc)
    y = yg.reshape(n_g, nc, L, GROUP_CH).transpose(1, 2, 0, 3).reshape(t, d)
    return _s5_glu(x, y, modl, g, d_skip.reshape(1, d), w_glu.astype(BF16), seg_len, tm)


def _load_rows(ref, *lead):
    return jnp.concatenate([ref[(*lead, slice(None), s, slice(None))] for s in range(ref.shape[-2])], axis=-1)


def _store_rows(ref, val):
    for s in range(ref.shape[-2]):
        ref[:, s, :] = val[:, s * LANES:(s + 1) * LANES]


def _router_kernel(x_ref, mod_ref, g_ref, w_ref, b_ref, h_ref, idx_ref, gate_ref, rank_ref, cnt_ref,
                   carry_ref):
    tm = x_ref.shape[0]
    n_e = w_ref.shape[0]

    @pl.when(pl.program_id(0) == 0)
    def _():
        carry_ref[...] = jnp.zeros(carry_ref.shape, F32)

    h = _rms_mod(x_ref[...], g_ref[...], mod_ref[0, 4:5, :], mod_ref[0, 3:4, :])
    _store_rows(h_ref, h)
    logits = lax.dot_general(w_ref[...], h, (((1,), (1,)), ((), ())), precision=HIGHEST,
                             preferred_element_type=F32) + b_ref[...]
    e_iota = lax.broadcasted_iota(jnp.int32, (n_e, tm), 0)
    work = logits
    chosen = jnp.zeros((n_e, tm), F32)
    vals, idxs = [], []
    for _ in range(TOP_K):
        m = jnp.max(work, axis=0, keepdims=True)
        ix = jnp.min(jnp.where(work == m, e_iota, n_e), axis=0, keepdims=True)
        hit = e_iota == ix
        work = jnp.where(hit, -jnp.inf, work)
        chosen = jnp.where(hit, 1.0, chosen)
        vals.append(m)
        idxs.append(ix)
    v = jnp.concatenate(vals, axis=0)
    ex = jnp.exp(v - v[0:1])
    gate_ref[...] = ex / jnp.sum(ex, axis=0, keepdims=True)
    idx_ref[...] = jnp.concatenate(idxs, axis=0)
    row = lax.broadcasted_iota(jnp.int32, (tm, tm), 0)
    col = lax.broadcasted_iota(jnp.int32, (tm, tm), 1)
    before = (row < col).astype(BF16)
    cum = jnp.dot(chosen.astype(BF16), before, preferred_element_type=F32) + carry_ref[...]
    ranks = [jnp.sum(jnp.where(e_iota == ix, cum, 0.0), axis=0, keepdims=True) for ix in idxs]
    rank_ref[...] = jnp.concatenate(ranks, axis=0).astype(jnp.int32)
    carry_ref[...] = carry_ref[...] + jnp.sum(chosen, axis=1, keepdims=True)
    cnt_ref[...] = jnp.broadcast_to(carry_ref[...], cnt_ref.shape).astype(jnp.int32)


def _router(x, modl, g, w_router_t, b_router, seg_len, tm):
    t, d = x.shape
    n_e = w_router_t.shape[0]
    per_seg = seg_len // tm
    sel = pl.BlockSpec((TOP_K, tm), lambda i: (0, i))
    return pl.pallas_call(
        _router_kernel,
        grid=(t // tm,),
        in_specs=[pl.BlockSpec((tm, d), lambda i: (i, 0)),
                  pl.BlockSpec((1, SUBLANES, d), lambda i: (i // per_seg, 0, 0)),
                  pl.BlockSpec((1, d), lambda i: (0, 0)),
                  pl.BlockSpec((n_e, d), lambda i: (0, 0)),
                  pl.BlockSpec((n_e, 1), lambda i: (0, 0))],
        out_specs=[pl.BlockSpec((tm, d // LANES, LANES), lambda i: (i, 0, 0)), sel, sel, sel,
                   pl.BlockSpec((n_e, LANES), lambda i: (0, 0))],
        out_shape=[jax.ShapeDtypeStruct((t, d // LANES, LANES), F32),
                   jax.ShapeDtypeStruct((TOP_K, t), jnp.int32),
                   jax.ShapeDtypeStruct((TOP_K, t), F32),
                   jax.ShapeDtypeStruct((TOP_K, t), jnp.int32),
                   jax.ShapeDtypeStruct((n_e, LANES), jnp.int32)],
        scratch_shapes=[pltpu.VMEM((n_e, 1), F32)],
        compiler_params=_cparams("arbitrary"),
        name="moe_router",
    )(x, modl, g, w_router_t, b_router)


def _row_copy_kernel(src_idx_ref, dst_idx_ref, src_hbm, *rest, n):
    dst_hbm, sem = rest[-2], rest[-1]

    def row_copy(s, d):
        return pltpu.make_async_copy(src_hbm.at[s], dst_hbm.at[d], sem)

    def issue(j, carry):
        row_copy(src_idx_ref[0, 0, j], dst_idx_ref[0, 0, j]).start()
        return carry

    lax.fori_loop(0, n, issue, 0)

    def drain(j, carry):
        row_copy(0, 0).wait()
        return carry

    lax.fori_loop(0, n, drain, 0)


def _row_copy(src, src_idx, dst_idx, dst_rows, dst_init=None):
    n_total = src_idx.shape[0]
    n = math.gcd(n_total, COPY_ROWS)
    steps = n_total // n
    idx_spec = pl.BlockSpec((1, 1, n), lambda i: (i, 0, 0), memory_space=pltpu.SMEM)
    any_spec = pl.BlockSpec(memory_space=pl.ANY)
    extra = () if dst_init is None else (dst_init,)
    return pl.pallas_call(
        functools.partial(_row_copy_kernel, n=n),
        grid=(steps,),
        in_specs=[idx_spec, idx_spec, any_spec] + [any_spec] * len(extra),
        out_specs=any_spec,
        out_shape=jax.ShapeDtypeStruct((dst_rows,) + src.shape[1:], src.dtype),
        scratch_shapes=[pltpu.SemaphoreType.DMA(())],
        input_output_aliases={3: 0} if extra else {},
        compiler_params=_cparams("arbitrary"),
        name="row_copy",
    )(src_idx.reshape(steps, 1, n), dst_idx.reshape(steps, 1, n), src, *extra)


def _expert_kernel(be_ref, bi_ref, nu_ref, x_ref, wgu_ref, bgu_ref, wd_ref, bd_ref, y_ref):
    del be_ref, bi_ref
    n_ff = wd_ref.shape[1]

    @pl.when(pl.program_id(0) < nu_ref[0])
    def _():
        x = _load_rows(x_ref).astype(BF16)
        acc = jnp.zeros((x.shape[0], wd_ref.shape[2]), F32) + bd_ref[0]
        for c in range(n_ff // FF_CHUNK):
            lo = c * FF_CHUNK
            g_part = (jnp.dot(x, wgu_ref[0, :, lo:lo + FF_CHUNK], preferred_element_type=F32)
                      + bgu_ref[0, :, lo:lo + FF_CHUNK])
            u_part = (jnp.dot(x, wgu_ref[0, :, n_ff + lo:n_ff + lo + FF_CHUNK], preferred_element_type=F32)
                      + bgu_ref[0, :, n_ff + lo:n_ff + lo + FF_CHUNK])
            g_part = jnp.minimum(g_part, SWIGLU_LIMIT)
            u_part = jnp.clip(u_part, -SWIGLU_LIMIT, SWIGLU_LIMIT)
            act = (u_part + 1.0) * g_part * jax.nn.sigmoid(SWIGLU_ALPHA * g_part)
            acc = acc + jnp.dot(act.astype(BF16), wd_ref[0, lo:lo + FF_CHUNK, :],
                                preferred_element_type=F32)
        _store_rows(y_ref, acc)


def _experts(xs, blk_expert, blk_idx, n_used, w_gu_bf, b_gu, w_down_bf, b_down, tme):
    nr = xs.shape[0]
    n_e, d, two_f = w_gu_bf.shape
    n_ff = two_f // 2
    nblk = nr // tme
    rows = pl.BlockSpec((tme,) + xs.shape[1:], lambda b, be, bi, nu: (bi[b], 0, 0))
    return pl.pallas_call(
        _expert_kernel,
        grid_spec=pltpu.PrefetchScalarGridSpec(
            num_scalar_prefetch=3, grid=(nblk,),
            in_specs=[rows,
                      pl.BlockSpec((1, d, two_f), lambda b, be, bi, nu: (be[b], 0, 0)),
                      pl.BlockSpec((1, 1, two_f), lambda b, be, bi, nu: (be[b], 0, 0)),
                      pl.BlockSpec((1, n_ff, d), lambda b, be, bi, nu: (be[b], 0, 0)),
                      pl.BlockSpec((1, 1, d), lambda b, be, bi, nu: (be[b], 0, 0))],
            out_specs=rows),
        out_shape=jax.ShapeDtypeStruct(xs.shape, F32),
        compiler_params=_cparams("arbitrary"),
        name="moe_experts",
    )(blk_expert, blk_idx, n_used, xs, w_gu_bf, b_gu.reshape(n_e, 1, two_f), w_down_bf,
      b_down.reshape(n_e, 1, d))


def _combine_kernel(x_ref, yc_ref, gate_ref, mod_ref, g_ref, o_ref, *, final_norm):
    gates = gate_ref[...]
    moe = gates[:, 0:1] * _load_rows(yc_ref, 0)
    for k in range(1, TOP_K):
        moe = moe + gates[:, k:k + 1] * _load_rows(yc_ref, k)
    x = x_ref[...] + mod_ref[0, 5:6, :] * moe
    if final_norm:
        ms = jnp.mean(x * x, axis=-1, keepdims=True)
        x = x * lax.rsqrt(ms + NORM_EPS) * g_ref[...]
    o_ref[...] = x


def _combine(x, yc, gates, modl, final_g, seg_len, tm, final_norm, row0=0, rows=None):
    t, d = x.shape
    rows = t if rows is None else rows
    per_seg = seg_len // tm
    b0 = row0 // tm
    return pl.pallas_call(
        functools.partial(_combine_kernel, final_norm=final_norm),
        grid=(rows // tm,),
        in_specs=[pl.BlockSpec((tm, d), lambda i: (i + b0, 0)),
                  pl.BlockSpec((TOP_K, tm) + yc.shape[2:], lambda i: (0, i + b0, 0, 0)),
                  pl.BlockSpec((tm, TOP_K), lambda i: (i + b0, 0)),
                  pl.BlockSpec((1, SUBLANES, d), lambda i: ((i + b0) // per_seg, 0, 0)),
                  pl.BlockSpec((1, d), lambda i: (0, 0))],
        out_specs=pl.BlockSpec((tm, d), lambda i: (i, 0)),
        out_shape=jax.ShapeDtypeStruct((rows, d), F32),
        compiler_params=_cparams("arbitrary"),
        name="moe_combine",
    )(x, yc, gates, modl, final_g)


def _moe_dispatch_and_experts(x, modl, g, w_router, b_router, w_gu, b_gu, w_down, b_down, seg_len, tm):
    t, d = x.shape
    n_e = w_router.shape[1]
    tme = math.gcd(t * TOP_K, EXPERT_ROW_TILE)
    h, idx_t, gate_t, rank_t, cnt = _router(x, modl, g, w_router.T, b_router.reshape(n_e, 1), seg_len, tm)
    counts = cnt[:, 0]
    padded = (counts + tme - 1) // tme * tme
    pad_end = jnp.cumsum(padded)
    pad_start = pad_end - padded
    nblk = (t * TOP_K) // tme + n_e
    n_used = (pad_end[-1] // tme).astype(jnp.int32).reshape(1)
    blk_idx = jnp.minimum(jnp.arange(nblk, dtype=jnp.int32), n_used[0] - 1)
    blk_expert = jnp.minimum(jnp.searchsorted(pad_end, blk_idx * tme, side='right'), n_e - 1).astype(jnp.int32)
    start_of = jnp.sum(jnp.where(idx_t[None] == jnp.arange(n_e, dtype=jnp.int32)[:, None, None],
                                 pad_start[:, None, None], 0), axis=0)
    dest = (start_of + rank_t).astype(jnp.int32)
    dest_tk = dest.T.reshape(-1)
    tok_tk = jnp.repeat(jnp.arange(t, dtype=jnp.int32), TOP_K)
    slot_tk = (jnp.tile(jnp.arange(TOP_K, dtype=jnp.int32) * t, t) + tok_tk)
    xs = _row_copy(h, tok_tk, dest_tk, nblk * tme, jnp.zeros((nblk * tme,) + h.shape[1:], F32))
    ys = _experts(xs, blk_expert, blk_idx, n_used, w_gu.astype(BF16), b_gu, w_down.astype(BF16), b_down, tme)
    yc = _row_copy(ys, dest_tk, slot_tk, TOP_K * t)
    return yc.reshape((TOP_K, t) + h.shape[1:]), gate_t.T


def kernel(x_prompt, x_sample, c_prompt, c_sample, norm1_g, norm2_g, final_g, w_ada, b_ada, w_qkv, w_o,
           lam_q1, lam_k1, lam_q2, lam_k2, subln_g, ssm_a_re, ssm_a_im, ssm_log_dt, ssm_b_re, ssm_b_im,
           ssm_c_re, ssm_c_im, ssm_d, ssm_w_glu, w_router, b_router, w_gu, b_gu, w_down, b_down):
    n_b, seg_len, d = x_prompt.shape
    n_bs, s_len, _ = x_sample.shape
    depth = w_ada.shape[0]
    assert s_len % seg_len == 0 and s_len // seg_len in (1, 2)
    per_sample = s_len // seg_len
    n_seg = n_b + n_bs * per_sample
    t = n_seg * seg_len
    tm = math.gcd(seg_len, TOKEN_TILE)

    seg_seq = list(range(n_b)) + [n_b + j for j in range(n_bs) for _ in range(per_sample)]
    seg_pos = [0] * n_b + [r for _ in range(n_bs) for r in range(per_sample)]
    seg_a = list(range(n_b)) + [n_b + j * per_sample for j in range(n_bs) for _ in range(per_sample)]
    seg_b = list(range(n_b)) + [n_b + j * per_sample + per_sample - 1 for j in range(n_bs) for _ in range(per_sample)]
    two = [0] * n_b + [int(per_sample == 2)] * (n_bs * per_sample)
    seq_first = [1] * n_b + [int(r == 0) for _ in range(n_bs) for r in range(per_sample)]
    seq_last = [1] * n_b + [int(r == per_sample - 1) for _ in range(n_bs) for r in range(per_sample)]
    per_seg = seg_len // tm
    pos_blk = jnp.asarray([seg_pos[s] * per_seg + r for s in range(n_seg) for r in range(per_seg)], jnp.int32)

    x = jnp.concatenate([x_prompt.reshape(n_b * seg_len, d), x_sample.reshape(n_bs * s_len, d)], axis=0)

    n_c = n_b + n_bs
    c_rows = -(-n_c // SUBLANES) * SUBLANES
    c_all = jnp.concatenate([c_prompt, c_sample, jnp.zeros((c_rows - n_c, d), F32)], axis=0)
    mod = _ada(c_all, w_ada, b_ada)
    mod = mod[:, jnp.asarray(seg_seq)].reshape(depth, n_seg, 6, d)
    mod = jnp.concatenate([mod, jnp.zeros((depth, n_seg, SUBLANES - 6, d), F32)], axis=2)

    inv_freq = ROPE_THETA ** (-jnp.arange(0, HEAD_DIM, 2, dtype=F32) / HEAD_DIM)
    ang = jnp.arange(s_len, dtype=F32)[:, None] * inv_freq[None, :]
    cos_t = jnp.tile(jnp.cos(ang), (1, 2 * LANES // HEAD_DIM))
    sin_t = jnp.tile(jnp.concatenate([-jnp.sin(ang), jnp.sin(ang)], axis=1), (1, LANES // HEAD_DIM))

    out_p = out_s = None
    for i in range(depth):
        modl = mod[i]
        g1 = norm1_g[i].reshape(1, d)
        g2 = norm2_g[i].reshape(1, d)
        j = i // 2
        if i % 2 == 0:
            lambda_init = 0.8 - 0.6 * math.exp(-0.3 * i)
            q, k, v = _qkv(x, modl, g1, w_qkv[j].astype(BF16), cos_t, sin_t, pos_blk, seg_len, tm)
            lam_pack = jnp.zeros((SUBLANES, 2 * HEAD_DIM), F32).at[0:4, 0:HEAD_DIM].set(
                jnp.stack([lam_q1[j], lam_k1[j], lam_q2[j], lam_k2[j]]).astype(F32))
            o = _flash(q, k, v, lam_pack, subln_g[j].reshape(1, 2 * HEAD_DIM).astype(F32),
                       jnp.asarray(seg_a, jnp.int32), jnp.asarray(seg_b, jnp.int32),
                       jnp.asarray(two, jnp.int32), seg_len, lambda_init)
            x = _proj_res(o, w_o[j].astype(BF16), x, modl, seg_len, tm, gate_row=2)
        else:
            params = (ssm_a_re[j], ssm_a_im[j], ssm_log_dt[j], ssm_b_re[j], ssm_b_im[j], ssm_c_re[j],
                      ssm_c_im[j], ssm_d[j], ssm_w_glu[j])
            x = _s5_layer(x, modl, g1, params, seg_len, seq_first, seq_last, tm)
        yc, gates = _moe_dispatch_and_experts(x, modl, g2, w_router[i], b_router[i], w_gu[i], b_gu[i],
                                              w_down[i], b_down[i], seg_len, tm)
        fg = final_g.reshape(1, d)
        if i == depth - 1:
            out_p = _combine(x, yc, gates, modl, fg, seg_len, tm, True, 0, n_b * seg_len)
            out_s = _combine(x, yc, gates, modl, fg, seg_len, tm, True, n_b * seg_len, n_bs * s_len)
        else:
            x = _combine(x, yc, gates, modl, fg, seg_len, tm, False)
    return out_p.reshape(n_b, seg_len, d), out_s.reshape(n_bs, s_len, d)
```

```python
import functools
import math

import jax
import jax.numpy as jnp
from jax import lax
from jax.experimental import pallas as pl
from jax.experimental.pallas import tpu as pltpu

F32 = jnp.float32
BF16 = jnp.bfloat16
HIGHEST = lax.Precision.HIGHEST

N_DIFF_HEADS = 8
HEAD_DIM = 64
ROPE_THETA = 10000.0
SUBLN_EPS = 1e-5
NORM_EPS = 1e-6
GROUP_CH = 16
STATE_DIM = 64
N_EXPERTS = 32
TOP_K = 4
SWIGLU_ALPHA = 1.702
SWIGLU_LIMIT = 7.0

LANES = 128
SUBLANES = 8
VMEM_LIMIT_BYTES = 56 * 1024 * 1024

TOKEN_TILE = 512
ATT_Q_TILE = 256
ATT_KV_TILE = 512
SSM_CHUNK = 16
SSM_GROUP_TILE = 8
SSM_ROW_TILE = 512
SSM_SCAN_TILE = 64
EXPERT_ROW_TILE = 512
FF_CHUNK = 512


def _cparams(*sem):
    return pltpu.CompilerParams(dimension_semantics=sem, vmem_limit_bytes=VMEM_LIMIT_BYTES)


def _rms_mod(x, g, scale, shift):
    ms = jnp.mean(x * x, axis=-1, keepdims=True)
    return x * lax.rsqrt(ms + NORM_EPS) * g * (1.0 + scale) + shift


def _ada_kernel(c_ref, w_ref, b_ref, o_ref):
    c = c_ref[...]
    cond = c * jax.nn.sigmoid(c)
    o_ref[0] = jnp.dot(cond, w_ref[0], precision=HIGHEST, preferred_element_type=F32) + b_ref[0]


def _ada(c_pad, w_ada, b_ada):
    depth, d, n6 = w_ada.shape
    rows = c_pad.shape[0]
    tn = 1536 if n6 % 1536 == 0 else n6
    return pl.pallas_call(
        _ada_kernel,
        grid=(depth, n6 // tn),
        in_specs=[pl.BlockSpec((rows, d), lambda i, j: (0, 0)),
                  pl.BlockSpec((1, d, tn), lambda i, j: (i, 0, j)),
                  pl.BlockSpec((1, 1, tn), lambda i, j: (i, 0, j))],
        out_specs=pl.BlockSpec((1, rows, tn), lambda i, j: (i, 0, j)),
        out_shape=jax.ShapeDtypeStruct((depth, rows, n6), F32),
        compiler_params=_cparams("arbitrary", "arbitrary"),
        name="ada_mod",
    )(c_pad, w_ada, b_ada.reshape(depth, 1, n6))


def _qkv_kernel(pos_ref, x_ref, mod_ref, g_ref, w_ref, cos_ref, sin_ref, q_ref, k_ref, v_ref):
    del pos_ref
    d = x_ref.shape[1]
    h = _rms_mod(x_ref[...], g_ref[...], mod_ref[0, 1:2, :], mod_ref[0, 0:1, :])
    qkv = jnp.dot(h.astype(BF16), w_ref[...], preferred_element_type=F32)
    cos = cos_ref[...]
    sin = sin_ref[...]
    lane = lax.broadcasted_iota(jnp.int32, cos.shape, 1)
    first_half = (lane % HEAD_DIM) < (HEAD_DIM // 2)

    def rotary(t):
        partner = jnp.where(first_half,
                            pltpu.roll(t, LANES - HEAD_DIM // 2, 1),
                            pltpu.roll(t, HEAD_DIM // 2, 1))
        return t * cos + partner * sin

    for j in range(d // LANES):
        sl = slice(j * LANES, (j + 1) * LANES)
        q_ref[:, sl] = (rotary(qkv[:, sl]) * (HEAD_DIM ** -0.5)).astype(BF16)
        k_ref[:, sl] = rotary(qkv[:, d + j * LANES:d + (j + 1) * LANES]).astype(BF16)
    v_ref[...] = qkv[:, 2 * d:].astype(BF16)


def _qkv(x, modl, g, w_bf, cos_t, sin_t, pos_blk, seg_len, tm):
    t, d = x.shape
    per_seg = seg_len // tm
    tok = lambda i, p: (i, 0)
    out = jax.ShapeDtypeStruct((t, d), BF16)
    return pl.pallas_call(
        _qkv_kernel,
        grid_spec=pltpu.PrefetchScalarGridSpec(
            num_scalar_prefetch=1, grid=(t // tm,),
            in_specs=[pl.BlockSpec((tm, d), tok),
                      pl.BlockSpec((1, SUBLANES, d), lambda i, p: (i // per_seg, 0, 0)),
                      pl.BlockSpec((1, d), lambda i, p: (0, 0)),
                      pl.BlockSpec((d, 3 * d), lambda i, p: (0, 0)),
                      pl.BlockSpec((tm, LANES), lambda i, p: (p[i], 0)),
                      pl.BlockSpec((tm, LANES), lambda i, p: (p[i], 0))],
            out_specs=[pl.BlockSpec((tm, d), tok)] * 3),
        out_shape=[out, out, out],
        compiler_params=_cparams("arbitrary"),
        name="qkv_rotary",
    )(pos_blk, x, modl, g, w_bf, cos_t, sin_t)


def _flash_kernel(sa_ref, sb_ref, two_ref, q_ref, ka_ref, va_ref, kb_ref, vb_ref, lam_ref, g_ref,
                  o_ref, qs_ref, m_ref, l_ref, acc_ref, *, tq, tk, lambda_init):
    del sa_ref, sb_ref
    seg = pl.program_id(0)
    q = q_ref[...]
    lane = lax.broadcasted_iota(jnp.int32, q.shape, 1)
    zero = jnp.zeros_like(q)
    qs_ref[0:tq, :] = jnp.where(lane < HEAD_DIM, q, zero)
    qs_ref[tq:2 * tq, :] = jnp.where(lane >= HEAD_DIM, q, zero)
    m_ref[...] = jnp.full(m_ref.shape, -jnp.inf, F32)
    l_ref[...] = jnp.zeros(l_ref.shape, F32)
    acc_ref[...] = jnp.zeros(acc_ref.shape, F32)

    def sweep(k_ref, v_ref):
        def body(j, carry):
            off = pl.multiple_of(j * tk, tk)
            kb = k_ref[pl.ds(off, tk), :]
            vb = v_ref[pl.ds(off, tk), :]
            s = lax.dot_general(qs_ref[...], kb, (((1,), (1,)), ((), ())),
                                preferred_element_type=F32)
            m_prev = m_ref[...]
            m_new = jnp.maximum(m_prev, jnp.max(s, axis=-1, keepdims=True))
            p = jnp.exp(s - m_new)
            alpha = jnp.exp(m_prev - m_new)
            l_ref[...] = alpha * l_ref[...] + jnp.sum(p, axis=-1, keepdims=True)
            acc_ref[...] = alpha * acc_ref[...] + jnp.dot(p.astype(BF16), vb,
                                                          preferred_element_type=F32)
            m_ref[...] = m_new
            return carry
        lax.fori_loop(0, k_ref.shape[0] // tk, body, 0)

    sweep(ka_ref, va_ref)

    @pl.when(two_ref[seg] == 1)
    def _():
        sweep(kb_ref, vb_ref)

    o1 = acc_ref[0:tq, :] / l_ref[0:tq, :]
    o2 = acc_ref[tq:2 * tq, :] / l_ref[tq:2 * tq, :]
    lam = (jnp.exp(jnp.sum(lam_ref[0:1, :] * lam_ref[1:2, :], axis=-1, keepdims=True))
           - jnp.exp(jnp.sum(lam_ref[2:3, :] * lam_ref[3:4, :], axis=-1, keepdims=True))
           + lambda_init)
    o = o1 - lam * o2
    ms = jnp.mean(o * o, axis=-1, keepdims=True)
    o = o * lax.rsqrt(ms + SUBLN_EPS) * g_ref[...] * (1.0 - lambda_init)
    o_ref[...] = o.astype(BF16)


def _flash(q, k, v, lam_pack, subln_g, seg_a, seg_b, two, seg_len, lambda_init):
    t, d = q.shape
    nseg = t // seg_len
    tq = min(ATT_Q_TILE, seg_len)
    tk = min(ATT_KV_TILE, seg_len)
    nq = seg_len // tq
    width = 2 * HEAD_DIM
    kern = functools.partial(_flash_kernel, tq=tq, tk=tk, lambda_init=lambda_init)
    kv_a = pl.BlockSpec((seg_len, width), lambda s, h, i, sa, sb, tw: (sa[s], h))
    kv_b = pl.BlockSpec((seg_len, width), lambda s, h, i, sa, sb, tw: (sb[s], h))
    qo = pl.BlockSpec((tq, width), lambda s, h, i, sa, sb, tw: (s * nq + i, h))
    return pl.pallas_call(
        kern,
        grid_spec=pltpu.PrefetchScalarGridSpec(
            num_scalar_prefetch=3, grid=(nseg, N_DIFF_HEADS, nq),
            in_specs=[qo, kv_a, kv_a, kv_b, kv_b,
                      pl.BlockSpec((SUBLANES, width), lambda s, h, i, sa, sb, tw: (0, 0)),
                      pl.BlockSpec((1, width), lambda s, h, i, sa, sb, tw: (0, 0))],
            out_specs=qo,
            scratch_shapes=[pltpu.VMEM((2 * tq, width), BF16),
                            pltpu.VMEM((2 * tq, 1), F32),
                            pltpu.VMEM((2 * tq, 1), F32),
                            pltpu.VMEM((2 * tq, width), F32)]),
        out_shape=jax.ShapeDtypeStruct((t, d), BF16),
        compiler_params=_cparams("arbitrary", "arbitrary", "arbitrary"),
        name="diff_flash_attention",
    )(seg_a, seg_b, two, q, k, v, k, v, lam_pack, subln_g)


def _proj_res_kernel(a_ref, w_ref, x_ref, mod_ref, o_ref, *, gate_row):
    y = jnp.dot(a_ref[...], w_ref[...], preferred_element_type=F32)
    o_ref[...] = x_ref[...] + mod_ref[0, gate_row:gate_row + 1, :] * y


def _proj_res(a, w_bf, x, modl, seg_len, tm, gate_row):
    t, d = x.shape
    per_seg = seg_len // tm
    return pl.pallas_call(
        functools.partial(_proj_res_kernel, gate_row=gate_row),
        grid=(t // tm,),
        in_specs=[pl.BlockSpec((tm, a.shape[1]), lambda i: (i, 0)),
                  pl.BlockSpec(w_bf.shape, lambda i: (0, 0)),
                  pl.BlockSpec((tm, d), lambda i: (i, 0)),
                  pl.BlockSpec((1, SUBLANES, d), lambda i: (i // per_seg, 0, 0))],
        out_specs=pl.BlockSpec((tm, d), lambda i: (i, 0)),
        out_shape=jax.ShapeDtypeStruct((t, d), F32),
        compiler_params=_cparams("arbitrary"),
        name="proj_residual",
    )(a, w_bf, x, modl)


def _norm_bf16_kernel(x_ref, mod_ref, g_ref, o_ref):
    o_ref[...] = _rms_mod(x_ref[...], g_ref[...], mod_ref[0, 1:2, :], mod_ref[0, 0:1, :]).astype(BF16)


def _norm_bf16(x, modl, g, seg_len, tm):
    t, d = x.shape
    per_seg = seg_len // tm
    return pl.pallas_call(
        _norm_bf16_kernel,
        grid=(t // tm,),
        in_specs=[pl.BlockSpec((tm, d), lambda i: (i, 0)),
                  pl.BlockSpec((1, SUBLANES, d), lambda i: (i // per_seg, 0, 0)),
                  pl.BlockSpec((1, d), lambda i: (0, 0))],
        out_specs=pl.BlockSpec((tm, d), lambda i: (i, 0)),
        out_shape=jax.ShapeDtypeStruct((t, d), BF16),
        compiler_params=_cparams("arbitrary"),
        name="s5_prenorm",
    )(x, modl, g)


def _s5_tables(a_re, a_im, log_dt, b_re, b_im, c_re, c_im):
    f32 = F32
    L = SSM_CHUNK
    ar, ai = a_re.astype(f32), a_im.astype(f32)
    n_g, n_p = ar.shape[1], ar.shape[2]
    n_h = b_re.shape[-1]
    dt = jnp.exp(log_dt.astype(f32))[..., None]
    mag = jnp.exp(ar * dt)
    lr, li = mag * jnp.cos(ai * dt), mag * jnp.sin(ai * dt)
    den = ar * ar + ai * ai
    fr = ((lr - 1.0) * ar + li * ai) / den
    fi = (li * ar - (lr - 1.0) * ai) / den
    br, bi = b_re.astype(f32), b_im.astype(f32)
    bbr = fr[..., None] * br - fi[..., None] * bi
    bbi = fr[..., None] * bi + fi[..., None] * br
    cr, ci = c_re.astype(f32), c_im.astype(f32)
    kk = jnp.arange(L + 1, dtype=f32)
    pmag = jnp.exp(ar[..., None] * dt[..., None] * kk)
    pang = ai[..., None] * dt[..., None] * kk
    pr, pi = pmag * jnp.cos(pang), pmag * jnp.sin(pang)
    er = cr[..., None] * pr[:, :, None] - ci[..., None] * pi[:, :, None]
    ei = cr[..., None] * pi[:, :, None] + ci[..., None] * pr[:, :, None]
    klag = (jnp.einsum('dgopk,dgpi->dgkoi', er[..., :L], bbr, precision=HIGHEST)
            - jnp.einsum('dgopk,dgpi->dgkoi', ei[..., :L], bbi, precision=HIGHEST))
    kf, kb = klag[0], klag[1]
    kfull = jnp.concatenate([kb[:, 1:][:, ::-1], (kf[:, 0] + kb[:, 0])[:, None], kf[:, 1:]], axis=1)
    s_idx = jnp.arange(L)[:, None]
    t_idx = jnp.arange(L)[None, :]
    w = kfull[:, t_idx - s_idx + L - 1]
    w = jnp.transpose(w, (0, 1, 4, 2, 3)).reshape(n_g, L * n_h, L * n_h)
    def in_to_state(d, pw_idx):
        pwr = jnp.take(pr[d], pw_idx, axis=-1)
        pwi = jnp.take(pi[d], pw_idx, axis=-1)
        re = pwr[..., None] * bbr[d][:, :, None, :] - pwi[..., None] * bbi[d][:, :, None, :]
        im = pwr[..., None] * bbi[d][:, :, None, :] + pwi[..., None] * bbr[d][:, :, None, :]
        m = jnp.concatenate([re, im], axis=1)
        return jnp.transpose(m, (0, 2, 3, 1)).reshape(n_g, L * n_h, 2 * n_p)
    pcat = jnp.concatenate([in_to_state(0, L - 1 - jnp.arange(L)), in_to_state(1, jnp.arange(L))], axis=-1)
    def state_to_out(d, pw_idx):
        e_r = jnp.take(er[d], pw_idx, axis=-1)
        e_i = jnp.take(ei[d], pw_idx, axis=-1)
        m = jnp.concatenate([e_r, -e_i], axis=2)
        return jnp.transpose(m, (0, 2, 3, 1)).reshape(n_g, 2 * n_p, L * n_h)
    qcat = jnp.concatenate([state_to_out(0, jnp.arange(L) + 1), state_to_out(1, L - jnp.arange(L))], axis=1)

    def scan_mult(d):
        a_r, a_i = pr[d][..., L], pi[d][..., L]
        m1 = jnp.concatenate([a_r, a_r], axis=-1)
        m2 = jnp.concatenate([-a_i, a_i], axis=-1)
        lay = lambda m: jnp.transpose(m.reshape(n_g // SSM_GROUP_TILE, SSM_GROUP_TILE, 2 * n_p),
                                      (1, 0, 2)).reshape(SSM_GROUP_TILE, -1)
        return lay(m1), lay(m2)
    af1, af2 = scan_mult(0)
    ab1, ab2 = scan_mult(1)
    return w.astype(BF16), pcat.astype(BF16), qcat.astype(BF16), (af1, af2, ab1, ab2)


def _s5_states_kernel(u_ref, p_ref, sf_ref, sb_ref):
    half = sf_ref.shape[-1]
    for r in range(u_ref.shape[0]):
        s = jnp.dot(u_ref[r], p_ref[r], preferred_element_type=F32)
        sf_ref[:, r, :] = s[:, :half]
        sb_ref[:, r, :] = s[:, half:]


def _s5_states(ug, pcat, tc):
    n_g, nc, kdim = ug.shape
    gt = SSM_GROUP_TILE
    half = pcat.shape[-1] // 2
    out = jax.ShapeDtypeStruct((nc, gt, (n_g // gt) * half), F32)
    return pl.pallas_call(
        _s5_states_kernel,
        grid=(n_g // gt, nc // tc),
        in_specs=[pl.BlockSpec((gt, tc, kdim), lambda j, c: (j, c, 0)),
                  pl.BlockSpec((gt, kdim, 2 * half), lambda j, c: (j, 0, 0))],
        out_specs=[pl.BlockSpec((tc, gt, half), lambda j, c: (c, 0, j))] * 2,
        out_shape=[out, out],
        compiler_params=_cparams("arbitrary", "arbitrary"),
        name="s5_chunk_states",
    )(ug, pcat)


def _s5_scan_kernel(rf_ref, rb_ref, sf_ref, sb_ref, af1_ref, af2_ref, ab1_ref, ab2_ref,
                    xf_ref, xb_ref, cf_ref, cb_ref, *, tcs, nblk):
    i = pl.program_id(0)

    @pl.when(i == 0)
    def _():
        cf_ref[...] = jnp.zeros(cf_ref.shape, F32)
        cb_ref[...] = jnp.zeros(cb_ref.shape, F32)

    width = cf_ref.shape[-1]

    def swap(x):
        return jnp.concatenate(
            [pltpu.roll(x[:, j * LANES:(j + 1) * LANES], LANES // 2, 1) for j in range(width // LANES)],
            axis=1)

    af1, af2 = af1_ref[...], af2_ref[...]
    ab1, ab2 = ab1_ref[...], ab2_ref[...]

    def fwd(r, carry):
        carry = carry * (1 - rf_ref[i * tcs + r]).astype(F32)
        xf_ref[r] = carry
        return af1 * carry + af2 * swap(carry) + sf_ref[r]

    cf_ref[...] = lax.fori_loop(0, tcs, fwd, cf_ref[...])

    def bwd(rr, carry):
        r = tcs - 1 - rr
        carry = carry * (1 - rb_ref[(nblk - 1 - i) * tcs + r]).astype(F32)
        xb_ref[r] = carry
        return ab1 * carry + ab2 * swap(carry) + sb_ref[r]

    cb_ref[...] = lax.fori_loop(0, tcs, bwd, cb_ref[...])


def _s5_scan(sf, sb, mults, reset_f, reset_b, tcs):
    nc, gt, width = sf.shape
    nblk = nc // tcs
    fblk = pl.BlockSpec((tcs, gt, width), lambda i, a, b: (i, 0, 0))
    bblk = pl.BlockSpec((tcs, gt, width), lambda i, a, b: (nblk - 1 - i, 0, 0))
    tab = pl.BlockSpec((gt, width), lambda i, a, b: (0, 0))
    out = jax.ShapeDtypeStruct((nc, gt, width), F32)
    return pl.pallas_call(
        functools.partial(_s5_scan_kernel, tcs=tcs, nblk=nblk),
        grid_spec=pltpu.PrefetchScalarGridSpec(
            num_scalar_prefetch=2, grid=(nblk,),
            in_specs=[fblk, bblk, tab, tab, tab, tab],
            out_specs=[fblk, bblk],
            scratch_shapes=[pltpu.VMEM((gt, width), F32), pltpu.VMEM((gt, width), F32)]),
        out_shape=[out, out],
        compiler_params=_cparams("arbitrary"),
        name="s5_chunk_scan",
    )(reset_f, reset_b, sf, sb, *mults)


def _s5_out_kernel(u_ref, xf_ref, xb_ref, w_ref, q_ref, y_ref):
    for r in range(u_ref.shape[0]):
        xcat = jnp.concatenate([xf_ref[:, r, :], xb_ref[:, r, :]], axis=-1).astype(BF16)
        y_ref[r] = (jnp.dot(u_ref[r], w_ref[r], preferred_element_type=F32)
                    + jnp.dot(xcat, q_ref[r], preferred_element_type=F32))


def _s5_out(ug, xf, xb, w, qcat, tc):
    n_g, nc, kdim = ug.shape
    gt = SSM_GROUP_TILE
    half = xf.shape[-1] // (n_g // gt)
    xblk = pl.BlockSpec((tc, gt, half), lambda j, c: (c, 0, j))
    return pl.pallas_call(
        _s5_out_kernel,
        grid=(n_g // gt, nc // tc),
        in_specs=[pl.BlockSpec((gt, tc, kdim), lambda j, c: (j, c, 0)), xblk, xblk,
                  pl.BlockSpec((gt, kdim, kdim), lambda j, c: (j, 0, 0)),
                  pl.BlockSpec((gt, 2 * half, kdim), lambda j, c: (j, 0, 0))],
        out_specs=pl.BlockSpec((gt, tc, kdim), lambda j, c: (j, c, 0)),
        out_shape=jax.ShapeDtypeStruct((n_g, nc, kdim), F32),
        compiler_params=_cparams("arbitrary", "arbitrary"),
        name="s5_chunk_outputs",
    )(ug, xf, xb, w, qcat)


def _s5_glu_kernel(x_ref, y_ref, mod_ref, g_ref, dskip_ref, w_ref, o_ref):
    d = x_ref.shape[1]
    x = x_ref[...]
    h = _rms_mod(x, g_ref[...], mod_ref[0, 1:2, :], mod_ref[0, 0:1, :])
    z = jax.nn.gelu(y_ref[...] + dskip_ref[...] * h)
    vg = jnp.dot(z.astype(BF16), w_ref[...], preferred_element_type=F32)
    m = vg[:, :d] * jax.nn.sigmoid(vg[:, d:])
    o_ref[...] = x + mod_ref[0, 2:3, :] * m


def _s5_glu(x, y, modl, g, d_skip, w_bf, seg_len, tm):
    t, d = x.shape
    per_seg = seg_len // tm
    tok = pl.BlockSpec((tm, d), lambda i: (i, 0))
    vec = pl.BlockSpec((1, d), lambda i: (0, 0))
    return pl.pallas_call(
        _s5_glu_kernel,
        grid=(t // tm,),
        in_specs=[tok, tok, pl.BlockSpec((1, SUBLANES, d), lambda i: (i // per_seg, 0, 0)), vec, vec,
                  pl.BlockSpec((d, 2 * d), lambda i: (0, 0))],
        out_specs=tok,
        out_shape=jax.ShapeDtypeStruct((t, d), F32),
        compiler_params=_cparams("arbitrary"),
        name="s5_glu_residual",
    )(x, y, modl, g, d_skip, w_bf)


def _s5_layer(x, modl, g, params, seg_len, seq_first, seq_last, tm):
    a_re, a_im, log_dt, b_re, b_im, c_re, c_im, d_skip, w_glu = params
    t, d = x.shape
    L = SSM_CHUNK
    n_g = d // GROUP_CH
    nc = t // L
    w, pcat, qcat, mults = _s5_tables(a_re, a_im, log_dt, b_re, b_im, c_re, c_im)
    hb = _norm_bf16(x, modl, g, seg_len, tm)
    ug = hb.reshape(nc, L, n_g, GROUP_CH).transpose(2, 0, 1, 3).reshape(n_g, nc, L * GROUP_CH)
    tc = math.gcd(nc, SSM_ROW_TILE)
    sf, sb = _s5_states(ug, pcat, tc)
    chunks_per_seg = seg_len // L
    reset_f = jnp.repeat(jnp.asarray(seq_first, jnp.int32), chunks_per_seg) * (
        jnp.tile(jnp.arange(chunks_per_seg) == 0, len(seq_first))).astype(jnp.int32)
    reset_b = jnp.repeat(jnp.asarray(seq_last, jnp.int32), chunks_per_seg) * (
        jnp.tile(jnp.arange(chunks_per_seg) == chunks_per_seg - 1, len(seq_last))).astype(jnp.int32)
    xf, xb = _s5_scan(sf, sb, mults, reset_f, reset_b, math.gcd(nc, SSM_SCAN_TILE))
    yg = _s5_out(ug, xf, xb, w, qcat, tc)
    y = yg.reshape(n_g, nc, L, GROUP_CH).transpose(1, 2, 0, 3).reshape(t, d)
    return _s5_glu(x, y, modl, g, d_skip.reshape(1, d), w_glu.astype(BF16), seg_len, tm)


def _router_kernel(x_ref, mod_ref, g_ref, w_ref, b_ref, h_ref, idx_ref, gate_ref, rank_ref, cnt_ref,
                   carry_ref):
    tm = x_ref.shape[0]
    n_e = w_ref.shape[0]

    @pl.when(pl.program_id(0) == 0)
    def _():
        carry_ref[...] = jnp.zeros(carry_ref.shape, F32)

    h = _rms_mod(x_ref[...], g_ref[...], mod_ref[0, 4:5, :], mod_ref[0, 3:4, :])
    h_ref[...] = h
    logits = lax.dot_general(w_ref[...], h, (((1,), (1,)), ((), ())), precision=HIGHEST,
                             preferred_element_type=F32) + b_ref[...]
    e_iota = lax.broadcasted_iota(jnp.int32, (n_e, tm), 0)
    work = logits
    chosen = jnp.zeros((n_e, tm), F32)
    vals, idxs = [], []
    for _ in range(TOP_K):
        m = jnp.max(work, axis=0, keepdims=True)
        ix = jnp.min(jnp.where(work == m, e_iota, n_e), axis=0, keepdims=True)
        hit = e_iota == ix
        work = jnp.where(hit, -jnp.inf, work)
        chosen = jnp.where(hit, 1.0, chosen)
        vals.append(m)
        idxs.append(ix)
    v = jnp.concatenate(vals, axis=0)
    ex = jnp.exp(v - v[0:1])
    gate_ref[...] = ex / jnp.sum(ex, axis=0, keepdims=True)
    idx_ref[...] = jnp.concatenate(idxs, axis=0)
    row = lax.broadcasted_iota(jnp.int32, (tm, tm), 0)
    col = lax.broadcasted_iota(jnp.int32, (tm, tm), 1)
    before = (row < col).astype(BF16)
    cum = jnp.dot(chosen.astype(BF16), before, preferred_element_type=F32) + carry_ref[...]
    ranks = [jnp.sum(jnp.where(e_iota == ix, cum, 0.0), axis=0, keepdims=True) for ix in idxs]
    rank_ref[...] = jnp.concatenate(ranks, axis=0).astype(jnp.int32)
    carry_ref[...] = carry_ref[...] + jnp.sum(chosen, axis=1, keepdims=True)
    cnt_ref[...] = jnp.broadcast_to(carry_ref[...], cnt_ref.shape).astype(jnp.int32)


def _router(x, modl, g, w_router_t, b_router, seg_len, tm):
    t, d = x.shape
    n_e = w_router_t.shape[0]
    per_seg = seg_len // tm
    sel = pl.BlockSpec((TOP_K, tm), lambda i: (0, i))
    return pl.pallas_call(
        _router_kernel,
        grid=(t // tm,),
        in_specs=[pl.BlockSpec((tm, d), lambda i: (i, 0)),
                  pl.BlockSpec((1, SUBLANES, d), lambda i: (i // per_seg, 0, 0)),
                  pl.BlockSpec((1, d), lambda i: (0, 0)),
                  pl.BlockSpec((n_e, d), lambda i: (0, 0)),
                  pl.BlockSpec((n_e, 1), lambda i: (0, 0))],
        out_specs=[pl.BlockSpec((tm, d), lambda i: (i, 0)), sel, sel, sel,
                   pl.BlockSpec((n_e, LANES), lambda i: (0, 0))],
        out_shape=[jax.ShapeDtypeStruct((t, d), F32),
                   jax.ShapeDtypeStruct((TOP_K, t), jnp.int32),
                   jax.ShapeDtypeStruct((TOP_K, t), F32),
                   jax.ShapeDtypeStruct((TOP_K, t), jnp.int32),
                   jax.ShapeDtypeStruct((n_e, LANES), jnp.int32)],
        scratch_shapes=[pltpu.VMEM((n_e, 1), F32)],
        compiler_params=_cparams("arbitrary"),
        name="moe_router",
    )(x, modl, g, w_router_t, b_router)


def _dispatch_kernel(dest_ref, h_ref, init_hbm, xs_hbm, sem):
    del init_hbm
    tm = h_ref.shape[0]

    def issue(j, carry):
        for k in range(TOP_K):
            pltpu.make_async_copy(h_ref.at[pl.ds(j, 1), :],
                                  xs_hbm.at[pl.ds(dest_ref[0, 0, j * TOP_K + k], 1), :], sem).start()
        return carry

    lax.fori_loop(0, tm, issue, 0)
    for _ in range(TOP_K):
        pltpu.make_async_copy(h_ref, xs_hbm.at[pl.ds(0, tm), :], sem).wait()


def _dispatch(h, dest_tk, n_rows, tm):
    t, d = h.shape
    return pl.pallas_call(
        _dispatch_kernel,
        grid=(t // tm,),
        in_specs=[pl.BlockSpec((1, 1, tm * TOP_K), lambda i: (i, 0, 0), memory_space=pltpu.SMEM),
                  pl.BlockSpec((tm, d), lambda i: (i, 0)),
                  pl.BlockSpec(memory_space=pl.ANY)],
        out_specs=pl.BlockSpec(memory_space=pl.ANY),
        out_shape=jax.ShapeDtypeStruct((n_rows, d), F32),
        scratch_shapes=[pltpu.SemaphoreType.DMA(())],
        input_output_aliases={2: 0},
        compiler_params=_cparams("arbitrary"),
        name="moe_dispatch",
    )(dest_tk.reshape(t // tm, 1, tm * TOP_K), h, jnp.zeros((n_rows, d), F32))


def _expert_kernel(be_ref, bi_ref, nu_ref, x_ref, wgu_ref, bgu_ref, wd_ref, bd_ref, y_ref):
    del be_ref, bi_ref
    n_ff = wd_ref.shape[1]

    @pl.when(pl.program_id(0) < nu_ref[0])
    def _():
        x = x_ref[...].astype(BF16)
        acc = jnp.zeros(y_ref.shape, F32) + bd_ref[0]
        for c in range(n_ff // FF_CHUNK):
            lo = c * FF_CHUNK
            g_part = (jnp.dot(x, wgu_ref[0, :, lo:lo + FF_CHUNK], preferred_element_type=F32)
                      + bgu_ref[0, :, lo:lo + FF_CHUNK])
            u_part = (jnp.dot(x, wgu_ref[0, :, n_ff + lo:n_ff + lo + FF_CHUNK], preferred_element_type=F32)
                      + bgu_ref[0, :, n_ff + lo:n_ff + lo + FF_CHUNK])
            g_part = jnp.minimum(g_part, SWIGLU_LIMIT)
            u_part = jnp.clip(u_part, -SWIGLU_LIMIT, SWIGLU_LIMIT)
            act = (u_part + 1.0) * g_part * jax.nn.sigmoid(SWIGLU_ALPHA * g_part)
            acc = acc + jnp.dot(act.astype(BF16), wd_ref[0, lo:lo + FF_CHUNK, :],
                                preferred_element_type=F32)
        y_ref[...] = acc


def _experts(xs, blk_expert, blk_idx, n_used, w_gu_bf, b_gu, w_down_bf, b_down, tme):
    nr, d = xs.shape
    n_e, _, two_f = w_gu_bf.shape
    n_ff = two_f // 2
    nblk = nr // tme
    rows = pl.BlockSpec((tme, d), lambda b, be, bi, nu: (bi[b], 0))
    return pl.pallas_call(
        _expert_kernel,
        grid_spec=pltpu.PrefetchScalarGridSpec(
            num_scalar_prefetch=3, grid=(nblk,),
            in_specs=[rows,
                      pl.BlockSpec((1, d, two_f), lambda b, be, bi, nu: (be[b], 0, 0)),
                      pl.BlockSpec((1, 1, two_f), lambda b, be, bi, nu: (be[b], 0, 0)),
                      pl.BlockSpec((1, n_ff, d), lambda b, be, bi, nu: (be[b], 0, 0)),
                      pl.BlockSpec((1, 1, d), lambda b, be, bi, nu: (be[b], 0, 0))],
            out_specs=rows),
        out_shape=jax.ShapeDtypeStruct(xs.shape, F32),
        compiler_params=_cparams("arbitrary"),
        name="moe_experts",
    )(blk_expert, blk_idx, n_used, xs, w_gu_bf, b_gu.reshape(n_e, 1, two_f), w_down_bf,
      b_down.reshape(n_e, 1, d))


def _combine_kernel(dest_ref, x_ref, gate_ref, mod_ref, g_ref, ys_hbm, o_ref, buf, sem, *, final_norm):
    tm = x_ref.shape[0]

    def issue(j, carry):
        for k in range(TOP_K):
            pltpu.make_async_copy(ys_hbm.at[pl.ds(dest_ref[0, 0, j * TOP_K + k], 1), :],
                                  buf.at[k, pl.ds(j, 1), :], sem).start()
        return carry

    lax.fori_loop(0, tm, issue, 0)
    for k in range(TOP_K):
        pltpu.make_async_copy(ys_hbm.at[pl.ds(0, tm), :], buf.at[k], sem).wait()
    gates = gate_ref[...]
    moe = gates[:, 0:1] * buf[0]
    for k in range(1, TOP_K):
        moe = moe + gates[:, k:k + 1] * buf[k]
    x = x_ref[...] + mod_ref[0, 5:6, :] * moe
    if final_norm:
        ms = jnp.mean(x * x, axis=-1, keepdims=True)
        x = x * lax.rsqrt(ms + NORM_EPS) * g_ref[...]
    o_ref[...] = x


def _combine(x, ys, dest_tk, gates, modl, final_g, seg_len, tm, final_norm, row0=0, rows=None):
    t, d = x.shape
    rows = t if rows is None else rows
    per_seg = seg_len // tm
    b0 = row0 // tm
    return pl.pallas_call(
        functools.partial(_combine_kernel, final_norm=final_norm),
        grid=(rows // tm,),
        in_specs=[pl.BlockSpec((1, 1, tm * TOP_K), lambda i: (i + b0, 0, 0), memory_space=pltpu.SMEM),
                  pl.BlockSpec((tm, d), lambda i: (i + b0, 0)),
                  pl.BlockSpec((tm, TOP_K), lambda i: (i + b0, 0)),
                  pl.BlockSpec((1, SUBLANES, d), lambda i: ((i + b0) // per_seg, 0, 0)),
                  pl.BlockSpec((1, d), lambda i: (0, 0)),
                  pl.BlockSpec(memory_space=pl.ANY)],
        out_specs=pl.BlockSpec((tm, d), lambda i: (i, 0)),
        out_shape=jax.ShapeDtypeStruct((rows, d), F32),
        scratch_shapes=[pltpu.VMEM((TOP_K, tm, d), F32), pltpu.SemaphoreType.DMA(())],
        compiler_params=_cparams("arbitrary"),
        name="moe_combine",
    )(dest_tk.reshape(t // tm, 1, tm * TOP_K), x, gates, modl, final_g, ys)


def _moe_dispatch_and_experts(x, modl, g, w_router, b_router, w_gu, b_gu, w_down, b_down, seg_len, tm):
    t, d = x.shape
    n_e = w_router.shape[1]
    tme = math.gcd(t * TOP_K, EXPERT_ROW_TILE)
    h, idx_t, gate_t, rank_t, cnt = _router(x, modl, g, w_router.T, b_router.reshape(n_e, 1), seg_len, tm)
    counts = cnt[:, 0]
    padded = (counts + tme - 1) // tme * tme
    pad_end = jnp.cumsum(padded)
    pad_start = pad_end - padded
    nblk = (t * TOP_K) // tme + n_e
    n_used = (pad_end[-1] // tme).astype(jnp.int32).reshape(1)
    blk_idx = jnp.minimum(jnp.arange(nblk, dtype=jnp.int32), n_used[0] - 1)
    blk_expert = jnp.minimum(jnp.searchsorted(pad_end, blk_idx * tme, side='right'), n_e - 1).astype(jnp.int32)
    start_of = jnp.sum(jnp.where(idx_t[None] == jnp.arange(n_e, dtype=jnp.int32)[:, None, None],
                                 pad_start[:, None, None], 0), axis=0)
    dest = (start_of + rank_t).astype(jnp.int32)
    dest_tk = dest.T.reshape(-1)
    xs = _dispatch(h, dest_tk, nblk * tme, tm)
    ys = _experts(xs, blk_expert, blk_idx, n_used, w_gu.astype(BF16), b_gu, w_down.astype(BF16), b_down, tme)
    return ys, dest_tk, gate_t.T


def kernel(x_prompt, x_sample, c_prompt, c_sample, norm1_g, norm2_g, final_g, w_ada, b_ada, w_qkv, w_o,
           lam_q1, lam_k1, lam_q2, lam_k2, subln_g, ssm_a_re, ssm_a_im, ssm_log_dt, ssm_b_re, ssm_b_im,
           ssm_c_re, ssm_c_im, ssm_d, ssm_w_glu, w_router, b_router, w_gu, b_gu, w_down, b_down):
    n_b, seg_len, d = x_prompt.shape
    n_bs, s_len, _ = x_sample.shape
    depth = w_ada.shape[0]
    assert s_len % seg_len == 0 and s_len // seg_len in (1, 2)
    per_sample = s_len // seg_len
    n_seg = n_b + n_bs * per_sample
    t = n_seg * seg_len
    tm = math.gcd(seg_len, TOKEN_TILE)

    seg_seq = list(range(n_b)) + [n_b + j for j in range(n_bs) for _ in range(per_sample)]
    seg_pos = [0] * n_b + [r for _ in range(n_bs) for r in range(per_sample)]
    seg_a = list(range(n_b)) + [n_b + j * per_sample for j in range(n_bs) for _ in range(per_sample)]
    seg_b = list(range(n_b)) + [n_b + j * per_sample + per_sample - 1 for j in range(n_bs) for _ in range(per_sample)]
    two = [0] * n_b + [int(per_sample == 2)] * (n_bs * per_sample)
    seq_first = [1] * n_b + [int(r == 0) for _ in range(n_bs) for r in range(per_sample)]
    seq_last = [1] * n_b + [int(r == per_sample - 1) for _ in range(n_bs) for r in range(per_sample)]
    per_seg = seg_len // tm
    pos_blk = jnp.asarray([seg_pos[s] * per_seg + r for s in range(n_seg) for r in range(per_seg)], jnp.int32)

    x = jnp.concatenate([x_prompt.reshape(n_b * seg_len, d), x_sample.reshape(n_bs * s_len, d)], axis=0)

    n_c = n_b + n_bs
    c_rows = -(-n_c // SUBLANES) * SUBLANES
    c_all = jnp.concatenate([c_prompt, c_sample, jnp.zeros((c_rows - n_c, d), F32)], axis=0)
    mod = _ada(c_all, w_ada, b_ada)
    mod = mod[:, jnp.asarray(seg_seq)].reshape(depth, n_seg, 6, d)
    mod = jnp.concatenate([mod, jnp.zeros((depth, n_seg, SUBLANES - 6, d), F32)], axis=2)

    inv_freq = ROPE_THETA ** (-jnp.arange(0, HEAD_DIM, 2, dtype=F32) / HEAD_DIM)
    ang = jnp.arange(s_len, dtype=F32)[:, None] * inv_freq[None, :]
    cos_t = jnp.tile(jnp.cos(ang), (1, 2 * LANES // HEAD_DIM))
    sin_t = jnp.tile(jnp.concatenate([-jnp.sin(ang), jnp.sin(ang)], axis=1), (1, LANES // HEAD_DIM))

    out_p = out_s = None
    for i in range(depth):
        modl = mod[i]
        g1 = norm1_g[i].reshape(1, d)
        g2 = norm2_g[i].reshape(1, d)
        j = i // 2
        if i % 2 == 0:
            lambda_init = 0.8 - 0.6 * math.exp(-0.3 * i)
            q, k, v = _qkv(x, modl, g1, w_qkv[j].astype(BF16), cos_t, sin_t, pos_blk, seg_len, tm)
            lam_pack = jnp.zeros((SUBLANES, 2 * HEAD_DIM), F32).at[0:4, 0:HEAD_DIM].set(
                jnp.stack([lam_q1[j], lam_k1[j], lam_q2[j], lam_k2[j]]).astype(F32))
            o = _flash(q, k, v, lam_pack, subln_g[j].reshape(1, 2 * HEAD_DIM).astype(F32),
                       jnp.asarray(seg_a, jnp.int32), jnp.asarray(seg_b, jnp.int32),
                       jnp.asarray(two, jnp.int32), seg_len, lambda_init)
            x = _proj_res(o, w_o[j].astype(BF16), x, modl, seg_len, tm, gate_row=2)
        else:
            params = (ssm_a_re[j], ssm_a_im[j], ssm_log_dt[j], ssm_b_re[j], ssm_b_im[j], ssm_c_re[j],
                      ssm_c_im[j], ssm_d[j], ssm_w_glu[j])
            x = _s5_layer(x, modl, g1, params, seg_len, seq_first, seq_last, tm)
        ys, dest_tk, gates = _moe_dispatch_and_experts(x, modl, g2, w_router[i], b_router[i], w_gu[i], b_gu[i],
                                                       w_down[i], b_down[i], seg_len, tm)
        fg = final_g.reshape(1, d)
        if i == depth - 1:
            out_p = _combine(x, ys, dest_tk, gates, modl, fg, seg_len, tm, True, 0, n_b * seg_len)
            out_s = _combine(x, ys, dest_tk, gates, modl, fg, seg_len, tm, True, n_b * seg_len, n_bs * s_len)
        else:
            x = _combine(x, ys, dest_tk, gates, modl, fg, seg_len, tm, False)
    return out_p.reshape(n_b, seg_len, d), out_s.reshape(n_bs, s_len, d)
```

```python
import functools
import math

import jax
import jax.numpy as jnp
from jax import lax
from jax.experimental import pallas as pl
from jax.experimental.pallas import tpu as pltpu

F32 = jnp.float32
BF16 = jnp.bfloat16
HIGHEST = lax.Precision.HIGHEST

N_DIFF_HEADS = 8
HEAD_DIM = 64
ROPE_THETA = 10000.0
SUBLN_EPS = 1e-5
NORM_EPS = 1e-6
GROUP_CH = 16
STATE_DIM = 64
N_EXPERTS = 32
TOP_K = 4
SWIGLU_ALPHA = 1.702
SWIGLU_LIMIT = 7.0

LANES = 128
SUBLANES = 8
VMEM_LIMIT_BYTES = 56 * 1024 * 1024

TOKEN_TILE = 512
ATT_Q_TILE = 256
ATT_UNROLL = 4
VT_ONES_ROWS = 16
LOG2_E = math.log2(math.e)
SSM_CHUNK = 16
SSM_GROUP_TILE = 8
SSM_ROW_TILE = 512
SSM_SCAN_TILE = 64
EXPERT_ROW_TILE = 512
FF_CHUNK = 512


def _cparams(*sem):
    return pltpu.CompilerParams(dimension_semantics=sem, vmem_limit_bytes=VMEM_LIMIT_BYTES)


def _rms_mod(x, g, scale, shift):
    ms = jnp.mean(x * x, axis=-1, keepdims=True)
    return x * lax.rsqrt(ms + NORM_EPS) * g * (1.0 + scale) + shift


def _ada_kernel(c_ref, w_ref, b_ref, o_ref):
    c = c_ref[...]
    cond = c * jax.nn.sigmoid(c)
    o_ref[0] = jnp.dot(cond, w_ref[0], precision=HIGHEST, preferred_element_type=F32) + b_ref[0]


def _ada(c_pad, w_ada, b_ada):
    depth, d, n6 = w_ada.shape
    rows = c_pad.shape[0]
    tn = 1536 if n6 % 1536 == 0 else n6
    return pl.pallas_call(
        _ada_kernel,
        grid=(depth, n6 // tn),
        in_specs=[pl.BlockSpec((rows, d), lambda i, j: (0, 0)),
                  pl.BlockSpec((1, d, tn), lambda i, j: (i, 0, j)),
                  pl.BlockSpec((1, 1, tn), lambda i, j: (i, 0, j))],
        out_specs=pl.BlockSpec((1, rows, tn), lambda i, j: (i, 0, j)),
        out_shape=jax.ShapeDtypeStruct((depth, rows, n6), F32),
        compiler_params=_cparams("arbitrary", "arbitrary"),
        name="ada_mod",
    )(c_pad, w_ada, b_ada.reshape(depth, 1, n6))


def _qkv_kernel(pos_ref, x_ref, mod_ref, g_ref, w_ref, wvt_ref, cos_ref, sin_ref, q_ref, k_ref, vt_ref):
    del pos_ref
    d = x_ref.shape[1]
    h = _rms_mod(x_ref[...], g_ref[...], mod_ref[0, 1:2, :], mod_ref[0, 0:1, :]).astype(BF16)
    qkv = jnp.dot(h, w_ref[...], preferred_element_type=F32)
    vt = lax.dot_general(wvt_ref[...], h, (((1,), (1,)), ((), ())), preferred_element_type=F32)
    ones = jnp.ones((VT_ONES_ROWS, vt.shape[1]), BF16)
    for hd in range(vt_ref.shape[0]):
        vt_ref[hd, 0, 0:LANES, :] = vt[hd * LANES:(hd + 1) * LANES, :].astype(BF16)
        vt_ref[hd, 0, LANES:LANES + VT_ONES_ROWS, :] = ones
    cos = cos_ref[...]
    sin = sin_ref[...]
    lane = lax.broadcasted_iota(jnp.int32, cos.shape, 1)
    first_half = (lane % HEAD_DIM) < (HEAD_DIM // 2)

    def rotary(t):
        partner = jnp.where(first_half,
                            pltpu.roll(t, LANES - HEAD_DIM // 2, 1),
                            pltpu.roll(t, HEAD_DIM // 2, 1))
        return t * cos + partner * sin

    for j in range(d // LANES):
        sl = slice(j * LANES, (j + 1) * LANES)
        q_ref[:, sl] = (rotary(qkv[:, sl]) * (HEAD_DIM ** -0.5 * LOG2_E)).astype(BF16)
        k_ref[:, sl] = rotary(qkv[:, d + j * LANES:d + (j + 1) * LANES]).astype(BF16)


def _qkv(x, modl, g, w_qkv, cos_t, sin_t, pos_blk, seg_len, tm):
    t, d = x.shape
    per_seg = seg_len // tm
    n_vh = d // LANES
    tok = lambda i, p: (i, 0)
    out = jax.ShapeDtypeStruct((t, d), BF16)
    w_qk = w_qkv[:, :2 * d].astype(BF16)
    w_vt = w_qkv[:, 2 * d:].T.astype(BF16)
    return pl.pallas_call(
        _qkv_kernel,
        grid_spec=pltpu.PrefetchScalarGridSpec(
            num_scalar_prefetch=1, grid=(t // tm,),
            in_specs=[pl.BlockSpec((tm, d), tok),
                      pl.BlockSpec((1, SUBLANES, d), lambda i, p: (i // per_seg, 0, 0)),
                      pl.BlockSpec((1, d), lambda i, p: (0, 0)),
                      pl.BlockSpec((d, 2 * d), lambda i, p: (0, 0)),
                      pl.BlockSpec((d, d), lambda i, p: (0, 0)),
                      pl.BlockSpec((tm, LANES), lambda i, p: (p[i], 0)),
                      pl.BlockSpec((tm, LANES), lambda i, p: (p[i], 0))],
            out_specs=[pl.BlockSpec((tm, d), tok), pl.BlockSpec((tm, d), tok),
                       pl.BlockSpec((n_vh, 1, LANES + VT_ONES_ROWS, tm), lambda i, p: (0, i, 0, 0))]),
        out_shape=[out, out, jax.ShapeDtypeStruct((n_vh, t // tm, LANES + VT_ONES_ROWS, tm), BF16)],
        compiler_params=_cparams("arbitrary"),
        name="qkv_rotary",
    )(pos_blk, x, modl, g, w_qk, w_vt, cos_t, sin_t)


def _flash_kernel(sa_ref, sb_ref, two_ref, q_ref, ka_ref, vta_ref, kb_ref, vtb_ref, lam_ref, g_ref,
                  o_ref, qt_ref, m_ref, acc_ref, *, tk, unroll, lambda_init):
    del sa_ref, sb_ref
    seg = pl.program_id(0)
    qt = q_ref[...].astype(F32).T
    chan = lax.broadcasted_iota(jnp.int32, qt.shape, 0)
    qt_ref[0] = jnp.where(chan < HEAD_DIM, qt, 0.0).astype(BF16)
    qt_ref[1] = jnp.where(chan >= HEAD_DIM, qt, 0.0).astype(BF16)
    m_ref[...] = jnp.full(m_ref.shape, -jnp.inf, F32)
    acc_ref[...] = jnp.zeros(acc_ref.shape, F32)

    def sweep(k_ref, vt_ref):
        units = [(tile, g) for tile in range(unroll) for g in range(2)]

        def scores(jj, u):
            tile, g = units[u]
            off = pl.multiple_of((jj * unroll + tile) * tk, tk)
            half = tk // 2
            return jnp.concatenate(
                [jnp.dot(k_ref[pl.ds(off + r * half, half), :], qt_ref[g], preferred_element_type=F32)
                 for r in range(2)], axis=0)

        def softmax(u, s):
            g = units[u][1]
            m_prev = m_ref[g]
            m_new = jnp.maximum(m_prev, jnp.max(s, axis=0, keepdims=True))
            p = jnp.exp2((s - m_new).astype(BF16))
            alpha = jnp.exp2(m_prev - m_new)
            m_ref[g] = m_new
            return p, alpha

        def values(jj, u, p, alpha):
            tile, g = units[u]
            acc_ref[g] = alpha * acc_ref[g] + jnp.dot(vt_ref[0, jj * unroll + tile], p,
                                                      preferred_element_type=F32)

        def body(jj, carry):
            s_next = scores(jj, 0)
            pending = None
            for u in range(len(units)):
                s_cur = s_next
                if u + 1 < len(units):
                    s_next = scores(jj, u + 1)
                if pending is not None:
                    values(jj, *pending)
                pending = (u,) + softmax(u, s_cur)
            values(jj, *pending)
            return carry
        lax.fori_loop(0, k_ref.shape[0] // (tk * unroll), body, 0)

    sweep(ka_ref, vta_ref)

    @pl.when(two_ref[seg] == 1)
    def _():
        sweep(kb_ref, vtb_ref)

    o1 = acc_ref[0, 0:LANES, :] / acc_ref[0, LANES:LANES + 1, :]
    o2 = acc_ref[1, 0:LANES, :] / acc_ref[1, LANES:LANES + 1, :]
    lam = (jnp.exp(jnp.sum(lam_ref[0:1, :] * lam_ref[1:2, :], axis=-1, keepdims=True))
           - jnp.exp(jnp.sum(lam_ref[2:3, :] * lam_ref[3:4, :], axis=-1, keepdims=True))
           + lambda_init)
    ot = o1 - lam * o2
    ms = jnp.mean(ot * ot, axis=0, keepdims=True)
    ot = ot * lax.rsqrt(ms + SUBLN_EPS) * g_ref[...] * (1.0 - lambda_init)
    o_ref[...] = ot.T.astype(BF16)


def _flash(q, k, vt, lam_pack, subln_g_col, seg_a, seg_b, two, seg_len, lambda_init):
    t, d = q.shape
    nseg = t // seg_len
    n_vh, _, v_rows, tk = vt.shape
    width = 2 * HEAD_DIM
    per_seg = seg_len // tk
    tq = min(ATT_Q_TILE, seg_len)
    nq = seg_len // tq
    kern = functools.partial(_flash_kernel, tk=tk, unroll=min(ATT_UNROLL, per_seg), lambda_init=lambda_init)
    k_a = pl.BlockSpec((seg_len, width), lambda s, h, i, sa, sb, tw: (sa[s], h))
    k_b = pl.BlockSpec((seg_len, width), lambda s, h, i, sa, sb, tw: (sb[s], h))
    vt_a = pl.BlockSpec((1, per_seg, v_rows, tk), lambda s, h, i, sa, sb, tw: (h, sa[s], 0, 0))
    vt_b = pl.BlockSpec((1, per_seg, v_rows, tk), lambda s, h, i, sa, sb, tw: (h, sb[s], 0, 0))
    qo = pl.BlockSpec((tq, width), lambda s, h, i, sa, sb, tw: (s * nq + i, h))
    return pl.pallas_call(
        kern,
        grid_spec=pltpu.PrefetchScalarGridSpec(
            num_scalar_prefetch=3, grid=(nseg, n_vh, nq),
            in_specs=[qo, k_a, vt_a, k_b, vt_b,
                      pl.BlockSpec((SUBLANES, width), lambda s, h, i, sa, sb, tw: (0, 0)),
                      pl.BlockSpec((width, 1), lambda s, h, i, sa, sb, tw: (0, 0))],
            out_specs=qo,
            scratch_shapes=[pltpu.VMEM((2, width, tq), BF16),
                            pltpu.VMEM((2, 1, tq), F32),
                            pltpu.VMEM((2, v_rows, tq), F32)]),
        out_shape=jax.ShapeDtypeStruct((t, d), BF16),
        compiler_params=_cparams("arbitrary", "arbitrary", "arbitrary"),
        name="diff_flash_attention",
    )(seg_a, seg_b, two, q, k, vt, k, vt, lam_pack, subln_g_col)


def _proj_res_kernel(a_ref, w_ref, x_ref, mod_ref, o_ref, *, gate_row):
    y = jnp.dot(a_ref[...], w_ref[...], preferred_element_type=F32)
    o_ref[...] = x_ref[...] + mod_ref[0, gate_row:gate_row + 1, :] * y


def _proj_res(a, w_bf, x, modl, seg_len, tm, gate_row):
    t, d = x.shape
    per_seg = seg_len // tm
    return pl.pallas_call(
        functools.partial(_proj_res_kernel, gate_row=gate_row),
        grid=(t // tm,),
        in_specs=[pl.BlockSpec((tm, a.shape[1]), lambda i: (i, 0)),
                  pl.BlockSpec(w_bf.shape, lambda i: (0, 0)),
                  pl.BlockSpec((tm, d), lambda i: (i, 0)),
                  pl.BlockSpec((1, SUBLANES, d), lambda i: (i // per_seg, 0, 0))],
        out_specs=pl.BlockSpec((tm, d), lambda i: (i, 0)),
        out_shape=jax.ShapeDtypeStruct((t, d), F32),
        compiler_params=_cparams("arbitrary"),
        name="proj_residual",
    )(a, w_bf, x, modl)


def _norm_bf16_kernel(x_ref, mod_ref, g_ref, o_ref):
    o_ref[...] = _rms_mod(x_ref[...], g_ref[...], mod_ref[0, 1:2, :], mod_ref[0, 0:1, :]).astype(BF16)


def _norm_bf16(x, modl, g, seg_len, tm):
    t, d = x.shape
    per_seg = seg_len // tm
    return pl.pallas_call(
        _norm_bf16_kernel,
        grid=(t // tm,),
        in_specs=[pl.BlockSpec((tm, d), lambda i: (i, 0)),
                  pl.BlockSpec((1, SUBLANES, d), lambda i: (i // per_seg, 0, 0)),
                  pl.BlockSpec((1, d), lambda i: (0, 0))],
        out_specs=pl.BlockSpec((tm, d), lambda i: (i, 0)),
        out_shape=jax.ShapeDtypeStruct((t, d), BF16),
        compiler_params=_cparams("arbitrary"),
        name="s5_prenorm",
    )(x, modl, g)


def _s5_tables(a_re, a_im, log_dt, b_re, b_im, c_re, c_im):
    f32 = F32
    L = SSM_CHUNK
    ar, ai = a_re.astype(f32), a_im.astype(f32)
    n_g, n_p = ar.shape[1], ar.shape[2]
    n_h = b_re.shape[-1]
    dt = jnp.exp(log_dt.astype(f32))[..., None]
    mag = jnp.exp(ar * dt)
    lr, li = mag * jnp.cos(ai * dt), mag * jnp.sin(ai * dt)
    den = ar * ar + ai * ai
    fr = ((lr - 1.0) * ar + li * ai) / den
    fi = (li * ar - (lr - 1.0) * ai) / den
    br, bi = b_re.astype(f32), b_im.astype(f32)
    bbr = fr[..., None] * br - fi[..., None] * bi
    bbi = fr[..., None] * bi + fi[..., None] * br
    cr, ci = c_re.astype(f32), c_im.astype(f32)
    kk = jnp.arange(L + 1, dtype=f32)
    pmag = jnp.exp(ar[..., None] * dt[..., None] * kk)
    pang = ai[..., None] * dt[..., None] * kk
    pr, pi = pmag * jnp.cos(pang), pmag * jnp.sin(pang)
    er = cr[..., None] * pr[:, :, None] - ci[..., None] * pi[:, :, None]
    ei = cr[..., None] * pi[:, :, None] + ci[..., None] * pr[:, :, None]
    klag = (jnp.einsum('dgopk,dgpi->dgkoi', er[..., :L], bbr, precision=HIGHEST)
            - jnp.einsum('dgopk,dgpi->dgkoi', ei[..., :L], bbi, precision=HIGHEST))
    kf, kb = klag[0], klag[1]
    kfull = jnp.concatenate([kb[:, 1:][:, ::-1], (kf[:, 0] + kb[:, 0])[:, None], kf[:, 1:]], axis=1)
    s_idx = jnp.arange(L)[:, None]
    t_idx = jnp.arange(L)[None, :]
    w = kfull[:, t_idx - s_idx + L - 1]
    w = jnp.transpose(w, (0, 1, 4, 2, 3)).reshape(n_g, L * n_h, L * n_h)
    def in_to_state(d, pw_idx):
        pwr = jnp.take(pr[d], pw_idx, axis=-1)
        pwi = jnp.take(pi[d], pw_idx, axis=-1)
        re = pwr[..., None] * bbr[d][:, :, None, :] - pwi[..., None] * bbi[d][:, :, None, :]
        im = pwr[..., None] * bbi[d][:, :, None, :] + pwi[..., None] * bbr[d][:, :, None, :]
        m = jnp.concatenate([re, im], axis=1)
        return jnp.transpose(m, (0, 2, 3, 1)).reshape(n_g, L * n_h, 2 * n_p)
    pcat = jnp.concatenate([in_to_state(0, L - 1 - jnp.arange(L)), in_to_state(1, jnp.arange(L))], axis=-1)
    def state_to_out(d, pw_idx):
        e_r = jnp.take(er[d], pw_idx, axis=-1)
        e_i = jnp.take(ei[d], pw_idx, axis=-1)
        m = jnp.concatenate([e_r, -e_i], axis=2)
        return jnp.transpose(m, (0, 2, 3, 1)).reshape(n_g, 2 * n_p, L * n_h)
    qcat = jnp.concatenate([state_to_out(0, jnp.arange(L) + 1), state_to_out(1, L - jnp.arange(L))], axis=1)

    def scan_mult(d):
        a_r, a_i = pr[d][..., L], pi[d][..., L]
        m1 = jnp.concatenate([a_r, a_r], axis=-1)
        m2 = jnp.concatenate([-a_i, a_i], axis=-1)
        lay = lambda m: jnp.transpose(m.reshape(n_g // SSM_GROUP_TILE, SSM_GROUP_TILE, 2 * n_p),
                                      (1, 0, 2)).reshape(SSM_GROUP_TILE, -1)
        return lay(m1), lay(m2)
    af1, af2 = scan_mult(0)
    ab1, ab2 = scan_mult(1)
    return w.astype(BF16), pcat.astype(BF16), qcat.astype(BF16), (af1, af2, ab1, ab2)


def _s5_states_kernel(u_ref, p_ref, sf_ref, sb_ref):
    half = sf_ref.shape[-1]
    for r in range(u_ref.shape[0]):
        s = jnp.dot(u_ref[r], p_ref[r], preferred_element_type=F32)
        sf_ref[:, r, :] = s[:, :half]
        sb_ref[:, r, :] = s[:, half:]


def _s5_states(ug, pcat, tc):
    n_g, nc, kdim = ug.shape
    gt = SSM_GROUP_TILE
    half = pcat.shape[-1] // 2
    out = jax.ShapeDtypeStruct((nc, gt, (n_g // gt) * half), F32)
    return pl.pallas_call(
        _s5_states_kernel,
        grid=(n_g // gt, nc // tc),
        in_specs=[pl.BlockSpec((gt, tc, kdim), lambda j, c: (j, c, 0)),
                  pl.BlockSpec((gt, kdim, 2 * half), lambda j, c: (j, 0, 0))],
        out_specs=[pl.BlockSpec((tc, gt, half), lambda j, c: (c, 0, j))] * 2,
        out_shape=[out, out],
        compiler_params=_cparams("arbitrary", "arbitrary"),
        name="s5_chunk_states",
    )(ug, pcat)


def _s5_scan_kernel(rf_ref, rb_ref, sf_ref, sb_ref, af1_ref, af2_ref, ab1_ref, ab2_ref,
                    xf_ref, xb_ref, cf_ref, cb_ref, *, tcs, nblk):
    i = pl.program_id(0)

    @pl.when(i == 0)
    def _():
        cf_ref[...] = jnp.zeros(cf_ref.shape, F32)
        cb_ref[...] = jnp.zeros(cb_ref.shape, F32)

    width = cf_ref.shape[-1]

    def swap(x):
        return jnp.concatenate(
            [pltpu.roll(x[:, j * LANES:(j + 1) * LANES], LANES // 2, 1) for j in range(width // LANES)],
            axis=1)

    af1, af2 = af1_ref[...], af2_ref[...]
    ab1, ab2 = ab1_ref[...], ab2_ref[...]

    def fwd(r, carry):
        carry = carry * (1 - rf_ref[i * tcs + r]).astype(F32)
        xf_ref[r] = carry
        return af1 * carry + af2 * swap(carry) + sf_ref[r]

    cf_ref[...] = lax.fori_loop(0, tcs, fwd, cf_ref[...])

    def bwd(rr, carry):
        r = tcs - 1 - rr
        carry = carry * (1 - rb_ref[(nblk - 1 - i) * tcs + r]).astype(F32)
        xb_ref[r] = carry
        return ab1 * carry + ab2 * swap(carry) + sb_ref[r]

    cb_ref[...] = lax.fori_loop(0, tcs, bwd, cb_ref[...])


def _s5_scan(sf, sb, mults, reset_f, reset_b, tcs):
    nc, gt, width = sf.shape
    nblk = nc // tcs
    fblk = pl.BlockSpec((tcs, gt, width), lambda i, a, b: (i, 0, 0))
    bblk = pl.BlockSpec((tcs, gt, width), lambda i, a, b: (nblk - 1 - i, 0, 0))
    tab = pl.BlockSpec((gt, width), lambda i, a, b: (0, 0))
    out = jax.ShapeDtypeStruct((nc, gt, width), F32)
    return pl.pallas_call(
        functools.partial(_s5_scan_kernel, tcs=tcs, nblk=nblk),
        grid_spec=pltpu.PrefetchScalarGridSpec(
            num_scalar_prefetch=2, grid=(nblk,),
            in_specs=[fblk, bblk, tab, tab, tab, tab],
            out_specs=[fblk, bblk],
            scratch_shapes=[pltpu.VMEM((gt, width), F32), pltpu.VMEM((gt, width), F32)]),
        out_shape=[out, out],
        compiler_params=_cparams("arbitrary"),
        name="s5_chunk_scan",
    )(reset_f, reset_b, sf, sb, *mults)


def _s5_out_kernel(u_ref, xf_ref, xb_ref, w_ref, q_ref, y_ref):
    for r in range(u_ref.shape[0]):
        xcat = jnp.concatenate([xf_ref[:, r, :], xb_ref[:, r, :]], axis=-1).astype(BF16)
        y_ref[r] = (jnp.dot(u_ref[r], w_ref[r], preferred_element_type=F32)
                    + jnp.dot(xcat, q_ref[r], preferred_element_type=F32))


def _s5_out(ug, xf, xb, w, qcat, tc):
    n_g, nc, kdim = ug.shape
    gt = SSM_GROUP_TILE
    half = xf.shape[-1] // (n_g // gt)
    xblk = pl.BlockSpec((tc, gt, half), lambda j, c: (c, 0, j))
    return pl.pallas_call(
        _s5_out_kernel,
        grid=(n_g // gt, nc // tc),
        in_specs=[pl.BlockSpec((gt, tc, kdim), lambda j, c: (j, c, 0)), xblk, xblk,
                  pl.BlockSpec((gt, kdim, kdim), lambda j, c: (j, 0, 0)),
                  pl.BlockSpec((gt, 2 * half, kdim), lambda j, c: (j, 0, 0))],
        out_specs=pl.BlockSpec((gt, tc, kdim), lambda j, c: (j, c, 0)),
        out_shape=jax.ShapeDtypeStruct((n_g, nc, kdim), F32),
        compiler_params=_cparams("arbitrary", "arbitrary"),
        name="s5_chunk_outputs",
    )(ug, xf, xb, w, qcat)


def _s5_glu_kernel(x_ref, y_ref, mod_ref, g_ref, dskip_ref, w_ref, o_ref):
    d = x_ref.shape[1]
    x = x_ref[...]
    h = _rms_mod(x, g_ref[...], mod_ref[0, 1:2, :], mod_ref[0, 0:1, :])
    z = jax.nn.gelu(y_ref[...] + dskip_ref[...] * h)
    vg = jnp.dot(z.astype(BF16), w_ref[...], preferred_element_type=F32)
    m = vg[:, :d] * jax.nn.sigmoid(vg[:, d:])
    o_ref[...] = x + mod_ref[0, 2:3, :] * m


def _s5_glu(x, y, modl, g, d_skip, w_bf, seg_len, tm):
    t, d = x.shape
    per_seg = seg_len // tm
    tok = pl.BlockSpec((tm, d), lambda i: (i, 0))
    vec = pl.BlockSpec((1, d), lambda i: (0, 0))
    return pl.pallas_call(
        _s5_glu_kernel,
        grid=(t // tm,),
        in_specs=[tok, tok, pl.BlockSpec((1, SUBLANES, d), lambda i: (i // per_seg, 0, 0)), vec, vec,
                  pl.BlockSpec((d, 2 * d), lambda i: (0, 0))],
        out_specs=tok,
        out_shape=jax.ShapeDtypeStruct((t, d), F32),
        compiler_params=_cparams("arbitrary"),
        name="s5_glu_residual",
    )(x, y, modl, g, d_skip, w_bf)


def _s5_layer(x, modl, g, params, seg_len, seq_first, seq_last, tm):
    a_re, a_im, log_dt, b_re, b_im, c_re, c_im, d_skip, w_glu = params
    t, d = x.shape
    L = SSM_CHUNK
    n_g = d // GROUP_CH
    nc = t // L
    w, pcat, qcat, mults = _s5_tables(a_re, a_im, log_dt, b_re, b_im, c_re, c_im)
    hb = _norm_bf16(x, modl, g, seg_len, tm)
    ug = hb.reshape(nc, L, n_g, GROUP_CH).transpose(2, 0, 1, 3).reshape(n_g, nc, L * GROUP_CH)
    tc = math.gcd(nc, SSM_ROW_TILE)
    sf, sb = _s5_states(ug, pcat, tc)
    chunks_per_seg = seg_len // L
    reset_f = jnp.repeat(jnp.asarray(seq_first, jnp.int32), chunks_per_seg) * (
        jnp.tile(jnp.arange(chunks_per_seg) == 0, len(seq_first))).astype(jnp.int32)
    reset_b = jnp.repeat(jnp.asarray(seq_last, jnp.int32), chunks_per_seg) * (
        jnp.tile(jnp.arange(chunks_per_seg) == chunks_per_seg - 1, len(seq_last))).astype(jnp.int32)
    xf, xb = _s5_scan(sf, sb, mults, reset_f, reset_b, math.gcd(nc, SSM_SCAN_TILE))
    yg = _s5_out(ug, xf, xb, w, qcat, tc)
    y = yg.reshape(n_g, nc, L, GROUP_CH).transpose(1, 2, 0, 3).reshape(t, d)
    return _s5_glu(x, y, modl, g, d_skip.reshape(1, d), w_glu.astype(BF16), seg_len, tm)


def _router_kernel(x_ref, mod_ref, g_ref, w_ref, b_ref, h_ref, idx_ref, gate_ref, rank_ref, cnt_ref,
                   carry_ref):
    tm = x_ref.shape[0]
    n_e = w_ref.shape[0]

    @pl.when(pl.program_id(0) == 0)
    def _():
        carry_ref[...] = jnp.zeros(carry_ref.shape, F32)

    h = _rms_mod(x_ref[...], g_ref[...], mod_ref[0, 4:5, :], mod_ref[0, 3:4, :])
    h_ref[...] = h
    logits = lax.dot_general(w_ref[...], h, (((1,), (1,)), ((), ())), precision=HIGHEST,
                             preferred_element_type=F32) + b_ref[...]
    e_iota = lax.broadcasted_iota(jnp.int32, (n_e, tm), 0)
    work = logits
    chosen = jnp.zeros((n_e, tm), F32)
    vals, idxs = [], []
    for _ in range(TOP_K):
        m = jnp.max(work, axis=0, keepdims=True)
        ix = jnp.min(jnp.where(work == m, e_iota, n_e), axis=0, keepdims=True)
        hit = e_iota == ix
        work = jnp.where(hit, -jnp.inf, work)
        chosen = jnp.where(hit, 1.0, chosen)
        vals.append(m)
        idxs.append(ix)
    v = jnp.concatenate(vals, axis=0)
    ex = jnp.exp(v - v[0:1])
    gate_ref[...] = ex / jnp.sum(ex, axis=0, keepdims=True)
    idx_ref[...] = jnp.concatenate(idxs, axis=0)
    row = lax.broadcasted_iota(jnp.int32, (tm, tm), 0)
    col = lax.broadcasted_iota(jnp.int32, (tm, tm), 1)
    before = (row < col).astype(BF16)
    cum = jnp.dot(chosen.astype(BF16), before, preferred_element_type=F32) + carry_ref[...]
    ranks = [jnp.sum(jnp.where(e_iota == ix, cum, 0.0), axis=0, keepdims=True) for ix in idxs]
    rank_ref[...] = jnp.concatenate(ranks, axis=0).astype(jnp.int32)
    carry_ref[...] = carry_ref[...] + jnp.sum(chosen, axis=1, keepdims=True)
    cnt_ref[...] = jnp.broadcast_to(carry_ref[...], cnt_ref.shape).astype(jnp.int32)


def _router(x, modl, g, w_router_t, b_router, seg_len, tm):
    t, d = x.shape
    n_e = w_router_t.shape[0]
    per_seg = seg_len // tm
    sel = pl.BlockSpec((TOP_K, tm), lambda i: (0, i))
    return pl.pallas_call(
        _router_kernel,
        grid=(t // tm,),
        in_specs=[pl.BlockSpec((tm, d), lambda i: (i, 0)),
                  pl.BlockSpec((1, SUBLANES, d), lambda i: (i // per_seg, 0, 0)),
                  pl.BlockSpec((1, d), lambda i: (0, 0)),
                  pl.BlockSpec((n_e, d), lambda i: (0, 0)),
                  pl.BlockSpec((n_e, 1), lambda i: (0, 0))],
        out_specs=[pl.BlockSpec((tm, d), lambda i: (i, 0)), sel, sel, sel,
                   pl.BlockSpec((n_e, LANES), lambda i: (0, 0))],
        out_shape=[jax.ShapeDtypeStruct((t, d), F32),
                   jax.ShapeDtypeStruct((TOP_K, t), jnp.int32),
                   jax.ShapeDtypeStruct((TOP_K, t), F32),
                   jax.ShapeDtypeStruct((TOP_K, t), jnp.int32),
                   jax.ShapeDtypeStruct((n_e, LANES), jnp.int32)],
        scratch_shapes=[pltpu.VMEM((n_e, 1), F32)],
        compiler_params=_cparams("arbitrary"),
        name="moe_router",
    )(x, modl, g, w_router_t, b_router)


def _dispatch_kernel(dest_ref, h_ref, init_hbm, xs_hbm, sem):
    del init_hbm
    tm = h_ref.shape[0]

    def issue(j, carry):
        for k in range(TOP_K):
            pltpu.make_async_copy(h_ref.at[pl.ds(j, 1), :],
                                  xs_hbm.at[pl.ds(dest_ref[0, 0, j * TOP_K + k], 1), :], sem).start()
        return carry

    lax.fori_loop(0, tm, issue, 0)
    for _ in range(TOP_K):
        pltpu.make_async_copy(h_ref, xs_hbm.at[pl.ds(0, tm), :], sem).wait()


def _dispatch(h, dest_tk, n_rows, tm):
    t, d = h.shape
    return pl.pallas_call(
        _dispatch_kernel,
        grid=(t // tm,),
        in_specs=[pl.BlockSpec((1, 1, tm * TOP_K), lambda i: (i, 0, 0), memory_space=pltpu.SMEM),
                  pl.BlockSpec((tm, d), lambda i: (i, 0)),
                  pl.BlockSpec(memory_space=pl.ANY)],
        out_specs=pl.BlockSpec(memory_space=pl.ANY),
        out_shape=jax.ShapeDtypeStruct((n_rows, d), F32),
        scratch_shapes=[pltpu.SemaphoreType.DMA(())],
        input_output_aliases={2: 0},
        compiler_params=_cparams("arbitrary"),
        name="moe_dispatch",
    )(dest_tk.reshape(t // tm, 1, tm * TOP_K), h, jnp.zeros((n_rows, d), F32))


def _expert_kernel(be_ref, bi_ref, nu_ref, x_ref, wgu_ref, bgu_ref, wd_ref, bd_ref, y_ref):
    del be_ref, bi_ref
    n_ff = wd_ref.shape[1]

    @pl.when(pl.program_id(0) < nu_ref[0])
    def _():
        x = x_ref[...].astype(BF16)
        acc = jnp.zeros(y_ref.shape, F32) + bd_ref[0]
        for c in range(n_ff // FF_CHUNK):
            lo = c * FF_CHUNK
            g_part = (jnp.dot(x, wgu_ref[0, :, lo:lo + FF_CHUNK], preferred_element_type=F32)
                      + bgu_ref[0, :, lo:lo + FF_CHUNK])
            u_part = (jnp.dot(x, wgu_ref[0, :, n_ff + lo:n_ff + lo + FF_CHUNK], preferred_element_type=F32)
                      + bgu_ref[0, :, n_ff + lo:n_ff + lo + FF_CHUNK])
            g_part = jnp.minimum(g_part, SWIGLU_LIMIT)
            u_part = jnp.clip(u_part, -SWIGLU_LIMIT, SWIGLU_LIMIT)
            act = (u_part + 1.0) * g_part * jax.nn.sigmoid(SWIGLU_ALPHA * g_part)
            acc = acc + jnp.dot(act.astype(BF16), wd_ref[0, lo:lo + FF_CHUNK, :],
                                preferred_element_type=F32)
        y_ref[...] = acc


def _experts(xs, blk_expert, blk_idx, n_used, w_gu_bf, b_gu, w_down_bf, b_down, tme):
    nr, d = xs.shape
    n_e, _, two_f = w_gu_bf.shape
    n_ff = two_f // 2
    nblk = nr // tme
    rows = pl.BlockSpec((tme, d), lambda b, be, bi, nu: (bi[b], 0))
    return pl.pallas_call(
        _expert_kernel,
        grid_spec=pltpu.PrefetchScalarGridSpec(
            num_scalar_prefetch=3, grid=(nblk,),
            in_specs=[rows,
                      pl.BlockSpec((1, d, two_f), lambda b, be, bi, nu: (be[b], 0, 0)),
                      pl.BlockSpec((1, 1, two_f), lambda b, be, bi, nu: (be[b], 0, 0)),
                      pl.BlockSpec((1, n_ff, d), lambda b, be, bi, nu: (be[b], 0, 0)),
                      pl.BlockSpec((1, 1, d), lambda b, be, bi, nu: (be[b], 0, 0))],
            out_specs=rows),
        out_shape=jax.ShapeDtypeStruct(xs.shape, F32),
        compiler_params=_cparams("arbitrary"),
        name="moe_experts",
    )(blk_expert, blk_idx, n_used, xs, w_gu_bf, b_gu.reshape(n_e, 1, two_f), w_down_bf,
      b_down.reshape(n_e, 1, d))


def _combine_kernel(dest_ref, x_ref, gate_ref, mod_ref, g_ref, ys_hbm, o_ref, buf, sem, *, final_norm):
    tm = x_ref.shape[0]

    def issue(j, carry):
        for k in range(TOP_K):
            pltpu.make_async_copy(ys_hbm.at[pl.ds(dest_ref[0, 0, j * TOP_K + k], 1), :],
                                  buf.at[k, pl.ds(j, 1), :], sem).start()
        return carry

    lax.fori_loop(0, tm, issue, 0)
    for k in range(TOP_K):
        pltpu.make_async_copy(ys_hbm.at[pl.ds(0, tm), :], buf.at[k], sem).wait()
    gates = gate_ref[...]
    moe = gates[:, 0:1] * buf[0]
    for k in range(1, TOP_K):
        moe = moe + gates[:, k:k + 1] * buf[k]
    x = x_ref[...] + mod_ref[0, 5:6, :] * moe
    if final_norm:
        ms = jnp.mean(x * x, axis=-1, keepdims=True)
        x = x * lax.rsqrt(ms + NORM_EPS) * g_ref[...]
    o_ref[...] = x


def _combine(x, ys, dest_tk, gates, modl, final_g, seg_len, tm, final_norm, row0=0, rows=None):
    t, d = x.shape
    rows = t if rows is None else rows
    per_seg = seg_len // tm
    b0 = row0 // tm
    return pl.pallas_call(
        functools.partial(_combine_kernel, final_norm=final_norm),
        grid=(rows // tm,),
        in_specs=[pl.BlockSpec((1, 1, tm * TOP_K), lambda i: (i + b0, 0, 0), memory_space=pltpu.SMEM),
                  pl.BlockSpec((tm, d), lambda i: (i + b0, 0)),
                  pl.BlockSpec((tm, TOP_K), lambda i: (i + b0, 0)),
                  pl.BlockSpec((1, SUBLANES, d), lambda i: ((i + b0) // per_seg, 0, 0)),
                  pl.BlockSpec((1, d), lambda i: (0, 0)),
                  pl.BlockSpec(memory_space=pl.ANY)],
        out_specs=pl.BlockSpec((tm, d), lambda i: (i, 0)),
        out_shape=jax.ShapeDtypeStruct((rows, d), F32),
        scratch_shapes=[pltpu.VMEM((TOP_K, tm, d), F32), pltpu.SemaphoreType.DMA(())],
        compiler_params=_cparams("arbitrary"),
        name="moe_combine",
    )(dest_tk.reshape(t // tm, 1, tm * TOP_K), x, gates, modl, final_g, ys)


def _moe_dispatch_and_experts(x, modl, g, w_router, b_router, w_gu, b_gu, w_down, b_down, seg_len, tm):
    t, d = x.shape
    n_e = w_router.shape[1]
    tme = math.gcd(t * TOP_K, EXPERT_ROW_TILE)
    h, idx_t, gate_t, rank_t, cnt = _router(x, modl, g, w_router.T, b_router.reshape(n_e, 1), seg_len, tm)
    counts = cnt[:, 0]
    padded = (counts + tme - 1) // tme * tme
    pad_end = jnp.cumsum(padded)
    pad_start = pad_end - padded
    nblk = (t * TOP_K) // tme + n_e
    n_used = (pad_end[-1] // tme).astype(jnp.int32).reshape(1)
    blk_idx = jnp.minimum(jnp.arange(nblk, dtype=jnp.int32), n_used[0] - 1)
    blk_expert = jnp.minimum(jnp.searchsorted(pad_end, blk_idx * tme, side='right'), n_e - 1).astype(jnp.int32)
    start_of = jnp.sum(jnp.where(idx_t[None] == jnp.arange(n_e, dtype=jnp.int32)[:, None, None],
                                 pad_start[:, None, None], 0), axis=0)
    dest = (start_of + rank_t).astype(jnp.int32)
    dest_tk = dest.T.reshape(-1)
    xs = _dispatch(h, dest_tk, nblk * tme, tm)
    ys = _experts(xs, blk_expert, blk_idx, n_used, w_gu.astype(BF16), b_gu, w_down.astype(BF16), b_down, tme)
    return ys, dest_tk, gate_t.T


def kernel(x_prompt, x_sample, c_prompt, c_sample, norm1_g, norm2_g, final_g, w_ada, b_ada, w_qkv, w_o,
           lam_q1, lam_k1, lam_q2, lam_k2, subln_g, ssm_a_re, ssm_a_im, ssm_log_dt, ssm_b_re, ssm_b_im,
           ssm_c_re, ssm_c_im, ssm_d, ssm_w_glu, w_router, b_router, w_gu, b_gu, w_down, b_down):
    n_b, seg_len, d = x_prompt.shape
    n_bs, s_len, _ = x_sample.shape
    depth = w_ada.shape[0]
    assert s_len % seg_len == 0 and s_len // seg_len in (1, 2)
    per_sample = s_len // seg_len
    n_seg = n_b + n_bs * per_sample
    t = n_seg * seg_len
    tm = math.gcd(seg_len, TOKEN_TILE)

    seg_seq = list(range(n_b)) + [n_b + j for j in range(n_bs) for _ in range(per_sample)]
    seg_pos = [0] * n_b + [r for _ in range(n_bs) for r in range(per_sample)]
    seg_a = list(range(n_b)) + [n_b + j * per_sample for j in range(n_bs) for _ in range(per_sample)]
    seg_b = list(range(n_b)) + [n_b + j * per_sample + per_sample - 1 for j in range(n_bs) for _ in range(per_sample)]
    two = [0] * n_b + [int(per_sample == 2)] * (n_bs * per_sample)
    seq_first = [1] * n_b + [int(r == 0) for _ in range(n_bs) for r in range(per_sample)]
    seq_last = [1] * n_b + [int(r == per_sample - 1) for _ in range(n_bs) for r in range(per_sample)]
    per_seg = seg_len // tm
    pos_blk = jnp.asarray([seg_pos[s] * per_seg + r for s in range(n_seg) for r in range(per_seg)], jnp.int32)

    x = jnp.concatenate([x_prompt.reshape(n_b * seg_len, d), x_sample.reshape(n_bs * s_len, d)], axis=0)

    n_c = n_b + n_bs
    c_rows = -(-n_c // SUBLANES) * SUBLANES
    c_all = jnp.concatenate([c_prompt, c_sample, jnp.zeros((c_rows - n_c, d), F32)], axis=0)
    mod = _ada(c_all, w_ada, b_ada)
    mod = mod[:, jnp.asarray(seg_seq)].reshape(depth, n_seg, 6, d)
    mod = jnp.concatenate([mod, jnp.zeros((depth, n_seg, SUBLANES - 6, d), F32)], axis=2)

    inv_freq = ROPE_THETA ** (-jnp.arange(0, HEAD_DIM, 2, dtype=F32) / HEAD_DIM)
    ang = jnp.arange(s_len, dtype=F32)[:, None] * inv_freq[None, :]
    cos_t = jnp.tile(jnp.cos(ang), (1, 2 * LANES // HEAD_DIM))
    sin_t = jnp.tile(jnp.concatenate([-jnp.sin(ang), jnp.sin(ang)], axis=1), (1, LANES // HEAD_DIM))

    out_p = out_s = None
    for i in range(depth):
        modl = mod[i]
        g1 = norm1_g[i].reshape(1, d)
        g2 = norm2_g[i].reshape(1, d)
        j = i // 2
        if i % 2 == 0:
            lambda_init = 0.8 - 0.6 * math.exp(-0.3 * i)
            q, k, vt = _qkv(x, modl, g1, w_qkv[j], cos_t, sin_t, pos_blk, seg_len, tm)
            lam_pack = jnp.zeros((SUBLANES, 2 * HEAD_DIM), F32).at[0:4, 0:HEAD_DIM].set(
                jnp.stack([lam_q1[j], lam_k1[j], lam_q2[j], lam_k2[j]]).astype(F32))
            o = _flash(q, k, vt, lam_pack, subln_g[j].reshape(2 * HEAD_DIM, 1).astype(F32),
                       jnp.asarray(seg_a, jnp.int32), jnp.asarray(seg_b, jnp.int32),
                       jnp.asarray(two, jnp.int32), seg_len, lambda_init)
            x = _proj_res(o, w_o[j].astype(BF16), x, modl, seg_len, tm, gate_row=2)
        else:
            params = (ssm_a_re[j], ssm_a_im[j], ssm_log_dt[j], ssm_b_re[j], ssm_b_im[j], ssm_c_re[j],
                      ssm_c_im[j], ssm_d[j], ssm_w_glu[j])
            x = _s5_layer(x, modl, g1, params, seg_len, seq_first, seq_last, tm)
        ys, dest_tk, gates = _moe_dispatch_and_experts(x, modl, g2, w_router[i], b_router[i], w_gu[i], b_gu[i],
                                                       w_down[i], b_down[i], seg_len, tm)
        fg = final_g.reshape(1, d)
        if i == depth - 1:
            out_p = _combine(x, ys, dest_tk, gates, modl, fg, seg_len, tm, True, 0, n_b * seg_len)
            out_s = _combine(x, ys, dest_tk, gates, modl, fg, seg_len, tm, True, n_b * seg_len, n_bs * s_len)
        else:
            x = _combine(x, ys, dest_tk, gates, modl, fg, seg_len, tm, False)
    return out_p.reshape(n_b, seg_len, d), out_s.reshape(n_bs, s_len, d)
```

```python
import functools
import math

import jax
import jax.numpy as jnp
from jax import lax
from jax.experimental import pallas as pl
from jax.experimental.pallas import tpu as pltpu

F32 = jnp.float32
BF16 = jnp.bfloat16
HIGHEST = lax.Precision.HIGHEST

N_DIFF_HEADS = 8
HEAD_DIM = 64
ROPE_THETA = 10000.0
SUBLN_EPS = 1e-5
NORM_EPS = 1e-6
GROUP_CH = 16
STATE_DIM = 64
N_EXPERTS = 32
TOP_K = 4
SWIGLU_ALPHA = 1.702
SWIGLU_LIMIT = 7.0

LANES = 128
SUBLANES = 8
VMEM_LIMIT_BYTES = 56 * 1024 * 1024

TOKEN_TILE = 512
ATT_Q_TILE = 512
ATT_UNROLL = 4
VT_ONES_ROWS = 16
LOG2_E = math.log2(math.e)
SSM_CHUNK = 16
SSM_GROUP_TILE = 8
SSM_ROW_TILE = 512
SSM_SCAN_TILE = 64
EXPERT_ROW_TILE = 512
FF_CHUNK = 512


def _cparams(*sem):
    return pltpu.CompilerParams(dimension_semantics=sem, vmem_limit_bytes=VMEM_LIMIT_BYTES)


def _rms_mod(x, g, scale, shift):
    ms = jnp.mean(x * x, axis=-1, keepdims=True)
    return x * lax.rsqrt(ms + NORM_EPS) * g * (1.0 + scale) + shift


def _ada_kernel(c_ref, w_ref, b_ref, o_ref):
    c = c_ref[...]
    cond = c * jax.nn.sigmoid(c)
    o_ref[0] = jnp.dot(cond, w_ref[0], precision=HIGHEST, preferred_element_type=F32) + b_ref[0]


def _ada(c_pad, w_ada, b_ada):
    depth, d, n6 = w_ada.shape
    rows = c_pad.shape[0]
    tn = 1536 if n6 % 1536 == 0 else n6
    return pl.pallas_call(
        _ada_kernel,
        grid=(depth, n6 // tn),
        in_specs=[pl.BlockSpec((rows, d), lambda i, j: (0, 0)),
                  pl.BlockSpec((1, d, tn), lambda i, j: (i, 0, j)),
                  pl.BlockSpec((1, 1, tn), lambda i, j: (i, 0, j))],
        out_specs=pl.BlockSpec((1, rows, tn), lambda i, j: (i, 0, j)),
        out_shape=jax.ShapeDtypeStruct((depth, rows, n6), F32),
        compiler_params=_cparams("arbitrary", "arbitrary"),
        name="ada_mod",
    )(c_pad, w_ada, b_ada.reshape(depth, 1, n6))


def _qkv_kernel(pos_ref, x_ref, mod_ref, g_ref, w_ref, wvt_ref, cos_ref, sin_ref, q_ref, k_ref, vt_ref):
    del pos_ref
    d = x_ref.shape[1]
    h = _rms_mod(x_ref[...], g_ref[...], mod_ref[0, 1:2, :], mod_ref[0, 0:1, :]).astype(BF16)
    qkv = jnp.dot(h, w_ref[...], preferred_element_type=F32)
    vt = lax.dot_general(wvt_ref[...], h, (((1,), (1,)), ((), ())), preferred_element_type=F32)
    ones = jnp.ones((VT_ONES_ROWS, vt.shape[1]), BF16)
    for hd in range(vt_ref.shape[0]):
        vt_ref[hd, 0, 0:LANES, :] = vt[hd * LANES:(hd + 1) * LANES, :].astype(BF16)
        vt_ref[hd, 0, LANES:LANES + VT_ONES_ROWS, :] = ones
    cos = cos_ref[...]
    sin = sin_ref[...]
    lane = lax.broadcasted_iota(jnp.int32, cos.shape, 1)
    first_half = (lane % HEAD_DIM) < (HEAD_DIM // 2)

    def rotary(t):
        partner = jnp.where(first_half,
                            pltpu.roll(t, LANES - HEAD_DIM // 2, 1),
                            pltpu.roll(t, HEAD_DIM // 2, 1))
        return t * cos + partner * sin

    for j in range(d // LANES):
        sl = slice(j * LANES, (j + 1) * LANES)
        q_ref[:, sl] = (rotary(qkv[:, sl]) * (HEAD_DIM ** -0.5 * LOG2_E)).astype(BF16)
        k_ref[:, sl] = rotary(qkv[:, d + j * LANES:d + (j + 1) * LANES]).astype(BF16)


def _qkv(x, modl, g, w_qkv, cos_t, sin_t, pos_blk, seg_len, tm):
    t, d = x.shape
    per_seg = seg_len // tm
    n_vh = d // LANES
    tok = lambda i, p: (i, 0)
    out = jax.ShapeDtypeStruct((t, d), BF16)
    w_qk = w_qkv[:, :2 * d].astype(BF16)
    w_vt = w_qkv[:, 2 * d:].T.astype(BF16)
    return pl.pallas_call(
        _qkv_kernel,
        grid_spec=pltpu.PrefetchScalarGridSpec(
            num_scalar_prefetch=1, grid=(t // tm,),
            in_specs=[pl.BlockSpec((tm, d), tok),
                      pl.BlockSpec((1, SUBLANES, d), lambda i, p: (i // per_seg, 0, 0)),
                      pl.BlockSpec((1, d), lambda i, p: (0, 0)),
                      pl.BlockSpec((d, 2 * d), lambda i, p: (0, 0)),
                      pl.BlockSpec((d, d), lambda i, p: (0, 0)),
                      pl.BlockSpec((tm, LANES), lambda i, p: (p[i], 0)),
                      pl.BlockSpec((tm, LANES), lambda i, p: (p[i], 0))],
            out_specs=[pl.BlockSpec((tm, d), tok), pl.BlockSpec((tm, d), tok),
                       pl.BlockSpec((n_vh, 1, LANES + VT_ONES_ROWS, tm), lambda i, p: (0, i, 0, 0))]),
        out_shape=[out, out, jax.ShapeDtypeStruct((n_vh, t // tm, LANES + VT_ONES_ROWS, tm), BF16)],
        compiler_params=_cparams("arbitrary"),
        name="qkv_rotary",
    )(pos_blk, x, modl, g, w_qk, w_vt, cos_t, sin_t)


def _flash_kernel(sa_ref, sb_ref, two_ref, q_ref, ka_ref, vta_ref, kb_ref, vtb_ref, lam_ref, g_ref,
                  o_ref, qt_ref, m_ref, acc_ref, *, tk, unroll, lambda_init):
    del sa_ref, sb_ref
    seg = pl.program_id(0)
    qt = q_ref[...].astype(F32).T
    chan = lax.broadcasted_iota(jnp.int32, qt.shape, 0)
    qt_ref[0] = jnp.where(chan < HEAD_DIM, qt, 0.0).astype(BF16)
    qt_ref[1] = jnp.where(chan >= HEAD_DIM, qt, 0.0).astype(BF16)
    m_ref[...] = jnp.full(m_ref.shape, -jnp.inf, F32)
    acc_ref[...] = jnp.zeros(acc_ref.shape, F32)

    def sweep(k_ref, vt_ref):
        units = [(tile, g) for tile in range(unroll) for g in range(2)]

        def scores(jj, u):
            tile, g = units[u]
            off = pl.multiple_of((jj * unroll + tile) * tk, tk)
            half = tk // 2
            return jnp.concatenate(
                [jnp.dot(k_ref[pl.ds(off + r * half, half), :], qt_ref[g], preferred_element_type=F32)
                 for r in range(2)], axis=0)

        def softmax(u, s):
            g = units[u][1]
            m_prev = m_ref[g]
            m_new = jnp.maximum(m_prev, jnp.max(s, axis=0, keepdims=True))
            p = jnp.exp2((s - m_new).astype(BF16))
            alpha = jnp.exp2(m_prev - m_new)
            m_ref[g] = m_new
            return p, alpha

        def values(jj, u, p, alpha):
            tile, g = units[u]
            acc_ref[g] = alpha * acc_ref[g] + jnp.dot(vt_ref[0, jj * unroll + tile], p,
                                                      preferred_element_type=F32)

        def body(jj, carry):
            s_next = scores(jj, 0)
            pending = None
            for u in range(len(units)):
                s_cur = s_next
                if u + 1 < len(units):
                    s_next = scores(jj, u + 1)
                if pending is not None:
                    values(jj, *pending)
                pending = (u,) + softmax(u, s_cur)
            values(jj, *pending)
            return carry
        lax.fori_loop(0, k_ref.shape[0] // (tk * unroll), body, 0)

    sweep(ka_ref, vta_ref)

    @pl.when(two_ref[seg] == 1)
    def _():
        sweep(kb_ref, vtb_ref)

    o1 = acc_ref[0, 0:LANES, :] / acc_ref[0, LANES:LANES + 1, :]
    o2 = acc_ref[1, 0:LANES, :] / acc_ref[1, LANES:LANES + 1, :]
    lam = (jnp.exp(jnp.sum(lam_ref[0:1, :] * lam_ref[1:2, :], axis=-1, keepdims=True))
           - jnp.exp(jnp.sum(lam_ref[2:3, :] * lam_ref[3:4, :], axis=-1, keepdims=True))
           + lambda_init)
    ot = o1 - lam * o2
    ms = jnp.mean(ot * ot, axis=0, keepdims=True)
    ot = ot * lax.rsqrt(ms + SUBLN_EPS) * g_ref[...] * (1.0 - lambda_init)
    o_ref[...] = ot.T.astype(BF16)


def _flash(q, k, vt, lam_pack, subln_g_col, seg_a, seg_b, two, seg_len, lambda_init):
    t, d = q.shape
    nseg = t // seg_len
    n_vh, _, v_rows, tk = vt.shape
    width = 2 * HEAD_DIM
    per_seg = seg_len // tk
    tq = min(ATT_Q_TILE, seg_len)
    nq = seg_len // tq
    kern = functools.partial(_flash_kernel, tk=tk, unroll=min(ATT_UNROLL, per_seg), lambda_init=lambda_init)
    k_a = pl.BlockSpec((seg_len, width), lambda s, h, i, sa, sb, tw: (sa[s], h))
    k_b = pl.BlockSpec((seg_len, width), lambda s, h, i, sa, sb, tw: (sb[s], h))
    vt_a = pl.BlockSpec((1, per_seg, v_rows, tk), lambda s, h, i, sa, sb, tw: (h, sa[s], 0, 0))
    vt_b = pl.BlockSpec((1, per_seg, v_rows, tk), lambda s, h, i, sa, sb, tw: (h, sb[s], 0, 0))
    qo = pl.BlockSpec((tq, width), lambda s, h, i, sa, sb, tw: (s * nq + i, h))
    return pl.pallas_call(
        kern,
        grid_spec=pltpu.PrefetchScalarGridSpec(
            num_scalar_prefetch=3, grid=(nseg, n_vh, nq),
            in_specs=[qo, k_a, vt_a, k_b, vt_b,
                      pl.BlockSpec((SUBLANES, width), lambda s, h, i, sa, sb, tw: (0, 0)),
                      pl.BlockSpec((width, 1), lambda s, h, i, sa, sb, tw: (0, 0))],
            out_specs=qo,
            scratch_shapes=[pltpu.VMEM((2, width, tq), BF16),
                            pltpu.VMEM((2, 1, tq), F32),
                            pltpu.VMEM((2, v_rows, tq), F32)]),
        out_shape=jax.ShapeDtypeStruct((t, d), BF16),
        compiler_params=_cparams("arbitrary", "arbitrary", "arbitrary"),
        name="diff_flash_attention",
    )(seg_a, seg_b, two, q, k, vt, k, vt, lam_pack, subln_g_col)


def _proj_res_kernel(a_ref, w_ref, x_ref, mod_ref, o_ref, *, gate_row):
    y = jnp.dot(a_ref[...], w_ref[...], preferred_element_type=F32)
    o_ref[...] = x_ref[...] + mod_ref[0, gate_row:gate_row + 1, :] * y


def _proj_res(a, w_bf, x, modl, seg_len, tm, gate_row):
    t, d = x.shape
    per_seg = seg_len // tm
    return pl.pallas_call(
        functools.partial(_proj_res_kernel, gate_row=gate_row),
        grid=(t // tm,),
        in_specs=[pl.BlockSpec((tm, a.shape[1]), lambda i: (i, 0)),
                  pl.BlockSpec(w_bf.shape, lambda i: (0, 0)),
                  pl.BlockSpec((tm, d), lambda i: (i, 0)),
                  pl.BlockSpec((1, SUBLANES, d), lambda i: (i // per_seg, 0, 0))],
        out_specs=pl.BlockSpec((tm, d), lambda i: (i, 0)),
        out_shape=jax.ShapeDtypeStruct((t, d), F32),
        compiler_params=_cparams("arbitrary"),
        name="proj_residual",
    )(a, w_bf, x, modl)


def _norm_bf16_kernel(x_ref, mod_ref, g_ref, o_ref):
    o_ref[...] = _rms_mod(x_ref[...], g_ref[...], mod_ref[0, 1:2, :], mod_ref[0, 0:1, :]).astype(BF16)


def _norm_bf16(x, modl, g, seg_len, tm):
    t, d = x.shape
    per_seg = seg_len // tm
    return pl.pallas_call(
        _norm_bf16_kernel,
        grid=(t // tm,),
        in_specs=[pl.BlockSpec((tm, d), lambda i: (i, 0)),
                  pl.BlockSpec((1, SUBLANES, d), lambda i: (i // per_seg, 0, 0)),
                  pl.BlockSpec((1, d), lambda i: (0, 0))],
        out_specs=pl.BlockSpec((tm, d), lambda i: (i, 0)),
        out_shape=jax.ShapeDtypeStruct((t, d), BF16),
        compiler_params=_cparams("arbitrary"),
        name="s5_prenorm",
    )(x, modl, g)


def _s5_tables(a_re, a_im, log_dt, b_re, b_im, c_re, c_im):
    f32 = F32
    L = SSM_CHUNK
    ar, ai = a_re.astype(f32), a_im.astype(f32)
    n_g, n_p = ar.shape[1], ar.shape[2]
    n_h = b_re.shape[-1]
    dt = jnp.exp(log_dt.astype(f32))[..., None]
    mag = jnp.exp(ar * dt)
    lr, li = mag * jnp.cos(ai * dt), mag * jnp.sin(ai * dt)
    den = ar * ar + ai * ai
    fr = ((lr - 1.0) * ar + li * ai) / den
    fi = (li * ar - (lr - 1.0) * ai) / den
    br, bi = b_re.astype(f32), b_im.astype(f32)
    bbr = fr[..., None] * br - fi[..., None] * bi
    bbi = fr[..., None] * bi + fi[..., None] * br
    cr, ci = c_re.astype(f32), c_im.astype(f32)
    kk = jnp.arange(L + 1, dtype=f32)
    pmag = jnp.exp(ar[..., None] * dt[..., None] * kk)
    pang = ai[..., None] * dt[..., None] * kk
    pr, pi = pmag * jnp.cos(pang), pmag * jnp.sin(pang)
    er = cr[..., None] * pr[:, :, None] - ci[..., None] * pi[:, :, None]
    ei = cr[..., None] * pi[:, :, None] + ci[..., None] * pr[:, :, None]
    klag = (jnp.einsum('dgopk,dgpi->dgkoi', er[..., :L], bbr, precision=HIGHEST)
            - jnp.einsum('dgopk,dgpi->dgkoi', ei[..., :L], bbi, precision=HIGHEST))
    kf, kb = klag[0], klag[1]
    kfull = jnp.concatenate([kb[:, 1:][:, ::-1], (kf[:, 0] + kb[:, 0])[:, None], kf[:, 1:]], axis=1)
    s_idx = jnp.arange(L)[:, None]
    t_idx = jnp.arange(L)[None, :]
    w = kfull[:, t_idx - s_idx + L - 1]
    w = jnp.transpose(w, (0, 1, 4, 2, 3)).reshape(n_g, L * n_h, L * n_h)
    def in_to_state(d, pw_idx):
        pwr = jnp.take(pr[d], pw_idx, axis=-1)
        pwi = jnp.take(pi[d], pw_idx, axis=-1)
        re = pwr[..., None] * bbr[d][:, :, None, :] - pwi[..., None] * bbi[d][:, :, None, :]
        im = pwr[..., None] * bbi[d][:, :, None, :] + pwi[..., None] * bbr[d][:, :, None, :]
        m = jnp.concatenate([re, im], axis=1)
        return jnp.transpose(m, (0, 2, 3, 1)).reshape(n_g, L * n_h, 2 * n_p)
    pcat = jnp.concatenate([in_to_state(0, L - 1 - jnp.arange(L)), in_to_state(1, jnp.arange(L))], axis=-1)
    def state_to_out(d, pw_idx):
        e_r = jnp.take(er[d], pw_idx, axis=-1)
        e_i = jnp.take(ei[d], pw_idx, axis=-1)
        m = jnp.concatenate([e_r, -e_i], axis=2)
        return jnp.transpose(m, (0, 2, 3, 1)).reshape(n_g, 2 * n_p, L * n_h)
    qcat = jnp.concatenate([state_to_out(0, jnp.arange(L) + 1), state_to_out(1, L - jnp.arange(L))], axis=1)

    def scan_mult(d):
        a_r, a_i = pr[d][..., L], pi[d][..., L]
        m1 = jnp.concatenate([a_r, a_r], axis=-1)
        m2 = jnp.concatenate([-a_i, a_i], axis=-1)
        lay = lambda m: jnp.transpose(m.reshape(n_g // SSM_GROUP_TILE, SSM_GROUP_TILE, 2 * n_p),
                                      (1, 0, 2)).reshape(SSM_GROUP_TILE, -1)
        return lay(m1), lay(m2)
    af1, af2 = scan_mult(0)
    ab1, ab2 = scan_mult(1)
    return w.astype(BF16), pcat.astype(BF16), qcat.astype(BF16), (af1, af2, ab1, ab2)


def _s5_states_kernel(u_ref, p_ref, sf_ref, sb_ref):
    half = sf_ref.shape[-1]
    for r in range(u_ref.shape[0]):
        s = jnp.dot(u_ref[r], p_ref[r], preferred_element_type=F32)
        sf_ref[:, r, :] = s[:, :half]
        sb_ref[:, r, :] = s[:, half:]


def _s5_states(ug, pcat, tc):
    n_g, nc, kdim = ug.shape
    gt = SSM_GROUP_TILE
    half = pcat.shape[-1] // 2
    out = jax.ShapeDtypeStruct((nc, gt, (n_g // gt) * half), F32)
    return pl.pallas_call(
        _s5_states_kernel,
        grid=(n_g // gt, nc // tc),
        in_specs=[pl.BlockSpec((gt, tc, kdim), lambda j, c: (j, c, 0)),
                  pl.BlockSpec((gt, kdim, 2 * half), lambda j, c: (j, 0, 0))],
        out_specs=[pl.BlockSpec((tc, gt, half), lambda j, c: (c, 0, j))] * 2,
        out_shape=[out, out],
        compiler_params=_cparams("arbitrary", "arbitrary"),
        name="s5_chunk_states",
    )(ug, pcat)


def _s5_scan_kernel(rf_ref, rb_ref, sf_ref, sb_ref, af1_ref, af2_ref, ab1_ref, ab2_ref,
                    xf_ref, xb_ref, cf_ref, cb_ref, *, tcs, nblk):
    i = pl.program_id(0)

    @pl.when(i == 0)
    def _():
        cf_ref[...] = jnp.zeros(cf_ref.shape, F32)
        cb_ref[...] = jnp.zeros(cb_ref.shape, F32)

    width = cf_ref.shape[-1]

    def swap(x):
        return jnp.concatenate(
            [pltpu.roll(x[:, j * LANES:(j + 1) * LANES], LANES // 2, 1) for j in range(width // LANES)],
            axis=1)

    af1, af2 = af1_ref[...], af2_ref[...]
    ab1, ab2 = ab1_ref[...], ab2_ref[...]

    def fwd(r, carry):
        carry = carry * (1 - rf_ref[i * tcs + r]).astype(F32)
        xf_ref[r] = carry
        return af1 * carry + af2 * swap(carry) + sf_ref[r]

    cf_ref[...] = lax.fori_loop(0, tcs, fwd, cf_ref[...])

    def bwd(rr, carry):
        r = tcs - 1 - rr
        carry = carry * (1 - rb_ref[(nblk - 1 - i) * tcs + r]).astype(F32)
        xb_ref[r] = carry
        return ab1 * carry + ab2 * swap(carry) + sb_ref[r]

    cb_ref[...] = lax.fori_loop(0, tcs, bwd, cb_ref[...])


def _s5_scan(sf, sb, mults, reset_f, reset_b, tcs):
    nc, gt, width = sf.shape
    nblk = nc // tcs
    fblk = pl.BlockSpec((tcs, gt, width), lambda i, a, b: (i, 0, 0))
    bblk = pl.BlockSpec((tcs, gt, width), lambda i, a, b: (nblk - 1 - i, 0, 0))
    tab = pl.BlockSpec((gt, width), lambda i, a, b: (0, 0))
    out = jax.ShapeDtypeStruct((nc, gt, width), F32)
    return pl.pallas_call(
        functools.partial(_s5_scan_kernel, tcs=tcs, nblk=nblk),
        grid_spec=pltpu.PrefetchScalarGridSpec(
            num_scalar_prefetch=2, grid=(nblk,),
            in_specs=[fblk, bblk, tab, tab, tab, tab],
            out_specs=[fblk, bblk],
            scratch_shapes=[pltpu.VMEM((gt, width), F32), pltpu.VMEM((gt, width), F32)]),
        out_shape=[out, out],
        compiler_params=_cparams("arbitrary"),
        name="s5_chunk_scan",
    )(reset_f, reset_b, sf, sb, *mults)


def _s5_out_kernel(u_ref, xf_ref, xb_ref, w_ref, q_ref, y_ref):
    for r in range(u_ref.shape[0]):
        xcat = jnp.concatenate([xf_ref[:, r, :], xb_ref[:, r, :]], axis=-1).astype(BF16)
        y_ref[r] = (jnp.dot(u_ref[r], w_ref[r], preferred_element_type=F32)
                    + jnp.dot(xcat, q_ref[r], preferred_element_type=F32))


def _s5_out(ug, xf, xb, w, qcat, tc):
    n_g, nc, kdim = ug.shape
    gt = SSM_GROUP_TILE
    half = xf.shape[-1] // (n_g // gt)
    xblk = pl.BlockSpec((tc, gt, half), lambda j, c: (c, 0, j))
    return pl.pallas_call(
        _s5_out_kernel,
        grid=(n_g // gt, nc // tc),
        in_specs=[pl.BlockSpec((gt, tc, kdim), lambda j, c: (j, c, 0)), xblk, xblk,
                  pl.BlockSpec((gt, kdim, kdim), lambda j, c: (j, 0, 0)),
                  pl.BlockSpec((gt, 2 * half, kdim), lambda j, c: (j, 0, 0))],
        out_specs=pl.BlockSpec((gt, tc, kdim), lambda j, c: (j, c, 0)),
        out_shape=jax.ShapeDtypeStruct((n_g, nc, kdim), F32),
        compiler_params=_cparams("arbitrary", "arbitrary"),
        name="s5_chunk_outputs",
    )(ug, xf, xb, w, qcat)


def _s5_glu_kernel(x_ref, y_ref, mod_ref, g_ref, dskip_ref, w_ref, o_ref):
    d = x_ref.shape[1]
    x = x_ref[...]
    h = _rms_mod(x, g_ref[...], mod_ref[0, 1:2, :], mod_ref[0, 0:1, :])
    z = jax.nn.gelu(y_ref[...] + dskip_ref[...] * h)
    vg = jnp.dot(z.astype(BF16), w_ref[...], preferred_element_type=F32)
    m = vg[:, :d] * jax.nn.sigmoid(vg[:, d:])
    o_ref[...] = x + mod_ref[0, 2:3, :] * m


def _s5_glu(x, y, modl, g, d_skip, w_bf, seg_len, tm):
    t, d = x.shape
    per_seg = seg_len // tm
    tok = pl.BlockSpec((tm, d), lambda i: (i, 0))
    vec = pl.BlockSpec((1, d), lambda i: (0, 0))
    return pl.pallas_call(
        _s5_glu_kernel,
        grid=(t // tm,),
        in_specs=[tok, tok, pl.BlockSpec((1, SUBLANES, d), lambda i: (i // per_seg, 0, 0)), vec, vec,
                  pl.BlockSpec((d, 2 * d), lambda i: (0, 0))],
        out_specs=tok,
        out_shape=jax.ShapeDtypeStruct((t, d), F32),
        compiler_params=_cparams("arbitrary"),
        name="s5_glu_residual",
    )(x, y, modl, g, d_skip, w_bf)


def _s5_layer(x, modl, g, params, seg_len, seq_first, seq_last, tm):
    a_re, a_im, log_dt, b_re, b_im, c_re, c_im, d_skip, w_glu = params
    t, d = x.shape
    L = SSM_CHUNK
    n_g = d // GROUP_CH
    nc = t // L
    w, pcat, qcat, mults = _s5_tables(a_re, a_im, log_dt, b_re, b_im, c_re, c_im)
    hb = _norm_bf16(x, modl, g, seg_len, tm)
    ug = hb.reshape(nc, L, n_g, GROUP_CH).transpose(2, 0, 1, 3).reshape(n_g, nc, L * GROUP_CH)
    tc = math.gcd(nc, SSM_ROW_TILE)
    sf, sb = _s5_states(ug, pcat, tc)
    chunks_per_seg = seg_len // L
    reset_f = jnp.repeat(jnp.asarray(seq_first, jnp.int32), chunks_per_seg) * (
        jnp.tile(jnp.arange(chunks_per_seg) == 0, len(seq_first))).astype(jnp.int32)
    reset_b = jnp.repeat(jnp.asarray(seq_last, jnp.int32), chunks_per_seg) * (
        jnp.tile(jnp.arange(chunks_per_seg) == chunks_per_seg - 1, len(seq_last))).astype(jnp.int32)
    xf, xb = _s5_scan(sf, sb, mults, reset_f, reset_b, math.gcd(nc, SSM_SCAN_TILE))
    yg = _s5_out(ug, xf, xb, w, qcat, tc)
    y = yg.reshape(n_g, nc, L, GROUP_CH).transpose(1, 2, 0, 3).reshape(t, d)
    return _s5_glu(x, y, modl, g, d_skip.reshape(1, d), w_glu.astype(BF16), seg_len, tm)


def _router_kernel(x_ref, mod_ref, g_ref, w_ref, b_ref, h_ref, idx_ref, gate_ref, rank_ref, cnt_ref,
                   carry_ref):
    tm = x_ref.shape[0]
    n_e = w_ref.shape[0]

    @pl.when(pl.program_id(0) == 0)
    def _():
        carry_ref[...] = jnp.zeros(carry_ref.shape, F32)

    h = _rms_mod(x_ref[...], g_ref[...], mod_ref[0, 4:5, :], mod_ref[0, 3:4, :])
    h_ref[...] = h
    logits = lax.dot_general(w_ref[...], h, (((1,), (1,)), ((), ())), precision=HIGHEST,
                             preferred_element_type=F32) + b_ref[...]
    e_iota = lax.broadcasted_iota(jnp.int32, (n_e, tm), 0)
    work = logits
    chosen = jnp.zeros((n_e, tm), F32)
    vals, idxs = [], []
    for _ in range(TOP_K):
        m = jnp.max(work, axis=0, keepdims=True)
        ix = jnp.min(jnp.where(work == m, e_iota, n_e), axis=0, keepdims=True)
        hit = e_iota == ix
        work = jnp.where(hit, -jnp.inf, work)
        chosen = jnp.where(hit, 1.0, chosen)
        vals.append(m)
        idxs.append(ix)
    v = jnp.concatenate(vals, axis=0)
    ex = jnp.exp(v - v[0:1])
    gate_ref[...] = ex / jnp.sum(ex, axis=0, keepdims=True)
    idx_ref[...] = jnp.concatenate(idxs, axis=0)
    row = lax.broadcasted_iota(jnp.int32, (tm, tm), 0)
    col = lax.broadcasted_iota(jnp.int32, (tm, tm), 1)
    before = (row < col).astype(BF16)
    cum = jnp.dot(chosen.astype(BF16), before, preferred_element_type=F32) + carry_ref[...]
    ranks = [jnp.sum(jnp.where(e_iota == ix, cum, 0.0), axis=0, keepdims=True) for ix in idxs]
    rank_ref[...] = jnp.concatenate(ranks, axis=0).astype(jnp.int32)
    carry_ref[...] = carry_ref[...] + jnp.sum(chosen, axis=1, keepdims=True)
    cnt_ref[...] = jnp.broadcast_to(carry_ref[...], cnt_ref.shape).astype(jnp.int32)


def _router(x, modl, g, w_router_t, b_router, seg_len, tm):
    t, d = x.shape
    n_e = w_router_t.shape[0]
    per_seg = seg_len // tm
    sel = pl.BlockSpec((TOP_K, tm), lambda i: (0, i))
    return pl.pallas_call(
        _router_kernel,
        grid=(t // tm,),
        in_specs=[pl.BlockSpec((tm, d), lambda i: (i, 0)),
                  pl.BlockSpec((1, SUBLANES, d), lambda i: (i // per_seg, 0, 0)),
                  pl.BlockSpec((1, d), lambda i: (0, 0)),
                  pl.BlockSpec((n_e, d), lambda i: (0, 0)),
                  pl.BlockSpec((n_e, 1), lambda i: (0, 0))],
        out_specs=[pl.BlockSpec((tm, d), lambda i: (i, 0)), sel, sel, sel,
                   pl.BlockSpec((n_e, LANES), lambda i: (0, 0))],
        out_shape=[jax.ShapeDtypeStruct((t, d), F32),
                   jax.ShapeDtypeStruct((TOP_K, t), jnp.int32),
                   jax.ShapeDtypeStruct((TOP_K, t), F32),
                   jax.ShapeDtypeStruct((TOP_K, t), jnp.int32),
                   jax.ShapeDtypeStruct((n_e, LANES), jnp.int32)],
        scratch_shapes=[pltpu.VMEM((n_e, 1), F32)],
        compiler_params=_cparams("arbitrary"),
        name="moe_router",
    )(x, modl, g, w_router_t, b_router)


def _dispatch_kernel(pad_from_ref, pad_n_ref, dest_ref, h_ref, xs_hbm, zrow_ref, sem, zsem):
    tm = h_ref.shape[0]

    def zero_row(r):
        return pltpu.make_async_copy(zrow_ref, xs_hbm.at[pl.ds(r, 1), :], zsem)

    @pl.when(pl.program_id(0) == 0)
    def _():
        zrow_ref[...] = jnp.zeros(zrow_ref.shape, F32)
        for e in range(pad_n_ref.shape[0]):
            def fill(r, carry, e=e):
                zero_row(pad_from_ref[e] + r).start()
                return carry
            lax.fori_loop(0, pad_n_ref[e], fill, 0)
        for e in range(pad_n_ref.shape[0]):
            def drain(r, carry):
                zero_row(0).wait()
                return carry
            lax.fori_loop(0, pad_n_ref[e], drain, 0)

    def issue(j, carry):
        for k in range(TOP_K):
            pltpu.make_async_copy(h_ref.at[pl.ds(j, 1), :],
                                  xs_hbm.at[pl.ds(dest_ref[0, 0, j * TOP_K + k], 1), :], sem).start()
        return carry

    lax.fori_loop(0, tm, issue, 0)
    for _ in range(TOP_K):
        pltpu.make_async_copy(h_ref, xs_hbm.at[pl.ds(0, tm), :], sem).wait()


def _dispatch(h, dest_tk, pad_from, pad_n, n_rows, tm):
    t, d = h.shape
    return pl.pallas_call(
        _dispatch_kernel,
        grid_spec=pltpu.PrefetchScalarGridSpec(
            num_scalar_prefetch=2, grid=(t // tm,),
            in_specs=[pl.BlockSpec((1, 1, tm * TOP_K), lambda i, pf, pn: (i, 0, 0), memory_space=pltpu.SMEM),
                      pl.BlockSpec((tm, d), lambda i, pf, pn: (i, 0))],
            out_specs=pl.BlockSpec(memory_space=pl.ANY),
            scratch_shapes=[pltpu.VMEM((1, d), F32), pltpu.SemaphoreType.DMA(()),
                            pltpu.SemaphoreType.DMA(())]),
        out_shape=jax.ShapeDtypeStruct((n_rows, d), F32),
        compiler_params=_cparams("arbitrary"),
        name="moe_dispatch",
    )(pad_from, pad_n, dest_tk.reshape(t // tm, 1, tm * TOP_K), h)


def _expert_kernel(be_ref, bi_ref, nu_ref, x_ref, wgu_ref, bgu_ref, wd_ref, bd_ref, y_ref):
    del be_ref, bi_ref
    n_ff = wd_ref.shape[1]

    @pl.when(pl.program_id(0) < nu_ref[0])
    def _():
        x = x_ref[...].astype(BF16)
        acc = jnp.zeros(y_ref.shape, F32) + bd_ref[0]
        for c in range(n_ff // FF_CHUNK):
            lo = c * FF_CHUNK
            g_part = (jnp.dot(x, wgu_ref[0, :, lo:lo + FF_CHUNK], preferred_element_type=F32)
                      + bgu_ref[0, :, lo:lo + FF_CHUNK])
            u_part = (jnp.dot(x, wgu_ref[0, :, n_ff + lo:n_ff + lo + FF_CHUNK], preferred_element_type=F32)
                      + bgu_ref[0, :, n_ff + lo:n_ff + lo + FF_CHUNK])
            g_part = jnp.minimum(g_part, SWIGLU_LIMIT)
            u_part = jnp.clip(u_part, -SWIGLU_LIMIT, SWIGLU_LIMIT)
            act = (u_part + 1.0) * g_part * jax.nn.sigmoid(SWIGLU_ALPHA * g_part)
            acc = acc + jnp.dot(act.astype(BF16), wd_ref[0, lo:lo + FF_CHUNK, :],
                                preferred_element_type=F32)
        y_ref[...] = acc


def _experts(xs, blk_expert, blk_idx, n_used, w_gu_bf, b_gu, w_down_bf, b_down, tme):
    nr, d = xs.shape
    n_e, _, two_f = w_gu_bf.shape
    n_ff = two_f // 2
    nblk = nr // tme
    rows = pl.BlockSpec((tme, d), lambda b, be, bi, nu: (bi[b], 0))
    return pl.pallas_call(
        _expert_kernel,
        grid_spec=pltpu.PrefetchScalarGridSpec(
            num_scalar_prefetch=3, grid=(nblk,),
            in_specs=[rows,
                      pl.BlockSpec((1, d, two_f), lambda b, be, bi, nu: (be[b], 0, 0)),
                      pl.BlockSpec((1, 1, two_f), lambda b, be, bi, nu: (be[b], 0, 0)),
                      pl.BlockSpec((1, n_ff, d), lambda b, be, bi, nu: (be[b], 0, 0)),
                      pl.BlockSpec((1, 1, d), lambda b, be, bi, nu: (be[b], 0, 0))],
            out_specs=rows),
        out_shape=jax.ShapeDtypeStruct(xs.shape, F32),
        compiler_params=_cparams("arbitrary"),
        name="moe_experts",
    )(blk_expert, blk_idx, n_used, xs, w_gu_bf, b_gu.reshape(n_e, 1, two_f), w_down_bf,
      b_down.reshape(n_e, 1, d))


def _combine_kernel(dest_ref, x_ref, gate_ref, mod_ref, g_ref, ys_hbm, o_ref, buf, sem, *, final_norm):
    tm = x_ref.shape[0]

    def issue(j, carry):
        for k in range(TOP_K):
            pltpu.make_async_copy(ys_hbm.at[pl.ds(dest_ref[0, 0, j * TOP_K + k], 1), :],
                                  buf.at[k, pl.ds(j, 1), :], sem).start()
        return carry

    lax.fori_loop(0, tm, issue, 0)
    for k in range(TOP_K):
        pltpu.make_async_copy(ys_hbm.at[pl.ds(0, tm), :], buf.at[k], sem).wait()
    gates = gate_ref[...]
    moe = gates[:, 0:1] * buf[0]
    for k in range(1, TOP_K):
        moe = moe + gates[:, k:k + 1] * buf[k]
    x = x_ref[...] + mod_ref[0, 5:6, :] * moe
    if final_norm:
        ms = jnp.mean(x * x, axis=-1, keepdims=True)
        x = x * lax.rsqrt(ms + NORM_EPS) * g_ref[...]
    o_ref[...] = x


def _combine(x, ys, dest_tk, gates, modl, final_g, seg_len, tm, final_norm, row0=0, rows=None):
    t, d = x.shape
    rows = t if rows is None else rows
    per_seg = seg_len // tm
    b0 = row0 // tm
    return pl.pallas_call(
        functools.partial(_combine_kernel, final_norm=final_norm),
        grid=(rows // tm,),
        in_specs=[pl.BlockSpec((1, 1, tm * TOP_K), lambda i: (i + b0, 0, 0), memory_space=pltpu.SMEM),
                  pl.BlockSpec((tm, d), lambda i: (i + b0, 0)),
                  pl.BlockSpec((tm, TOP_K), lambda i: (i + b0, 0)),
                  pl.BlockSpec((1, SUBLANES, d), lambda i: ((i + b0) // per_seg, 0, 0)),
                  pl.BlockSpec((1, d), lambda i: (0, 0)),
                  pl.BlockSpec(memory_space=pl.ANY)],
        out_specs=pl.BlockSpec((tm, d), lambda i: (i, 0)),
        out_shape=jax.ShapeDtypeStruct((rows, d), F32),
        scratch_shapes=[pltpu.VMEM((TOP_K, tm, d), F32), pltpu.SemaphoreType.DMA(())],
        compiler_params=_cparams("arbitrary"),
        name="moe_combine",
    )(dest_tk.reshape(t // tm, 1, tm * TOP_K), x, gates, modl, final_g, ys)


def _moe_dispatch_and_experts(x, modl, g, w_router, b_router, w_gu, b_gu, w_down, b_down, seg_len, tm):
    t, d = x.shape
    n_e = w_router.shape[1]
    tme = math.gcd(t * TOP_K, EXPERT_ROW_TILE)
    h, idx_t, gate_t, rank_t, cnt = _router(x, modl, g, w_router.T, b_router.reshape(n_e, 1), seg_len, tm)
    counts = cnt[:, 0]
    padded = (counts + tme - 1) // tme * tme
    pad_end = jnp.cumsum(padded)
    pad_start = pad_end - padded
    nblk = (t * TOP_K) // tme + n_e
    n_used = (pad_end[-1] // tme).astype(jnp.int32).reshape(1)
    blk_idx = jnp.minimum(jnp.arange(nblk, dtype=jnp.int32), n_used[0] - 1)
    blk_expert = jnp.minimum(jnp.sum((pad_end[None, :] <= (blk_idx * tme)[:, None]).astype(jnp.int32), axis=1),
                             n_e - 1)
    start_of = jnp.sum(jnp.where(idx_t[None] == jnp.arange(n_e, dtype=jnp.int32)[:, None, None],
                                 pad_start[:, None, None], 0), axis=0)
    dest = (start_of + rank_t).astype(jnp.int32)
    dest_tk = dest.T.reshape(-1)
    xs = _dispatch(h, dest_tk, (pad_start + counts).astype(jnp.int32), (padded - counts).astype(jnp.int32),
                   nblk * tme, tm)
    ys = _experts(xs, blk_expert, blk_idx, n_used, w_gu.astype(BF16), b_gu, w_down.astype(BF16), b_down, tme)
    return ys, dest_tk, gate_t.T


def kernel(x_prompt, x_sample, c_prompt, c_sample, norm1_g, norm2_g, final_g, w_ada, b_ada, w_qkv, w_o,
           lam_q1, lam_k1, lam_q2, lam_k2, subln_g, ssm_a_re, ssm_a_im, ssm_log_dt, ssm_b_re, ssm_b_im,
           ssm_c_re, ssm_c_im, ssm_d, ssm_w_glu, w_router, b_router, w_gu, b_gu, w_down, b_down):
    n_b, seg_len, d = x_prompt.shape
    n_bs, s_len, _ = x_sample.shape
    depth = w_ada.shape[0]
    assert s_len % seg_len == 0 and s_len // seg_len in (1, 2)
    per_sample = s_len // seg_len
    n_seg = n_b + n_bs * per_sample
    t = n_seg * seg_len
    tm = math.gcd(seg_len, TOKEN_TILE)

    seg_seq = list(range(n_b)) + [n_b + j for j in range(n_bs) for _ in range(per_sample)]
    seg_pos = [0] * n_b + [r for _ in range(n_bs) for r in range(per_sample)]
    seg_a = list(range(n_b)) + [n_b + j * per_sample for j in range(n_bs) for _ in range(per_sample)]
    seg_b = list(range(n_b)) + [n_b + j * per_sample + per_sample - 1 for j in range(n_bs) for _ in range(per_sample)]
    two = [0] * n_b + [int(per_sample == 2)] * (n_bs * per_sample)
    seq_first = [1] * n_b + [int(r == 0) for _ in range(n_bs) for r in range(per_sample)]
    seq_last = [1] * n_b + [int(r == per_sample - 1) for _ in range(n_bs) for r in range(per_sample)]
    per_seg = seg_len // tm
    pos_blk = jnp.asarray([seg_pos[s] * per_seg + r for s in range(n_seg) for r in range(per_seg)], jnp.int32)

    x = jnp.concatenate([x_prompt.reshape(n_b * seg_len, d), x_sample.reshape(n_bs * s_len, d)], axis=0)

    n_c = n_b + n_bs
    c_rows = -(-n_c // SUBLANES) * SUBLANES
    c_all = jnp.concatenate([c_prompt, c_sample, jnp.zeros((c_rows - n_c, d), F32)], axis=0)
    mod = _ada(c_all, w_ada, b_ada)
    mod = mod[:, jnp.asarray(seg_seq)].reshape(depth, n_seg, 6, d)
    mod = jnp.concatenate([mod, jnp.zeros((depth, n_seg, SUBLANES - 6, d), F32)], axis=2)

    inv_freq = ROPE_THETA ** (-jnp.arange(0, HEAD_DIM, 2, dtype=F32) / HEAD_DIM)
    ang = jnp.arange(s_len, dtype=F32)[:, None] * inv_freq[None, :]
    cos_t = jnp.tile(jnp.cos(ang), (1, 2 * LANES // HEAD_DIM))
    sin_t = jnp.tile(jnp.concatenate([-jnp.sin(ang), jnp.sin(ang)], axis=1), (1, LANES // HEAD_DIM))

    out_p = out_s = None
    for i in range(depth):
        modl = mod[i]
        g1 = norm1_g[i].reshape(1, d)
        g2 = norm2_g[i].reshape(1, d)
        j = i // 2
        if i % 2 == 0:
            lambda_init = 0.8 - 0.6 * math.exp(-0.3 * i)
            q, k, vt = _qkv(x, modl, g1, w_qkv[j], cos_t, sin_t, pos_blk, seg_len, tm)
            lam_pack = jnp.zeros((SUBLANES, 2 * HEAD_DIM), F32).at[0:4, 0:HEAD_DIM].set(
                jnp.stack([lam_q1[j], lam_k1[j], lam_q2[j], lam_k2[j]]).astype(F32))
            o = _flash(q, k, vt, lam_pack, subln_g[j].reshape(2 * HEAD_DIM, 1).astype(F32),
                       jnp.asarray(seg_a, jnp.int32), jnp.asarray(seg_b, jnp.int32),
                       jnp.asarray(two, jnp.int32), seg_len, lambda_init)
            x = _proj_res(o, w_o[j].astype(BF16), x, modl, seg_len, tm, gate_row=2)
        else:
            params = (ssm_a_re[j], ssm_a_im[j], ssm_log_dt[j], ssm_b_re[j], ssm_b_im[j], ssm_c_re[j],
                      ssm_c_im[j], ssm_d[j], ssm_w_glu[j])
            x = _s5_layer(x, modl, g1, params, seg_len, seq_first, seq_last, tm)
        ys, dest_tk, gates = _moe_dispatch_and_experts(x, modl, g2, w_router[i], b_router[i], w_gu[i], b_gu[i],
                                                       w_down[i], b_down[i], seg_len, tm)
        fg = final_g.reshape(1, d)
        if i == depth - 1:
            out_p = _combine(x, ys, dest_tk, gates, modl, fg, seg_len, tm, True, 0, n_b * seg_len)
            out_s = _combine(x, ys, dest_tk, gates, modl, fg, seg_len, tm, True, n_b * seg_len, n_bs * s_len)
        else:
            x = _combine(x, ys, dest_tk, gates, modl, fg, seg_len, tm, False)
    return out_p.reshape(n_b, seg_len, d), out_s.reshape(n_bs, s_len, d)
```

```python
import functools
import math

import jax
import jax.numpy as jnp
from jax import lax
from jax.experimental import pallas as pl
from jax.experimental.pallas import tpu as pltpu

F32 = jnp.float32
BF16 = jnp.bfloat16
HIGHEST = lax.Precision.HIGHEST

N_DIFF_HEADS = 8
HEAD_DIM = 64
ROPE_THETA = 10000.0
SUBLN_EPS = 1e-5
NORM_EPS = 1e-6
GROUP_CH = 16
STATE_DIM = 64
N_EXPERTS = 32
TOP_K = 4
SWIGLU_ALPHA = 1.702
SWIGLU_LIMIT = 7.0

LANES = 128
SUBLANES = 8
VMEM_LIMIT_BYTES = 56 * 1024 * 1024

TOKEN_TILE = 512
ATT_Q_TILE = 512
ATT_UNROLL = 8
VT_ONES_ROWS = 16
LOG2_E = math.log2(math.e)
SSM_CHUNK = 16
SSM_GROUP_TILE = 8
SSM_ROW_TILE = 512
SSM_SCAN_TILE = 64
EXPERT_ROW_TILE = 512
FF_CHUNK = 512
DMA_ISSUE_UNROLL = 8


def _cparams(*sem):
    return pltpu.CompilerParams(dimension_semantics=sem, vmem_limit_bytes=VMEM_LIMIT_BYTES)


def _rms_mod(x, g, scale, shift):
    ms = jnp.mean(x * x, axis=-1, keepdims=True)
    return x * lax.rsqrt(ms + NORM_EPS) * g * (1.0 + scale) + shift


def _ada_kernel(c_ref, w_ref, b_ref, o_ref):
    c = c_ref[...]
    cond = c * jax.nn.sigmoid(c)
    o_ref[0] = jnp.dot(cond, w_ref[0], precision=HIGHEST, preferred_element_type=F32) + b_ref[0]


def _ada(c_pad, w_ada, b_ada):
    depth, d, n6 = w_ada.shape
    rows = c_pad.shape[0]
    tn = 1536 if n6 % 1536 == 0 else n6
    return pl.pallas_call(
        _ada_kernel,
        grid=(depth, n6 // tn),
        in_specs=[pl.BlockSpec((rows, d), lambda i, j: (0, 0)),
                  pl.BlockSpec((1, d, tn), lambda i, j: (i, 0, j)),
                  pl.BlockSpec((1, 1, tn), lambda i, j: (i, 0, j))],
        out_specs=pl.BlockSpec((1, rows, tn), lambda i, j: (i, 0, j)),
        out_shape=jax.ShapeDtypeStruct((depth, rows, n6), F32),
        compiler_params=_cparams("arbitrary", "arbitrary"),
        name="ada_mod",
    )(c_pad, w_ada, b_ada.reshape(depth, 1, n6))


def _qkv_kernel(pos_ref, x_ref, mod_ref, g_ref, w_ref, wvt_ref, cos_ref, sin_ref, q_ref, k_ref, vt_ref):
    del pos_ref
    d = x_ref.shape[1]
    h = _rms_mod(x_ref[...], g_ref[...], mod_ref[0, 1:2, :], mod_ref[0, 0:1, :]).astype(BF16)
    qkv = jnp.dot(h, w_ref[...], preferred_element_type=F32)
    vt = lax.dot_general(wvt_ref[...], h, (((1,), (1,)), ((), ())), preferred_element_type=F32)
    ones = jnp.ones((VT_ONES_ROWS, vt.shape[1]), BF16)
    for hd in range(vt_ref.shape[0]):
        vt_ref[hd, 0, 0:LANES, :] = vt[hd * LANES:(hd + 1) * LANES, :].astype(BF16)
        vt_ref[hd, 0, LANES:LANES + VT_ONES_ROWS, :] = ones
    cos = cos_ref[...]
    sin = sin_ref[...]
    lane = lax.broadcasted_iota(jnp.int32, cos.shape, 1)
    first_half = (lane % HEAD_DIM) < (HEAD_DIM // 2)

    def rotary(t):
        partner = jnp.where(first_half,
                            pltpu.roll(t, LANES - HEAD_DIM // 2, 1),
                            pltpu.roll(t, HEAD_DIM // 2, 1))
        return t * cos + partner * sin

    for j in range(d // LANES):
        sl = slice(j * LANES, (j + 1) * LANES)
        q_ref[:, sl] = (rotary(qkv[:, sl]) * (HEAD_DIM ** -0.5 * LOG2_E)).astype(BF16)
        k_ref[:, sl] = rotary(qkv[:, d + j * LANES:d + (j + 1) * LANES]).astype(BF16)


def _qkv(x, modl, g, w_qkv, cos_t, sin_t, pos_blk, seg_len, tm):
    t, d = x.shape
    per_seg = seg_len // tm
    n_vh = d // LANES
    tok = lambda i, p: (i, 0)
    out = jax.ShapeDtypeStruct((t, d), BF16)
    w_qk = w_qkv[:, :2 * d].astype(BF16)
    w_vt = w_qkv[:, 2 * d:].T.astype(BF16)
    return pl.pallas_call(
        _qkv_kernel,
        grid_spec=pltpu.PrefetchScalarGridSpec(
            num_scalar_prefetch=1, grid=(t // tm,),
            in_specs=[pl.BlockSpec((tm, d), tok),
                      pl.BlockSpec((1, SUBLANES, d), lambda i, p: (i // per_seg, 0, 0)),
                      pl.BlockSpec((1, d), lambda i, p: (0, 0)),
                      pl.BlockSpec((d, 2 * d), lambda i, p: (0, 0)),
                      pl.BlockSpec((d, d), lambda i, p: (0, 0)),
                      pl.BlockSpec((tm, LANES), lambda i, p: (p[i], 0)),
                      pl.BlockSpec((tm, LANES), lambda i, p: (p[i], 0))],
            out_specs=[pl.BlockSpec((tm, d), tok), pl.BlockSpec((tm, d), tok),
                       pl.BlockSpec((n_vh, 1, LANES + VT_ONES_ROWS, tm), lambda i, p: (0, i, 0, 0))]),
        out_shape=[out, out, jax.ShapeDtypeStruct((n_vh, t // tm, LANES + VT_ONES_ROWS, tm), BF16)],
        compiler_params=_cparams("arbitrary"),
        name="qkv_rotary",
    )(pos_blk, x, modl, g, w_qk, w_vt, cos_t, sin_t)


def _flash_kernel(sa_ref, sb_ref, two_ref, q_ref, ka_ref, vta_ref, kb_ref, vtb_ref, lam_ref, g_ref,
                  o_ref, qt_ref, m_ref, acc_ref, *, tk, unroll, lambda_init):
    del sa_ref, sb_ref
    seg = pl.program_id(0)
    qt = q_ref[...].astype(F32).T
    chan = lax.broadcasted_iota(jnp.int32, qt.shape, 0)
    qt_ref[0] = jnp.where(chan < HEAD_DIM, qt, 0.0).astype(BF16)
    qt_ref[1] = jnp.where(chan >= HEAD_DIM, qt, 0.0).astype(BF16)
    m_ref[...] = jnp.full(m_ref.shape, -jnp.inf, F32)
    acc_ref[...] = jnp.zeros(acc_ref.shape, F32)

    def sweep(k_ref, vt_ref):
        units = [(tile, g) for tile in range(unroll) for g in range(2)]

        def scores(jj, u):
            tile, g = units[u]
            off = pl.multiple_of((jj * unroll + tile) * tk, tk)
            half = tk // 2
            return jnp.concatenate(
                [jnp.dot(k_ref[pl.ds(off + r * half, half), :], qt_ref[g], preferred_element_type=F32)
                 for r in range(2)], axis=0)

        def softmax(u, s):
            g = units[u][1]
            m_prev = m_ref[g]
            m_new = jnp.maximum(m_prev, jnp.max(s, axis=0, keepdims=True))
            p = jnp.exp2((s - m_new).astype(BF16))
            alpha = jnp.exp2(m_prev - m_new)
            m_ref[g] = m_new
            return p, alpha

        def values(jj, u, p, alpha):
            tile, g = units[u]
            acc_ref[g] = alpha * acc_ref[g] + jnp.dot(vt_ref[0, jj * unroll + tile], p,
                                                      preferred_element_type=F32)

        def body(jj, carry):
            s_next = scores(jj, 0)
            pending = None
            for u in range(len(units)):
                s_cur = s_next
                if u + 1 < len(units):
                    s_next = scores(jj, u + 1)
                if pending is not None:
                    values(jj, *pending)
                pending = (u,) + softmax(u, s_cur)
            values(jj, *pending)
            return carry
        lax.fori_loop(0, k_ref.shape[0] // (tk * unroll), body, 0)

    sweep(ka_ref, vta_ref)

    @pl.when(two_ref[seg] == 1)
    def _():
        sweep(kb_ref, vtb_ref)

    o1 = acc_ref[0, 0:LANES, :] / acc_ref[0, LANES:LANES + 1, :]
    o2 = acc_ref[1, 0:LANES, :] / acc_ref[1, LANES:LANES + 1, :]
    lam = (jnp.exp(jnp.sum(lam_ref[0:1, :] * lam_ref[1:2, :], axis=-1, keepdims=True))
           - jnp.exp(jnp.sum(lam_ref[2:3, :] * lam_ref[3:4, :], axis=-1, keepdims=True))
           + lambda_init)
    ot = o1 - lam * o2
    ms = jnp.mean(ot * ot, axis=0, keepdims=True)
    ot = ot * lax.rsqrt(ms + SUBLN_EPS) * g_ref[...] * (1.0 - lambda_init)
    o_ref[...] = ot.T.astype(BF16)


def _flash(q, k, vt, lam_pack, subln_g_col, seg_a, seg_b, two, seg_len, lambda_init):
    t, d = q.shape
    nseg = t // seg_len
    n_vh, _, v_rows, tk = vt.shape
    width = 2 * HEAD_DIM
    per_seg = seg_len // tk
    tq = min(ATT_Q_TILE, seg_len)
    nq = seg_len // tq
    kern = functools.partial(_flash_kernel, tk=tk, unroll=min(ATT_UNROLL, per_seg), lambda_init=lambda_init)
    k_a = pl.BlockSpec((seg_len, width), lambda s, h, i, sa, sb, tw: (sa[s], h))
    k_b = pl.BlockSpec((seg_len, width), lambda s, h, i, sa, sb, tw: (sb[s], h))
    vt_a = pl.BlockSpec((1, per_seg, v_rows, tk), lambda s, h, i, sa, sb, tw: (h, sa[s], 0, 0))
    vt_b = pl.BlockSpec((1, per_seg, v_rows, tk), lambda s, h, i, sa, sb, tw: (h, sb[s], 0, 0))
    qo = pl.BlockSpec((tq, width), lambda s, h, i, sa, sb, tw: (s * nq + i, h))
    return pl.pallas_call(
        kern,
        grid_spec=pltpu.PrefetchScalarGridSpec(
            num_scalar_prefetch=3, grid=(nseg, n_vh, nq),
            in_specs=[qo, k_a, vt_a, k_b, vt_b,
                      pl.BlockSpec((SUBLANES, width), lambda s, h, i, sa, sb, tw: (0, 0)),
                      pl.BlockSpec((width, 1), lambda s, h, i, sa, sb, tw: (0, 0))],
            out_specs=qo,
            scratch_shapes=[pltpu.VMEM((2, width, tq), BF16),
                            pltpu.VMEM((2, 1, tq), F32),
                            pltpu.VMEM((2, v_rows, tq), F32)]),
        out_shape=jax.ShapeDtypeStruct((t, d), BF16),
        compiler_params=_cparams("arbitrary", "arbitrary", "arbitrary"),
        name="diff_flash_attention",
    )(seg_a, seg_b, two, q, k, vt, k, vt, lam_pack, subln_g_col)


def _proj_res_kernel(a_ref, w_ref, x_ref, mod_ref, o_ref, *, gate_row):
    y = jnp.dot(a_ref[...], w_ref[...], preferred_element_type=F32)
    o_ref[...] = x_ref[...] + mod_ref[0, gate_row:gate_row + 1, :] * y


def _proj_res(a, w_bf, x, modl, seg_len, tm, gate_row):
    t, d = x.shape
    per_seg = seg_len // tm
    return pl.pallas_call(
        functools.partial(_proj_res_kernel, gate_row=gate_row),
        grid=(t // tm,),
        in_specs=[pl.BlockSpec((tm, a.shape[1]), lambda i: (i, 0)),
                  pl.BlockSpec(w_bf.shape, lambda i: (0, 0)),
                  pl.BlockSpec((tm, d), lambda i: (i, 0)),
                  pl.BlockSpec((1, SUBLANES, d), lambda i: (i // per_seg, 0, 0))],
        out_specs=pl.BlockSpec((tm, d), lambda i: (i, 0)),
        out_shape=jax.ShapeDtypeStruct((t, d), F32),
        compiler_params=_cparams("arbitrary"),
        name="proj_residual",
    )(a, w_bf, x, modl)


def _norm_bf16_kernel(x_ref, mod_ref, g_ref, o_ref):
    o_ref[...] = _rms_mod(x_ref[...], g_ref[...], mod_ref[0, 1:2, :], mod_ref[0, 0:1, :]).astype(BF16)


def _norm_bf16(x, modl, g, seg_len, tm):
    t, d = x.shape
    per_seg = seg_len // tm
    return pl.pallas_call(
        _norm_bf16_kernel,
        grid=(t // tm,),
        in_specs=[pl.BlockSpec((tm, d), lambda i: (i, 0)),
                  pl.BlockSpec((1, SUBLANES, d), lambda i: (i // per_seg, 0, 0)),
                  pl.BlockSpec((1, d), lambda i: (0, 0))],
        out_specs=pl.BlockSpec((tm, d), lambda i: (i, 0)),
        out_shape=jax.ShapeDtypeStruct((t, d), BF16),
        compiler_params=_cparams("arbitrary"),
        name="s5_prenorm",
    )(x, modl, g)


def _s5_tables(a_re, a_im, log_dt, b_re, b_im, c_re, c_im):
    f32 = F32
    L = SSM_CHUNK
    ar, ai = a_re.astype(f32), a_im.astype(f32)
    n_g, n_p = ar.shape[1], ar.shape[2]
    n_h = b_re.shape[-1]
    dt = jnp.exp(log_dt.astype(f32))[..., None]
    mag = jnp.exp(ar * dt)
    lr, li = mag * jnp.cos(ai * dt), mag * jnp.sin(ai * dt)
    den = ar * ar + ai * ai
    fr = ((lr - 1.0) * ar + li * ai) / den
    fi = (li * ar - (lr - 1.0) * ai) / den
    br, bi = b_re.astype(f32), b_im.astype(f32)
    bbr = fr[..., None] * br - fi[..., None] * bi
    bbi = fr[..., None] * bi + fi[..., None] * br
    cr, ci = c_re.astype(f32), c_im.astype(f32)
    kk = jnp.arange(L + 1, dtype=f32)
    pmag = jnp.exp(ar[..., None] * dt[..., None] * kk)
    pang = ai[..., None] * dt[..., None] * kk
    pr, pi = pmag * jnp.cos(pang), pmag * jnp.sin(pang)
    er = cr[..., None] * pr[:, :, None] - ci[..., None] * pi[:, :, None]
    ei = cr[..., None] * pi[:, :, None] + ci[..., None] * pr[:, :, None]
    klag = (jnp.einsum('dgopk,dgpi->dgkoi', er[..., :L], bbr, precision=HIGHEST)
            - jnp.einsum('dgopk,dgpi->dgkoi', ei[..., :L], bbi, precision=HIGHEST))
    kf, kb = klag[0], klag[1]
    kfull = jnp.concatenate([kb[:, 1:][:, ::-1], (kf[:, 0] + kb[:, 0])[:, None], kf[:, 1:]], axis=1)
    s_idx = jnp.arange(L)[:, None]
    t_idx = jnp.arange(L)[None, :]
    w = kfull[:, t_idx - s_idx + L - 1]
    w = jnp.transpose(w, (0, 1, 4, 2, 3)).reshape(n_g, L * n_h, L * n_h)
    def in_to_state(d, pw_idx):
        pwr = jnp.take(pr[d], pw_idx, axis=-1)
        pwi = jnp.take(pi[d], pw_idx, axis=-1)
        re = pwr[..., None] * bbr[d][:, :, None, :] - pwi[..., None] * bbi[d][:, :, None, :]
        im = pwr[..., None] * bbi[d][:, :, None, :] + pwi[..., None] * bbr[d][:, :, None, :]
        m = jnp.concatenate([re, im], axis=1)
        return jnp.transpose(m, (0, 2, 3, 1)).reshape(n_g, L * n_h, 2 * n_p)
    pcat = jnp.concatenate([in_to_state(0, L - 1 - jnp.arange(L)), in_to_state(1, jnp.arange(L))], axis=-1)
    def state_to_out(d, pw_idx):
        e_r = jnp.take(er[d], pw_idx, axis=-1)
        e_i = jnp.take(ei[d], pw_idx, axis=-1)
        m = jnp.concatenate([e_r, -e_i], axis=2)
        return jnp.transpose(m, (0, 2, 3, 1)).reshape(n_g, 2 * n_p, L * n_h)
    qcat = jnp.concatenate([state_to_out(0, jnp.arange(L) + 1), state_to_out(1, L - jnp.arange(L))], axis=1)

    def scan_mult(d):
        a_r, a_i = pr[d][..., L], pi[d][..., L]
        m1 = jnp.concatenate([a_r, a_r], axis=-1)
        m2 = jnp.concatenate([-a_i, a_i], axis=-1)
        lay = lambda m: jnp.transpose(m.reshape(n_g // SSM_GROUP_TILE, SSM_GROUP_TILE, 2 * n_p),
                                      (1, 0, 2)).reshape(SSM_GROUP_TILE, -1)
        return lay(m1), lay(m2)
    af1, af2 = scan_mult(0)
    ab1, ab2 = scan_mult(1)
    return w.astype(BF16), pcat.astype(BF16), qcat.astype(BF16), (af1, af2, ab1, ab2)


def _s5_states_kernel(u_ref, p_ref, sf_ref, sb_ref):
    half = sf_ref.shape[-1]
    for r in range(u_ref.shape[0]):
        s = jnp.dot(u_ref[r], p_ref[r], preferred_element_type=F32)
        sf_ref[:, r, :] = s[:, :half]
        sb_ref[:, r, :] = s[:, half:]


def _s5_states(ug, pcat, tc):
    n_g, nc, kdim = ug.shape
    gt = SSM_GROUP_TILE
    half = pcat.shape[-1] // 2
    out = jax.ShapeDtypeStruct((nc, gt, (n_g // gt) * half), F32)
    return pl.pallas_call(
        _s5_states_kernel,
        grid=(n_g // gt, nc // tc),
        in_specs=[pl.BlockSpec((gt, tc, kdim), lambda j, c: (j, c, 0)),
                  pl.BlockSpec((gt, kdim, 2 * half), lambda j, c: (j, 0, 0))],
        out_specs=[pl.BlockSpec((tc, gt, half), lambda j, c: (c, 0, j))] * 2,
        out_shape=[out, out],
        compiler_params=_cparams("arbitrary", "arbitrary"),
        name="s5_chunk_states",
    )(ug, pcat)


def _s5_scan_kernel(rf_ref, rb_ref, sf_ref, sb_ref, af1_ref, af2_ref, ab1_ref, ab2_ref,
                    xf_ref, xb_ref, cf_ref, cb_ref, *, tcs, nblk):
    i = pl.program_id(0)

    @pl.when(i == 0)
    def _():
        cf_ref[...] = jnp.zeros(cf_ref.shape, F32)
        cb_ref[...] = jnp.zeros(cb_ref.shape, F32)

    width = cf_ref.shape[-1]

    def swap(x):
        return jnp.concatenate(
            [pltpu.roll(x[:, j * LANES:(j + 1) * LANES], LANES // 2, 1) for j in range(width // LANES)],
            axis=1)

    af1, af2 = af1_ref[...], af2_ref[...]
    ab1, ab2 = ab1_ref[...], ab2_ref[...]

    def fwd(r, carry):
        carry = carry * (1 - rf_ref[i * tcs + r]).astype(F32)
        xf_ref[r] = carry
        return af1 * carry + af2 * swap(carry) + sf_ref[r]

    cf_ref[...] = lax.fori_loop(0, tcs, fwd, cf_ref[...])

    def bwd(rr, carry):
        r = tcs - 1 - rr
        carry = carry * (1 - rb_ref[(nblk - 1 - i) * tcs + r]).astype(F32)
        xb_ref[r] = carry
        return ab1 * carry + ab2 * swap(carry) + sb_ref[r]

    cb_ref[...] = lax.fori_loop(0, tcs, bwd, cb_ref[...])


def _s5_scan(sf, sb, mults, reset_f, reset_b, tcs):
    nc, gt, width = sf.shape
    nblk = nc // tcs
    fblk = pl.BlockSpec((tcs, gt, width), lambda i, a, b: (i, 0, 0))
    bblk = pl.BlockSpec((tcs, gt, width), lambda i, a, b: (nblk - 1 - i, 0, 0))
    tab = pl.BlockSpec((gt, width), lambda i, a, b: (0, 0))
    out = jax.ShapeDtypeStruct((nc, gt, width), F32)
    return pl.pallas_call(
        functools.partial(_s5_scan_kernel, tcs=tcs, nblk=nblk),
        grid_spec=pltpu.PrefetchScalarGridSpec(
            num_scalar_prefetch=2, grid=(nblk,),
            in_specs=[fblk, bblk, tab, tab, tab, tab],
            out_specs=[fblk, bblk],
            scratch_shapes=[pltpu.VMEM((gt, width), F32), pltpu.VMEM((gt, width), F32)]),
        out_shape=[out, out],
        compiler_params=_cparams("arbitrary"),
        name="s5_chunk_scan",
    )(reset_f, reset_b, sf, sb, *mults)


def _s5_out_kernel(u_ref, xf_ref, xb_ref, w_ref, q_ref, y_ref):
    for r in range(u_ref.shape[0]):
        xcat = jnp.concatenate([xf_ref[:, r, :], xb_ref[:, r, :]], axis=-1).astype(BF16)
        y_ref[r] = (jnp.dot(u_ref[r], w_ref[r], preferred_element_type=F32)
                    + jnp.dot(xcat, q_ref[r], preferred_element_type=F32))


def _s5_out(ug, xf, xb, w, qcat, tc):
    n_g, nc, kdim = ug.shape
    gt = SSM_GROUP_TILE
    half = xf.shape[-1] // (n_g // gt)
    xblk = pl.BlockSpec((tc, gt, half), lambda j, c: (c, 0, j))
    return pl.pallas_call(
        _s5_out_kernel,
        grid=(n_g // gt, nc // tc),
        in_specs=[pl.BlockSpec((gt, tc, kdim), lambda j, c: (j, c, 0)), xblk, xblk,
                  pl.BlockSpec((gt, kdim, kdim), lambda j, c: (j, 0, 0)),
                  pl.BlockSpec((gt, 2 * half, kdim), lambda j, c: (j, 0, 0))],
        out_specs=pl.BlockSpec((gt, tc, kdim), lambda j, c: (j, c, 0)),
        out_shape=jax.ShapeDtypeStruct((n_g, nc, kdim), F32),
        compiler_params=_cparams("arbitrary", "arbitrary"),
        name="s5_chunk_outputs",
    )(ug, xf, xb, w, qcat)


def _s5_glu_kernel(x_ref, y_ref, mod_ref, g_ref, dskip_ref, w_ref, o_ref):
    d = x_ref.shape[1]
    x = x_ref[...]
    h = _rms_mod(x, g_ref[...], mod_ref[0, 1:2, :], mod_ref[0, 0:1, :])
    z = jax.nn.gelu(y_ref[...] + dskip_ref[...] * h)
    vg = jnp.dot(z.astype(BF16), w_ref[...], preferred_element_type=F32)
    m = vg[:, :d] * jax.nn.sigmoid(vg[:, d:])
    o_ref[...] = x + mod_ref[0, 2:3, :] * m


def _s5_glu(x, y, modl, g, d_skip, w_bf, seg_len, tm):
    t, d = x.shape
    per_seg = seg_len // tm
    tok = pl.BlockSpec((tm, d), lambda i: (i, 0))
    vec = pl.BlockSpec((1, d), lambda i: (0, 0))
    return pl.pallas_call(
        _s5_glu_kernel,
        grid=(t // tm,),
        in_specs=[tok, tok, pl.BlockSpec((1, SUBLANES, d), lambda i: (i // per_seg, 0, 0)), vec, vec,
                  pl.BlockSpec((d, 2 * d), lambda i: (0, 0))],
        out_specs=tok,
        out_shape=jax.ShapeDtypeStruct((t, d), F32),
        compiler_params=_cparams("arbitrary"),
        name="s5_glu_residual",
    )(x, y, modl, g, d_skip, w_bf)


def _s5_layer(x, modl, g, params, seg_len, seq_first, seq_last, tm):
    a_re, a_im, log_dt, b_re, b_im, c_re, c_im, d_skip, w_glu = params
    t, d = x.shape
    L = SSM_CHUNK
    n_g = d // GROUP_CH
    nc = t // L
    w, pcat, qcat, mults = _s5_tables(a_re, a_im, log_dt, b_re, b_im, c_re, c_im)
    hb = _norm_bf16(x, modl, g, seg_len, tm)
    ug = hb.reshape(nc, L, n_g, GROUP_CH).transpose(2, 0, 1, 3).reshape(n_g, nc, L * GROUP_CH)
    tc = math.gcd(nc, SSM_ROW_TILE)
    sf, sb = _s5_states(ug, pcat, tc)
    chunks_per_seg = seg_len // L
    reset_f = jnp.repeat(jnp.asarray(seq_first, jnp.int32), chunks_per_seg) * (
        jnp.tile(jnp.arange(chunks_per_seg) == 0, len(seq_first))).astype(jnp.int32)
    reset_b = jnp.repeat(jnp.asarray(seq_last, jnp.int32), chunks_per_seg) * (
        jnp.tile(jnp.arange(chunks_per_seg) == chunks_per_seg - 1, len(seq_last))).astype(jnp.int32)
    xf, xb = _s5_scan(sf, sb, mults, reset_f, reset_b, math.gcd(nc, SSM_SCAN_TILE))
    yg = _s5_out(ug, xf, xb, w, qcat, tc)
    y = yg.reshape(n_g, nc, L, GROUP_CH).transpose(1, 2, 0, 3).reshape(t, d)
    return _s5_glu(x, y, modl, g, d_skip.reshape(1, d), w_glu.astype(BF16), seg_len, tm)


def _router_kernel(x_ref, mod_ref, g_ref, w_ref, b_ref, h_ref, idx_ref, gate_ref, rank_ref, cnt_ref,
                   carry_ref):
    tm = x_ref.shape[0]
    n_e = w_ref.shape[0]

    @pl.when(pl.program_id(0) == 0)
    def _():
        carry_ref[...] = jnp.zeros(carry_ref.shape, F32)

    h = _rms_mod(x_ref[...], g_ref[...], mod_ref[0, 4:5, :], mod_ref[0, 3:4, :])
    h_ref[...] = h
    logits = lax.dot_general(w_ref[...], h, (((1,), (1,)), ((), ())), precision=HIGHEST,
                             preferred_element_type=F32) + b_ref[...]
    e_iota = lax.broadcasted_iota(jnp.int32, (n_e, tm), 0)
    work = logits
    chosen = jnp.zeros((n_e, tm), F32)
    vals, idxs = [], []
    for _ in range(TOP_K):
        m = jnp.max(work, axis=0, keepdims=True)
        ix = jnp.min(jnp.where(work == m, e_iota, n_e), axis=0, keepdims=True)
        hit = e_iota == ix
        work = jnp.where(hit, -jnp.inf, work)
        chosen = jnp.where(hit, 1.0, chosen)
        vals.append(m)
        idxs.append(ix)
    v = jnp.concatenate(vals, axis=0)
    ex = jnp.exp(v - v[0:1])
    gate_ref[...] = ex / jnp.sum(ex, axis=0, keepdims=True)
    idx_ref[...] = jnp.concatenate(idxs, axis=0)
    row = lax.broadcasted_iota(jnp.int32, (tm, tm), 0)
    col = lax.broadcasted_iota(jnp.int32, (tm, tm), 1)
    before = (row < col).astype(BF16)
    cum = jnp.dot(chosen.astype(BF16), before, preferred_element_type=F32) + carry_ref[...]
    ranks = [jnp.sum(jnp.where(e_iota == ix, cum, 0.0), axis=0, keepdims=True) for ix in idxs]
    rank_ref[...] = jnp.concatenate(ranks, axis=0).astype(jnp.int32)
    carry_ref[...] = carry_ref[...] + jnp.sum(chosen, axis=1, keepdims=True)
    cnt_ref[...] = jnp.broadcast_to(carry_ref[...], cnt_ref.shape).astype(jnp.int32)


def _router(x, modl, g, w_router_t, b_router, seg_len, tm):
    t, d = x.shape
    n_e = w_router_t.shape[0]
    per_seg = seg_len // tm
    sel = pl.BlockSpec((TOP_K, tm), lambda i: (0, i))
    return pl.pallas_call(
        _router_kernel,
        grid=(t // tm,),
        in_specs=[pl.BlockSpec((tm, d), lambda i: (i, 0)),
                  pl.BlockSpec((1, SUBLANES, d), lambda i: (i // per_seg, 0, 0)),
                  pl.BlockSpec((1, d), lambda i: (0, 0)),
                  pl.BlockSpec((n_e, d), lambda i: (0, 0)),
                  pl.BlockSpec((n_e, 1), lambda i: (0, 0))],
        out_specs=[pl.BlockSpec((tm, d), lambda i: (i, 0)), sel, sel, sel,
                   pl.BlockSpec((n_e, LANES), lambda i: (0, 0))],
        out_shape=[jax.ShapeDtypeStruct((t, d), F32),
                   jax.ShapeDtypeStruct((TOP_K, t), jnp.int32),
                   jax.ShapeDtypeStruct((TOP_K, t), F32),
                   jax.ShapeDtypeStruct((TOP_K, t), jnp.int32),
                   jax.ShapeDtypeStruct((n_e, LANES), jnp.int32)],
        scratch_shapes=[pltpu.VMEM((n_e, 1), F32)],
        compiler_params=_cparams("arbitrary"),
        name="moe_router",
    )(x, modl, g, w_router_t, b_router)


def _dispatch_kernel(pad_from_ref, pad_n_ref, dest_ref, h_ref, xs_hbm, zrow_ref, sem, zsem):
    tm = h_ref.shape[0]

    def zero_row(r):
        return pltpu.make_async_copy(zrow_ref, xs_hbm.at[pl.ds(r, 1), :], zsem)

    @pl.when(pl.program_id(0) == 0)
    def _():
        zrow_ref[...] = jnp.zeros(zrow_ref.shape, F32)
        for e in range(pad_n_ref.shape[0]):
            def fill(r, carry, e=e):
                zero_row(pad_from_ref[e] + r).start()
                return carry
            lax.fori_loop(0, pad_n_ref[e], fill, 0)
        for e in range(pad_n_ref.shape[0]):
            def drain(r, carry):
                zero_row(0).wait()
                return carry
            lax.fori_loop(0, pad_n_ref[e], drain, 0)

    def issue(j, carry):
        for k in range(TOP_K):
            pltpu.make_async_copy(h_ref.at[pl.ds(j, 1), :],
                                  xs_hbm.at[pl.ds(dest_ref[0, 0, j * TOP_K + k], 1), :], sem).start()
        return carry

    lax.fori_loop(0, tm, issue, 0, unroll=DMA_ISSUE_UNROLL)
    for _ in range(TOP_K):
        pltpu.make_async_copy(h_ref, xs_hbm.at[pl.ds(0, tm), :], sem).wait()


def _dispatch(h, dest_tk, pad_from, pad_n, n_rows, tm):
    t, d = h.shape
    return pl.pallas_call(
        _dispatch_kernel,
        grid_spec=pltpu.PrefetchScalarGridSpec(
            num_scalar_prefetch=2, grid=(t // tm,),
            in_specs=[pl.BlockSpec((1, 1, tm * TOP_K), lambda i, pf, pn: (i, 0, 0), memory_space=pltpu.SMEM),
                      pl.BlockSpec((tm, d), lambda i, pf, pn: (i, 0))],
            out_specs=pl.BlockSpec(memory_space=pl.ANY),
            scratch_shapes=[pltpu.VMEM((1, d), F32), pltpu.SemaphoreType.DMA(()),
                            pltpu.SemaphoreType.DMA(())]),
        out_shape=jax.ShapeDtypeStruct((n_rows, d), F32),
        compiler_params=_cparams("arbitrary"),
        name="moe_dispatch",
    )(pad_from, pad_n, dest_tk.reshape(t // tm, 1, tm * TOP_K), h)


def _expert_kernel(be_ref, bi_ref, nu_ref, x_ref, wgu_ref, bgu_ref, wd_ref, bd_ref, y_ref,
                   wgu_bf_ref, wd_bf_ref):
    del bi_ref
    n_ff = wd_ref.shape[2]
    b = pl.program_id(0)

    @pl.when((b == 0) | (be_ref[b] != be_ref[jnp.maximum(b - 1, 0)]))
    def _():
        wgu_bf_ref[...] = wgu_ref[0, 0].astype(BF16)
        wd_bf_ref[...] = wd_ref[0, 0].astype(BF16)

    @pl.when(b < nu_ref[0])
    def _():
        x = x_ref[...].astype(BF16)
        acc = jnp.zeros(y_ref.shape, F32) + bd_ref[0]
        for c in range(n_ff // FF_CHUNK):
            lo = c * FF_CHUNK
            g_part = (jnp.dot(x, wgu_bf_ref[:, lo:lo + FF_CHUNK], preferred_element_type=F32)
                      + bgu_ref[0, :, lo:lo + FF_CHUNK])
            u_part = (jnp.dot(x, wgu_bf_ref[:, n_ff + lo:n_ff + lo + FF_CHUNK], preferred_element_type=F32)
                      + bgu_ref[0, :, n_ff + lo:n_ff + lo + FF_CHUNK])
            g_part = jnp.minimum(g_part, SWIGLU_LIMIT)
            u_part = jnp.clip(u_part, -SWIGLU_LIMIT, SWIGLU_LIMIT)
            act = (u_part + 1.0) * g_part * jax.nn.sigmoid(SWIGLU_ALPHA * g_part)
            acc = acc + jnp.dot(act.astype(BF16), wd_bf_ref[lo:lo + FF_CHUNK, :],
                                preferred_element_type=F32)
        y_ref[...] = acc


def _experts(xs, blk_expert, blk_idx, n_used, layer, w_gu, b_gu, w_down, b_down, tme):
    nr, d = xs.shape
    _, n_e, _, two_f = w_gu.shape
    n_ff = two_f // 2
    nblk = nr // tme
    rows = pl.BlockSpec((tme, d), lambda b, be, bi, nu: (bi[b], 0))
    b_gu = b_gu[layer]
    b_down = b_down[layer]
    return pl.pallas_call(
        _expert_kernel,
        grid_spec=pltpu.PrefetchScalarGridSpec(
            num_scalar_prefetch=3, grid=(nblk,),
            in_specs=[rows,
                      pl.BlockSpec((1, 1, d, two_f), lambda b, be, bi, nu: (layer, be[b], 0, 0)),
                      pl.BlockSpec((1, 1, two_f), lambda b, be, bi, nu: (be[b], 0, 0)),
                      pl.BlockSpec((1, 1, n_ff, d), lambda b, be, bi, nu: (layer, be[b], 0, 0)),
                      pl.BlockSpec((1, 1, d), lambda b, be, bi, nu: (be[b], 0, 0))],
            out_specs=rows,
            scratch_shapes=[pltpu.VMEM((d, two_f), BF16), pltpu.VMEM((n_ff, d), BF16)]),
        out_shape=jax.ShapeDtypeStruct(xs.shape, F32),
        compiler_params=_cparams("arbitrary"),
        name="moe_experts",
    )(blk_expert, blk_idx, n_used, xs, w_gu, b_gu.reshape(n_e, 1, two_f), w_down,
      b_down.reshape(n_e, 1, d))


def _combine_kernel(dest_ref, x_ref, gate_ref, mod_ref, g_ref, ys_hbm, o_ref, buf, sem, *, final_norm):
    tm = x_ref.shape[0]

    def issue(j, carry):
        for k in range(TOP_K):
            pltpu.make_async_copy(ys_hbm.at[pl.ds(dest_ref[0, 0, j * TOP_K + k], 1), :],
                                  buf.at[k, pl.ds(j, 1), :], sem).start()
        return carry

    lax.fori_loop(0, tm, issue, 0, unroll=DMA_ISSUE_UNROLL)
    for k in range(TOP_K):
        pltpu.make_async_copy(ys_hbm.at[pl.ds(0, tm), :], buf.at[k], sem).wait()
    gates = gate_ref[...]
    moe = gates[:, 0:1] * buf[0]
    for k in range(1, TOP_K):
        moe = moe + gates[:, k:k + 1] * buf[k]
    x = x_ref[...] + mod_ref[0, 5:6, :] * moe
    if final_norm:
        ms = jnp.mean(x * x, axis=-1, keepdims=True)
        x = x * lax.rsqrt(ms + NORM_EPS) * g_ref[...]
    o_ref[...] = x


def _combine(x, ys, dest_tk, gates, modl, final_g, seg_len, tm, final_norm, row0=0, rows=None):
    t, d = x.shape
    rows = t if rows is None else rows
    per_seg = seg_len // tm
    b0 = row0 // tm
    return pl.pallas_call(
        functools.partial(_combine_kernel, final_norm=final_norm),
        grid=(rows // tm,),
        in_specs=[pl.BlockSpec((1, 1, tm * TOP_K), lambda i: (i + b0, 0, 0), memory_space=pltpu.SMEM),
                  pl.BlockSpec((tm, d), lambda i: (i + b0, 0)),
                  pl.BlockSpec((tm, TOP_K), lambda i: (i + b0, 0)),
                  pl.BlockSpec((1, SUBLANES, d), lambda i: ((i + b0) // per_seg, 0, 0)),
                  pl.BlockSpec((1, d), lambda i: (0, 0)),
                  pl.BlockSpec(memory_space=pl.ANY)],
        out_specs=pl.BlockSpec((tm, d), lambda i: (i, 0)),
        out_shape=jax.ShapeDtypeStruct((rows, d), F32),
        scratch_shapes=[pltpu.VMEM((TOP_K, tm, d), F32), pltpu.SemaphoreType.DMA(())],
        compiler_params=_cparams("arbitrary"),
        name="moe_combine",
    )(dest_tk.reshape(t // tm, 1, tm * TOP_K), x, gates, modl, final_g, ys)


def _moe_dispatch_and_experts(x, modl, g, w_router, b_router, layer, w_gu, b_gu, w_down, b_down, seg_len, tm):
    t, d = x.shape
    n_e = w_router.shape[1]
    tme = math.gcd(t * TOP_K, EXPERT_ROW_TILE)
    h, idx_t, gate_t, rank_t, cnt = _router(x, modl, g, w_router.T, b_router.reshape(n_e, 1), seg_len, tm)
    counts = cnt[:, 0]
    padded = (counts + tme - 1) // tme * tme
    pad_end = jnp.cumsum(padded)
    pad_start = pad_end - padded
    nblk = (t * TOP_K) // tme + n_e
    n_used = (pad_end[-1] // tme).astype(jnp.int32).reshape(1)
    blk_idx = jnp.minimum(jnp.arange(nblk, dtype=jnp.int32), n_used[0] - 1)
    blk_expert = jnp.minimum(jnp.sum((pad_end[None, :] <= (blk_idx * tme)[:, None]).astype(jnp.int32), axis=1),
                             n_e - 1)
    start_of = jnp.sum(jnp.where(idx_t[None] == jnp.arange(n_e, dtype=jnp.int32)[:, None, None],
                                 pad_start[:, None, None], 0), axis=0)
    dest = (start_of + rank_t).astype(jnp.int32)
    dest_tk = dest.T.reshape(-1)
    xs = _dispatch(h, dest_tk, (pad_start + counts).astype(jnp.int32), (padded - counts).astype(jnp.int32),
                   nblk * tme, tm)
    ys = _experts(xs, blk_expert, blk_idx, n_used, layer, w_gu, b_gu, w_down, b_down, tme)
    return ys, dest_tk, gate_t.T


def kernel(x_prompt, x_sample, c_prompt, c_sample, norm1_g, norm2_g, final_g, w_ada, b_ada, w_qkv, w_o,
           lam_q1, lam_k1, lam_q2, lam_k2, subln_g, ssm_a_re, ssm_a_im, ssm_log_dt, ssm_b_re, ssm_b_im,
           ssm_c_re, ssm_c_im, ssm_d, ssm_w_glu, w_router, b_router, w_gu, b_gu, w_down, b_down):
    n_b, seg_len, d = x_prompt.shape
    n_bs, s_len, _ = x_sample.shape
    depth = w_ada.shape[0]
    assert s_len % seg_len == 0 and s_len // seg_len in (1, 2)
    per_sample = s_len // seg_len
    n_seg = n_b + n_bs * per_sample
    t = n_seg * seg_len
    tm = math.gcd(seg_len, TOKEN_TILE)

    seg_seq = list(range(n_b)) + [n_b + j for j in range(n_bs) for _ in range(per_sample)]
    seg_pos = [0] * n_b + [r for _ in range(n_bs) for r in range(per_sample)]
    seg_a = list(range(n_b)) + [n_b + j * per_sample for j in range(n_bs) for _ in range(per_sample)]
    seg_b = list(range(n_b)) + [n_b + j * per_sample + per_sample - 1 for j in range(n_bs) for _ in range(per_sample)]
    two = [0] * n_b + [int(per_sample == 2)] * (n_bs * per_sample)
    seq_first = [1] * n_b + [int(r == 0) for _ in range(n_bs) for r in range(per_sample)]
    seq_last = [1] * n_b + [int(r == per_sample - 1) for _ in range(n_bs) for r in range(per_sample)]
    per_seg = seg_len // tm
    pos_blk = jnp.asarray([seg_pos[s] * per_seg + r for s in range(n_seg) for r in range(per_seg)], jnp.int32)

    x = jnp.concatenate([x_prompt.reshape(n_b * seg_len, d), x_sample.reshape(n_bs * s_len, d)], axis=0)

    n_c = n_b + n_bs
    c_rows = -(-n_c // SUBLANES) * SUBLANES
    c_all = jnp.concatenate([c_prompt, c_sample, jnp.zeros((c_rows - n_c, d), F32)], axis=0)
    mod = _ada(c_all, w_ada, b_ada)
    mod = mod[:, jnp.asarray(seg_seq)].reshape(depth, n_seg, 6, d)
    mod = jnp.concatenate([mod, jnp.zeros((depth, n_seg, SUBLANES - 6, d), F32)], axis=2)

    inv_freq = ROPE_THETA ** (-jnp.arange(0, HEAD_DIM, 2, dtype=F32) / HEAD_DIM)
    ang = jnp.arange(s_len, dtype=F32)[:, None] * inv_freq[None, :]
    cos_t = jnp.tile(jnp.cos(ang), (1, 2 * LANES // HEAD_DIM))
    sin_t = jnp.tile(jnp.concatenate([-jnp.sin(ang), jnp.sin(ang)], axis=1), (1, LANES // HEAD_DIM))

    out_p = out_s = None
    for i in range(depth):
        modl = mod[i]
        g1 = norm1_g[i].reshape(1, d)
        g2 = norm2_g[i].reshape(1, d)
        j = i // 2
        if i % 2 == 0:
            lambda_init = 0.8 - 0.6 * math.exp(-0.3 * i)
            q, k, vt = _qkv(x, modl, g1, w_qkv[j], cos_t, sin_t, pos_blk, seg_len, tm)
            lam_pack = jnp.zeros((SUBLANES, 2 * HEAD_DIM), F32).at[0:4, 0:HEAD_DIM].set(
                jnp.stack([lam_q1[j], lam_k1[j], lam_q2[j], lam_k2[j]]).astype(F32))
            o = _flash(q, k, vt, lam_pack, subln_g[j].reshape(2 * HEAD_DIM, 1).astype(F32),
                       jnp.asarray(seg_a, jnp.int32), jnp.asarray(seg_b, jnp.int32),
                       jnp.asarray(two, jnp.int32), seg_len, lambda_init)
            x = _proj_res(o, w_o[j].astype(BF16), x, modl, seg_len, tm, gate_row=2)
        else:
            params = (ssm_a_re[j], ssm_a_im[j], ssm_log_dt[j], ssm_b_re[j], ssm_b_im[j], ssm_c_re[j],
                      ssm_c_im[j], ssm_d[j], ssm_w_glu[j])
            x = _s5_layer(x, modl, g1, params, seg_len, seq_first, seq_last, tm)
        ys, dest_tk, gates = _moe_dispatch_and_experts(x, modl, g2, w_router[i], b_router[i], i, w_gu, b_gu,
                                                       w_down, b_down, seg_len, tm)
        fg = final_g.reshape(1, d)
        if i == depth - 1:
            out_p = _combine(x, ys, dest_tk, gates, modl, fg, seg_len, tm, True, 0, n_b * seg_len)
            out_s = _combine(x, ys, dest_tk, gates, modl, fg, seg_len, tm, True, n_b * seg_len, n_bs * s_len)
        else:
            x = _combine(x, ys, dest_tk, gates, modl, fg, seg_len, tm, False)
    return out_p.reshape(n_b, seg_len, d), out_s.reshape(n_bs, s_len, d)
```

```python
import functools
import math

import jax
import jax.numpy as jnp
from jax import lax
from jax.experimental import pallas as pl
from jax.experimental.pallas import tpu as pltpu

F32 = jnp.float32
BF16 = jnp.bfloat16
HIGHEST = lax.Precision.HIGHEST

N_DIFF_HEADS = 8
HEAD_DIM = 64
ROPE_THETA = 10000.0
SUBLN_EPS = 1e-5
NORM_EPS = 1e-6
GROUP_CH = 16
STATE_DIM = 64
N_EXPERTS = 32
TOP_K = 4
SWIGLU_ALPHA = 1.702
SWIGLU_LIMIT = 7.0

LANES = 128
SUBLANES = 8
VMEM_LIMIT_BYTES = 56 * 1024 * 1024

TOKEN_TILE = 512
ATT_Q_TILE = 512
ATT_UNROLL = 8
VT_ONES_ROWS = 16
LOG2_E = math.log2(math.e)
SSM_CHUNK = 16
SSM_GROUP_TILE = 8
SSM_ROW_TILE = 256
SSM_SCAN_TILE = 64
EXPERT_ROW_TILE = 512
FF_CHUNK = 512
DMA_ISSUE_UNROLL = 8


def _cparams(*sem):
    return pltpu.CompilerParams(dimension_semantics=sem, vmem_limit_bytes=VMEM_LIMIT_BYTES)


def _rms_mod(x, g, scale, shift):
    ms = jnp.mean(x * x, axis=-1, keepdims=True)
    return x * lax.rsqrt(ms + NORM_EPS) * g * (1.0 + scale) + shift


def _ada_kernel(c_ref, w_ref, b_ref, o_ref):
    c = c_ref[...]
    cond = c * jax.nn.sigmoid(c)
    o_ref[0] = jnp.dot(cond, w_ref[0], precision=HIGHEST, preferred_element_type=F32) + b_ref[0]


def _ada(c_pad, w_ada, b_ada):
    depth, d, n6 = w_ada.shape
    rows = c_pad.shape[0]
    tn = 1536 if n6 % 1536 == 0 else n6
    return pl.pallas_call(
        _ada_kernel,
        grid=(depth, n6 // tn),
        in_specs=[pl.BlockSpec((rows, d), lambda i, j: (0, 0)),
                  pl.BlockSpec((1, d, tn), lambda i, j: (i, 0, j)),
                  pl.BlockSpec((1, 1, tn), lambda i, j: (i, 0, j))],
        out_specs=pl.BlockSpec((1, rows, tn), lambda i, j: (i, 0, j)),
        out_shape=jax.ShapeDtypeStruct((depth, rows, n6), F32),
        compiler_params=_cparams("arbitrary", "arbitrary"),
        name="ada_mod",
    )(c_pad, w_ada, b_ada.reshape(depth, 1, n6))


def _qkv_kernel(pos_ref, x_ref, mod_ref, g_ref, w_ref, wvt_ref, cos_ref, sin_ref, q_ref, k_ref, vt_ref):
    del pos_ref
    d = x_ref.shape[1]
    h = _rms_mod(x_ref[...], g_ref[...], mod_ref[0, 1:2, :], mod_ref[0, 0:1, :]).astype(BF16)
    qkv = jnp.dot(h, w_ref[...], preferred_element_type=F32)
    vt = lax.dot_general(wvt_ref[...], h, (((1,), (1,)), ((), ())), preferred_element_type=F32)
    ones = jnp.ones((VT_ONES_ROWS, vt.shape[1]), BF16)
    for hd in range(vt_ref.shape[0]):
        vt_ref[hd, 0, 0:LANES, :] = vt[hd * LANES:(hd + 1) * LANES, :].astype(BF16)
        vt_ref[hd, 0, LANES:LANES + VT_ONES_ROWS, :] = ones
    cos = cos_ref[...]
    sin = sin_ref[...]
    lane = lax.broadcasted_iota(jnp.int32, cos.shape, 1)
    first_half = (lane % HEAD_DIM) < (HEAD_DIM // 2)

    def rotary(t):
        partner = jnp.where(first_half,
                            pltpu.roll(t, LANES - HEAD_DIM // 2, 1),
                            pltpu.roll(t, HEAD_DIM // 2, 1))
        return t * cos + partner * sin

    for j in range(d // LANES):
        sl = slice(j * LANES, (j + 1) * LANES)
        q_ref[:, sl] = (rotary(qkv[:, sl]) * (HEAD_DIM ** -0.5 * LOG2_E)).astype(BF16)
        k_ref[:, sl] = rotary(qkv[:, d + j * LANES:d + (j + 1) * LANES]).astype(BF16)


def _qkv(x, modl, g, w_qkv, cos_t, sin_t, pos_blk, seg_len, tm):
    t, d = x.shape
    per_seg = seg_len // tm
    n_vh = d // LANES
    tok = lambda i, p: (i, 0)
    out = jax.ShapeDtypeStruct((t, d), BF16)
    w_qk = w_qkv[:, :2 * d].astype(BF16)
    w_vt = w_qkv[:, 2 * d:].T.astype(BF16)
    return pl.pallas_call(
        _qkv_kernel,
        grid_spec=pltpu.PrefetchScalarGridSpec(
            num_scalar_prefetch=1, grid=(t // tm,),
            in_specs=[pl.BlockSpec((tm, d), tok),
                      pl.BlockSpec((1, SUBLANES, d), lambda i, p: (i // per_seg, 0, 0)),
                      pl.BlockSpec((1, d), lambda i, p: (0, 0)),
                      pl.BlockSpec((d, 2 * d), lambda i, p: (0, 0)),
                      pl.BlockSpec((d, d), lambda i, p: (0, 0)),
                      pl.BlockSpec((tm, LANES), lambda i, p: (p[i], 0)),
                      pl.BlockSpec((tm, LANES), lambda i, p: (p[i], 0))],
            out_specs=[pl.BlockSpec((tm, d), tok), pl.BlockSpec((tm, d), tok),
                       pl.BlockSpec((n_vh, 1, LANES + VT_ONES_ROWS, tm), lambda i, p: (0, i, 0, 0))]),
        out_shape=[out, out, jax.ShapeDtypeStruct((n_vh, t // tm, LANES + VT_ONES_ROWS, tm), BF16)],
        compiler_params=_cparams("arbitrary"),
        name="qkv_rotary",
    )(pos_blk, x, modl, g, w_qk, w_vt, cos_t, sin_t)


def _flash_kernel(sa_ref, sb_ref, two_ref, q_ref, ka_ref, vta_ref, kb_ref, vtb_ref, lam_ref, g_ref,
                  o_ref, qt_ref, m_ref, acc_ref, *, tk, unroll, lambda_init):
    del sa_ref, sb_ref
    seg = pl.program_id(0)
    qt = q_ref[...].astype(F32).T
    chan = lax.broadcasted_iota(jnp.int32, qt.shape, 0)
    qt_ref[0] = jnp.where(chan < HEAD_DIM, qt, 0.0).astype(BF16)
    qt_ref[1] = jnp.where(chan >= HEAD_DIM, qt, 0.0).astype(BF16)
    m_ref[...] = jnp.full(m_ref.shape, -jnp.inf, F32)
    acc_ref[...] = jnp.zeros(acc_ref.shape, F32)

    def sweep(k_ref, vt_ref):
        units = [(tile, g) for tile in range(unroll) for g in range(2)]

        def scores(jj, u):
            tile, g = units[u]
            off = pl.multiple_of((jj * unroll + tile) * tk, tk)
            half = tk // 2
            return jnp.concatenate(
                [jnp.dot(k_ref[pl.ds(off + r * half, half), :], qt_ref[g], preferred_element_type=F32)
                 for r in range(2)], axis=0)

        def softmax(u, s):
            g = units[u][1]
            m_prev = m_ref[g]
            m_new = jnp.maximum(m_prev, jnp.max(s, axis=0, keepdims=True))
            p = jnp.exp2((s - m_new).astype(BF16))
            alpha = jnp.exp2(m_prev - m_new)
            m_ref[g] = m_new
            return p, alpha

        def values(jj, u, p, alpha):
            tile, g = units[u]
            acc_ref[g] = alpha * acc_ref[g] + jnp.dot(vt_ref[0, jj * unroll + tile], p,
                                                      preferred_element_type=F32)

        def body(jj, carry):
            s_next = scores(jj, 0)
            pending = None
            for u in range(len(units)):
                s_cur = s_next
                if u + 1 < len(units):
                    s_next = scores(jj, u + 1)
                if pending is not None:
                    values(jj, *pending)
                pending = (u,) + softmax(u, s_cur)
            values(jj, *pending)
            return carry
        lax.fori_loop(0, k_ref.shape[0] // (tk * unroll), body, 0)

    sweep(ka_ref, vta_ref)

    @pl.when(two_ref[seg] == 1)
    def _():
        sweep(kb_ref, vtb_ref)

    o1 = acc_ref[0, 0:LANES, :] / acc_ref[0, LANES:LANES + 1, :]
    o2 = acc_ref[1, 0:LANES, :] / acc_ref[1, LANES:LANES + 1, :]
    lam = (jnp.exp(jnp.sum(lam_ref[0:1, :] * lam_ref[1:2, :], axis=-1, keepdims=True))
           - jnp.exp(jnp.sum(lam_ref[2:3, :] * lam_ref[3:4, :], axis=-1, keepdims=True))
           + lambda_init)
    ot = o1 - lam * o2
    ms = jnp.mean(ot * ot, axis=0, keepdims=True)
    ot = ot * lax.rsqrt(ms + SUBLN_EPS) * g_ref[...] * (1.0 - lambda_init)
    o_ref[...] = ot.T.astype(BF16)


def _flash(q, k, vt, lam_pack, subln_g_col, seg_a, seg_b, two, seg_len, lambda_init):
    t, d = q.shape
    nseg = t // seg_len
    n_vh, _, v_rows, tk = vt.shape
    width = 2 * HEAD_DIM
    per_seg = seg_len // tk
    tq = min(ATT_Q_TILE, seg_len)
    nq = seg_len // tq
    kern = functools.partial(_flash_kernel, tk=tk, unroll=min(ATT_UNROLL, per_seg), lambda_init=lambda_init)
    k_a = pl.BlockSpec((seg_len, width), lambda s, h, i, sa, sb, tw: (sa[s], h))
    k_b = pl.BlockSpec((seg_len, width), lambda s, h, i, sa, sb, tw: (sb[s], h))
    vt_a = pl.BlockSpec((1, per_seg, v_rows, tk), lambda s, h, i, sa, sb, tw: (h, sa[s], 0, 0))
    vt_b = pl.BlockSpec((1, per_seg, v_rows, tk), lambda s, h, i, sa, sb, tw: (h, sb[s], 0, 0))
    qo = pl.BlockSpec((tq, width), lambda s, h, i, sa, sb, tw: (s * nq + i, h))
    return pl.pallas_call(
        kern,
        grid_spec=pltpu.PrefetchScalarGridSpec(
            num_scalar_prefetch=3, grid=(nseg, n_vh, nq),
            in_specs=[qo, k_a, vt_a, k_b, vt_b,
                      pl.BlockSpec((SUBLANES, width), lambda s, h, i, sa, sb, tw: (0, 0)),
                      pl.BlockSpec((width, 1), lambda s, h, i, sa, sb, tw: (0, 0))],
            out_specs=qo,
            scratch_shapes=[pltpu.VMEM((2, width, tq), BF16),
                            pltpu.VMEM((2, 1, tq), F32),
                            pltpu.VMEM((2, v_rows, tq), F32)]),
        out_shape=jax.ShapeDtypeStruct((t, d), BF16),
        compiler_params=_cparams("arbitrary", "arbitrary", "arbitrary"),
        name="diff_flash_attention",
    )(seg_a, seg_b, two, q, k, vt, k, vt, lam_pack, subln_g_col)


def _proj_res_kernel(a_ref, w_ref, x_ref, mod_ref, o_ref, *, gate_row):
    y = jnp.dot(a_ref[...], w_ref[...], preferred_element_type=F32)
    o_ref[...] = x_ref[...] + mod_ref[0, gate_row:gate_row + 1, :] * y


def _proj_res(a, w_bf, x, modl, seg_len, tm, gate_row):
    t, d = x.shape
    per_seg = seg_len // tm
    return pl.pallas_call(
        functools.partial(_proj_res_kernel, gate_row=gate_row),
        grid=(t // tm,),
        in_specs=[pl.BlockSpec((tm, a.shape[1]), lambda i: (i, 0)),
                  pl.BlockSpec(w_bf.shape, lambda i: (0, 0)),
                  pl.BlockSpec((tm, d), lambda i: (i, 0)),
                  pl.BlockSpec((1, SUBLANES, d), lambda i: (i // per_seg, 0, 0))],
        out_specs=pl.BlockSpec((tm, d), lambda i: (i, 0)),
        out_shape=jax.ShapeDtypeStruct((t, d), F32),
        compiler_params=_cparams("arbitrary"),
        name="proj_residual",
    )(a, w_bf, x, modl)


def _prenorm_kernel(x_ref, mod_ref, g_ref, o_ref):
    o_ref[...] = _rms_mod(x_ref[...], g_ref[...], mod_ref[0, 1:2, :], mod_ref[0, 0:1, :])


def _prenorm(x, modl, g, seg_len, tm):
    t, d = x.shape
    per_seg = seg_len // tm
    return pl.pallas_call(
        _prenorm_kernel,
        grid=(t // tm,),
        in_specs=[pl.BlockSpec((tm, d), lambda i: (i, 0)),
                  pl.BlockSpec((1, SUBLANES, d), lambda i: (i // per_seg, 0, 0)),
                  pl.BlockSpec((1, d), lambda i: (0, 0))],
        out_specs=pl.BlockSpec((tm, d), lambda i: (i, 0)),
        out_shape=jax.ShapeDtypeStruct((t, d), F32),
        compiler_params=_cparams("arbitrary"),
        name="s5_prenorm",
    )(x, modl, g)


def _s5_tables(a_re, a_im, log_dt, b_re, b_im, c_re, c_im):
    f32 = F32
    L = SSM_CHUNK
    ar, ai = a_re.astype(f32), a_im.astype(f32)
    n_g, n_p = ar.shape[1], ar.shape[2]
    n_h = b_re.shape[-1]
    dt = jnp.exp(log_dt.astype(f32))[..., None]
    mag = jnp.exp(ar * dt)
    lr, li = mag * jnp.cos(ai * dt), mag * jnp.sin(ai * dt)
    den = ar * ar + ai * ai
    fr = ((lr - 1.0) * ar + li * ai) / den
    fi = (li * ar - (lr - 1.0) * ai) / den
    br, bi = b_re.astype(f32), b_im.astype(f32)
    bbr = fr[..., None] * br - fi[..., None] * bi
    bbi = fr[..., None] * bi + fi[..., None] * br
    cr, ci = c_re.astype(f32), c_im.astype(f32)
    kk = jnp.arange(L + 1, dtype=f32)
    pmag = jnp.exp(ar[..., None] * dt[..., None] * kk)
    pang = ai[..., None] * dt[..., None] * kk
    pr, pi = pmag * jnp.cos(pang), pmag * jnp.sin(pang)
    er = cr[..., None] * pr[:, :, None] - ci[..., None] * pi[:, :, None]
    ei = cr[..., None] * pi[:, :, None] + ci[..., None] * pr[:, :, None]
    klag = (jnp.einsum('dgopk,dgpi->dgkoi', er[..., :L], bbr, precision=HIGHEST)
            - jnp.einsum('dgopk,dgpi->dgkoi', ei[..., :L], bbi, precision=HIGHEST))
    kf, kb = klag[0], klag[1]
    kfull = jnp.concatenate([kb[:, 1:][:, ::-1], (kf[:, 0] + kb[:, 0])[:, None], kf[:, 1:]], axis=1)
    s_idx = jnp.arange(L)[:, None]
    t_idx = jnp.arange(L)[None, :]
    w = kfull[:, t_idx - s_idx + L - 1]
    w = jnp.transpose(w, (0, 1, 4, 2, 3)).reshape(n_g, L * n_h, L * n_h)
    def in_to_state(d, pw_idx):
        pwr = jnp.take(pr[d], pw_idx, axis=-1)
        pwi = jnp.take(pi[d], pw_idx, axis=-1)
        re = pwr[..., None] * bbr[d][:, :, None, :] - pwi[..., None] * bbi[d][:, :, None, :]
        im = pwr[..., None] * bbi[d][:, :, None, :] + pwi[..., None] * bbr[d][:, :, None, :]
        m = jnp.concatenate([re, im], axis=1)
        return jnp.transpose(m, (0, 2, 3, 1)).reshape(n_g, L * n_h, 2 * n_p)
    pcat = jnp.concatenate([in_to_state(0, L - 1 - jnp.arange(L)), in_to_state(1, jnp.arange(L))], axis=-1)
    def state_to_out(d, pw_idx):
        e_r = jnp.take(er[d], pw_idx, axis=-1)
        e_i = jnp.take(ei[d], pw_idx, axis=-1)
        m = jnp.concatenate([e_r, -e_i], axis=2)
        return jnp.transpose(m, (0, 2, 3, 1)).reshape(n_g, 2 * n_p, L * n_h)
    qcat = jnp.concatenate([state_to_out(0, jnp.arange(L) + 1), state_to_out(1, L - jnp.arange(L))], axis=1)

    def scan_mult(d):
        a_r, a_i = pr[d][..., L], pi[d][..., L]
        m1 = jnp.concatenate([a_r, a_r], axis=-1)
        m2 = jnp.concatenate([-a_i, a_i], axis=-1)
        lay = lambda m: jnp.transpose(m.reshape(n_g // SSM_GROUP_TILE, SSM_GROUP_TILE, 2 * n_p),
                                      (1, 0, 2)).reshape(SSM_GROUP_TILE, -1)
        return lay(m1), lay(m2)
    af1, af2 = scan_mult(0)
    ab1, ab2 = scan_mult(1)
    gt = SSM_GROUP_TILE
    nb = n_g // gt
    n_c = pcat.shape[-1]
    same = jnp.eye(gt, dtype=bool)
    w6 = w.astype(BF16).reshape(nb, gt, L, n_h, L, n_h)
    wbd = jnp.where(same[None, :, None, None, None, :, None], w6[:, :, :, :, :, None, :], 0)
    wbd = wbd.transpose(0, 2, 1, 3, 4, 5, 6).reshape(nb, L * gt * n_h, L * gt * n_h)
    p5 = pcat.astype(BF16).reshape(nb, gt, L, n_h, n_c)
    pbd = jnp.where(same[None, :, None, None, :, None], p5[:, :, :, :, None, :], 0)
    pbd = pbd.transpose(0, 2, 1, 3, 4, 5).reshape(nb, L * gt * n_h, gt * n_c)
    q5 = qcat.astype(BF16).reshape(nb, gt, n_c, L, n_h)
    qbd = jnp.where(same[None, :, None, None, :, None], q5[:, :, :, :, None, :], 0)
    qbd = qbd.reshape(nb, gt * n_c, L * gt * n_h)
    return wbd, pbd, qbd, (af1, af2, ab1, ab2)


def _chunk_rows(h_ref, tc):
    return jnp.concatenate([h_ref[pl.ds(s, tc, stride=SSM_CHUNK), :] for s in range(SSM_CHUNK)],
                           axis=1).astype(BF16)


def _s5_states_kernel(h_ref, p_ref, sf_ref, sb_ref):
    tc, gt, half = sf_ref.shape
    st = jnp.dot(_chunk_rows(h_ref, tc), p_ref[0], preferred_element_type=F32)
    for r in range(gt):
        sf_ref[:, r, :] = st[:, r * 2 * half:r * 2 * half + half]
        sb_ref[:, r, :] = st[:, r * 2 * half + half:(r + 1) * 2 * half]


def _s5_states(h, pbd, tc):
    t, d = h.shape
    n_blk, kdim, ncol = pbd.shape
    gt = SSM_GROUP_TILE
    half = ncol // (2 * gt)
    nc = t // SSM_CHUNK
    out = jax.ShapeDtypeStruct((nc, gt, n_blk * half), F32)
    return pl.pallas_call(
        _s5_states_kernel,
        grid=(n_blk, nc // tc),
        in_specs=[pl.BlockSpec((tc * SSM_CHUNK, LANES), lambda j, c: (c, j)),
                  pl.BlockSpec((1, kdim, ncol), lambda j, c: (j, 0, 0))],
        out_specs=[pl.BlockSpec((tc, gt, half), lambda j, c: (c, 0, j))] * 2,
        out_shape=[out, out],
        compiler_params=_cparams("arbitrary", "arbitrary"),
        name="s5_chunk_states",
    )(h, pbd)


def _s5_scan_kernel(rf_ref, rb_ref, sf_ref, sb_ref, af1_ref, af2_ref, ab1_ref, ab2_ref,
                    xf_ref, xb_ref, cf_ref, cb_ref, *, tcs, nblk):
    i = pl.program_id(0)

    @pl.when(i == 0)
    def _():
        cf_ref[...] = jnp.zeros(cf_ref.shape, F32)
        cb_ref[...] = jnp.zeros(cb_ref.shape, F32)

    width = cf_ref.shape[-1]

    def swap(x):
        return jnp.concatenate(
            [pltpu.roll(x[:, j * LANES:(j + 1) * LANES], LANES // 2, 1) for j in range(width // LANES)],
            axis=1)

    af1, af2 = af1_ref[...], af2_ref[...]
    ab1, ab2 = ab1_ref[...], ab2_ref[...]

    def fwd(r, carry):
        carry = carry * (1 - rf_ref[i * tcs + r]).astype(F32)
        xf_ref[r] = carry
        return af1 * carry + af2 * swap(carry) + sf_ref[r]

    cf_ref[...] = lax.fori_loop(0, tcs, fwd, cf_ref[...])

    def bwd(rr, carry):
        r = tcs - 1 - rr
        carry = carry * (1 - rb_ref[(nblk - 1 - i) * tcs + r]).astype(F32)
        xb_ref[r] = carry
        return ab1 * carry + ab2 * swap(carry) + sb_ref[r]

    cb_ref[...] = lax.fori_loop(0, tcs, bwd, cb_ref[...])


def _s5_scan(sf, sb, mults, reset_f, reset_b, tcs):
    nc, gt, width = sf.shape
    nblk = nc // tcs
    fblk = pl.BlockSpec((tcs, gt, width), lambda i, a, b: (i, 0, 0))
    bblk = pl.BlockSpec((tcs, gt, width), lambda i, a, b: (nblk - 1 - i, 0, 0))
    tab = pl.BlockSpec((gt, width), lambda i, a, b: (0, 0))
    out = jax.ShapeDtypeStruct((nc, gt, width), F32)
    return pl.pallas_call(
        functools.partial(_s5_scan_kernel, tcs=tcs, nblk=nblk),
        grid_spec=pltpu.PrefetchScalarGridSpec(
            num_scalar_prefetch=2, grid=(nblk,),
            in_specs=[fblk, bblk, tab, tab, tab, tab],
            out_specs=[fblk, bblk],
            scratch_shapes=[pltpu.VMEM((gt, width), F32), pltpu.VMEM((gt, width), F32)]),
        out_shape=[out, out],
        compiler_params=_cparams("arbitrary"),
        name="s5_chunk_scan",
    )(reset_f, reset_b, sf, sb, *mults)


def _s5_out_kernel(h_ref, xf_ref, xb_ref, w_ref, q_ref, y_ref):
    tc, gt, _ = xf_ref.shape
    xcat = jnp.concatenate([ref[:, r, :] for r in range(gt) for ref in (xf_ref, xb_ref)],
                           axis=1).astype(BF16)
    y = (jnp.dot(_chunk_rows(h_ref, tc), w_ref[0], preferred_element_type=F32)
         + jnp.dot(xcat, q_ref[0], preferred_element_type=F32))
    for s in range(SSM_CHUNK):
        y_ref[pl.ds(s, tc, stride=SSM_CHUNK), :] = y[:, s * LANES:(s + 1) * LANES]


def _s5_out(h, xf, xb, wbd, qbd, tc):
    t, d = h.shape
    n_blk, kdim, ncol = wbd.shape
    gt = SSM_GROUP_TILE
    half = xf.shape[-1] // n_blk
    nc = t // SSM_CHUNK
    xblk = pl.BlockSpec((tc, gt, half), lambda j, c: (c, 0, j))
    tok = pl.BlockSpec((tc * SSM_CHUNK, LANES), lambda j, c: (c, j))
    wspec = lambda rows: pl.BlockSpec((1, rows, ncol), lambda j, c: (j, 0, 0), pipeline_mode=pl.Buffered(1))
    return pl.pallas_call(
        _s5_out_kernel,
        grid=(n_blk, nc // tc),
        in_specs=[tok, xblk, xblk, wspec(kdim), wspec(qbd.shape[1])],
        out_specs=tok,
        out_shape=jax.ShapeDtypeStruct((t, d), F32),
        compiler_params=_cparams("arbitrary", "arbitrary"),
        name="s5_chunk_outputs",
    )(h, xf, xb, wbd, qbd)


def _s5_glu_kernel(x_ref, y_ref, mod_ref, g_ref, dskip_ref, w_ref, o_ref):
    d = x_ref.shape[1]
    x = x_ref[...]
    h = _rms_mod(x, g_ref[...], mod_ref[0, 1:2, :], mod_ref[0, 0:1, :])
    z = jax.nn.gelu(y_ref[...] + dskip_ref[...] * h)
    vg = jnp.dot(z.astype(BF16), w_ref[...], preferred_element_type=F32)
    m = vg[:, :d] * jax.nn.sigmoid(vg[:, d:])
    o_ref[...] = x + mod_ref[0, 2:3, :] * m


def _s5_glu(x, y, modl, g, d_skip, w_bf, seg_len, tm):
    t, d = x.shape
    per_seg = seg_len // tm
    tok = pl.BlockSpec((tm, d), lambda i: (i, 0))
    vec = pl.BlockSpec((1, d), lambda i: (0, 0))
    return pl.pallas_call(
        _s5_glu_kernel,
        grid=(t // tm,),
        in_specs=[tok, tok, pl.BlockSpec((1, SUBLANES, d), lambda i: (i // per_seg, 0, 0)), vec, vec,
                  pl.BlockSpec((d, 2 * d), lambda i: (0, 0))],
        out_specs=tok,
        out_shape=jax.ShapeDtypeStruct((t, d), F32),
        compiler_params=_cparams("arbitrary"),
        name="s5_glu_residual",
    )(x, y, modl, g, d_skip, w_bf)


def _s5_layer(x, modl, g, params, seg_len, seq_first, seq_last, tm):
    a_re, a_im, log_dt, b_re, b_im, c_re, c_im, d_skip, w_glu = params
    t, d = x.shape
    L = SSM_CHUNK
    n_g = d // GROUP_CH
    nc = t // L
    wbd, pbd, qbd, mults = _s5_tables(a_re, a_im, log_dt, b_re, b_im, c_re, c_im)
    h = _prenorm(x, modl, g, seg_len, tm)
    tc = math.gcd(nc, SSM_ROW_TILE)
    sf, sb = _s5_states(h, pbd, tc)
    chunks_per_seg = seg_len // L
    reset_f = jnp.repeat(jnp.asarray(seq_first, jnp.int32), chunks_per_seg) * (
        jnp.tile(jnp.arange(chunks_per_seg) == 0, len(seq_first))).astype(jnp.int32)
    reset_b = jnp.repeat(jnp.asarray(seq_last, jnp.int32), chunks_per_seg) * (
        jnp.tile(jnp.arange(chunks_per_seg) == chunks_per_seg - 1, len(seq_last))).astype(jnp.int32)
    xf, xb = _s5_scan(sf, sb, mults, reset_f, reset_b, math.gcd(nc, SSM_SCAN_TILE))
    y = _s5_out(h, xf, xb, wbd, qbd, tc)
    return _s5_glu(x, y, modl, g, d_skip.reshape(1, d), w_glu.astype(BF16), seg_len, tm)


def _router_kernel(x_ref, mod_ref, g_ref, w_ref, b_ref, h_ref, idx_ref, gate_ref, rank_ref, cnt_ref,
                   carry_ref):
    tm = x_ref.shape[0]
    n_e = w_ref.shape[0]

    @pl.when(pl.program_id(0) == 0)
    def _():
        carry_ref[...] = jnp.zeros(carry_ref.shape, F32)

    h = _rms_mod(x_ref[...], g_ref[...], mod_ref[0, 4:5, :], mod_ref[0, 3:4, :])
    h_ref[...] = h
    logits = lax.dot_general(w_ref[...], h, (((1,), (1,)), ((), ())), precision=HIGHEST,
                             preferred_element_type=F32) + b_ref[...]
    e_iota = lax.broadcasted_iota(jnp.int32, (n_e, tm), 0)
    work = logits
    chosen = jnp.zeros((n_e, tm), F32)
    vals, idxs = [], []
    for _ in range(TOP_K):
        m = jnp.max(work, axis=0, keepdims=True)
        ix = jnp.min(jnp.where(work == m, e_iota, n_e), axis=0, keepdims=True)
        hit = e_iota == ix
        work = jnp.where(hit, -jnp.inf, work)
        chosen = jnp.where(hit, 1.0, chosen)
        vals.append(m)
        idxs.append(ix)
    v = jnp.concatenate(vals, axis=0)
    ex = jnp.exp(v - v[0:1])
    gate_ref[...] = ex / jnp.sum(ex, axis=0, keepdims=True)
    idx_ref[...] = jnp.concatenate(idxs, axis=0)
    row = lax.broadcasted_iota(jnp.int32, (tm, tm), 0)
    col = lax.broadcasted_iota(jnp.int32, (tm, tm), 1)
    before = (row < col).astype(BF16)
    cum = jnp.dot(chosen.astype(BF16), before, preferred_element_type=F32) + carry_ref[...]
    ranks = [jnp.sum(jnp.where(e_iota == ix, cum, 0.0), axis=0, keepdims=True) for ix in idxs]
    rank_ref[...] = jnp.concatenate(ranks, axis=0).astype(jnp.int32)
    carry_ref[...] = carry_ref[...] + jnp.sum(chosen, axis=1, keepdims=True)
    cnt_ref[...] = jnp.broadcast_to(carry_ref[...], cnt_ref.shape).astype(jnp.int32)


def _router(x, modl, g, w_router_t, b_router, seg_len, tm):
    t, d = x.shape
    n_e = w_router_t.shape[0]
    per_seg = seg_len // tm
    sel = pl.BlockSpec((TOP_K, tm), lambda i: (0, i))
    return pl.pallas_call(
        _router_kernel,
        grid=(t // tm,),
        in_specs=[pl.BlockSpec((tm, d), lambda i: (i, 0)),
                  pl.BlockSpec((1, SUBLANES, d), lambda i: (i // per_seg, 0, 0)),
                  pl.BlockSpec((1, d), lambda i: (0, 0)),
                  pl.BlockSpec((n_e, d), lambda i: (0, 0)),
                  pl.BlockSpec((n_e, 1), lambda i: (0, 0))],
        out_specs=[pl.BlockSpec((tm, d), lambda i: (i, 0)), sel, sel, sel,
                   pl.BlockSpec((n_e, LANES), lambda i: (0, 0))],
        out_shape=[jax.ShapeDtypeStruct((t, d), F32),
                   jax.ShapeDtypeStruct((TOP_K, t), jnp.int32),
                   jax.ShapeDtypeStruct((TOP_K, t), F32),
                   jax.ShapeDtypeStruct((TOP_K, t), jnp.int32),
                   jax.ShapeDtypeStruct((n_e, LANES), jnp.int32)],
        scratch_shapes=[pltpu.VMEM((n_e, 1), F32)],
        compiler_params=_cparams("arbitrary"),
        name="moe_router",
    )(x, modl, g, w_router_t, b_router)


def _dispatch_kernel(pad_from_ref, pad_n_ref, dest_ref, h_ref, xs_hbm, zrow_ref, sem, zsem):
    tm = h_ref.shape[0]

    def zero_row(r):
        return pltpu.make_async_copy(zrow_ref, xs_hbm.at[pl.ds(r, 1), :], zsem)

    @pl.when(pl.program_id(0) == 0)
    def _():
        zrow_ref[...] = jnp.zeros(zrow_ref.shape, F32)
        for e in range(pad_n_ref.shape[0]):
            def fill(r, carry, e=e):
                zero_row(pad_from_ref[e] + r).start()
                return carry
            lax.fori_loop(0, pad_n_ref[e], fill, 0)
        for e in range(pad_n_ref.shape[0]):
            def drain(r, carry):
                zero_row(0).wait()
                return carry
            lax.fori_loop(0, pad_n_ref[e], drain, 0)

    def issue(j, carry):
        for k in range(TOP_K):
            pltpu.make_async_copy(h_ref.at[pl.ds(j, 1), :],
                                  xs_hbm.at[pl.ds(dest_ref[0, 0, j * TOP_K + k], 1), :], sem).start()
        return carry

    lax.fori_loop(0, tm, issue, 0, unroll=DMA_ISSUE_UNROLL)
    for _ in range(TOP_K):
        pltpu.make_async_copy(h_ref, xs_hbm.at[pl.ds(0, tm), :], sem).wait()


def _dispatch(h, dest_tk, pad_from, pad_n, n_rows, tm):
    t, d = h.shape
    return pl.pallas_call(
        _dispatch_kernel,
        grid_spec=pltpu.PrefetchScalarGridSpec(
            num_scalar_prefetch=2, grid=(t // tm,),
            in_specs=[pl.BlockSpec((1, 1, tm * TOP_K), lambda i, pf, pn: (i, 0, 0), memory_space=pltpu.SMEM),
                      pl.BlockSpec((tm, d), lambda i, pf, pn: (i, 0))],
            out_specs=pl.BlockSpec(memory_space=pl.ANY),
            scratch_shapes=[pltpu.VMEM((1, d), F32), pltpu.SemaphoreType.DMA(()),
                            pltpu.SemaphoreType.DMA(())]),
        out_shape=jax.ShapeDtypeStruct((n_rows, d), F32),
        compiler_params=_cparams("arbitrary"),
        name="moe_dispatch",
    )(pad_from, pad_n, dest_tk.reshape(t // tm, 1, tm * TOP_K), h)


def _expert_kernel(be_ref, bi_ref, nu_ref, x_ref, wgu_ref, bgu_ref, wd_ref, bd_ref, y_ref,
                   wgu_bf_ref, wd_bf_ref):
    del bi_ref
    n_ff = wd_ref.shape[2]
    b = pl.program_id(0)

    @pl.when((b == 0) | (be_ref[b] != be_ref[jnp.maximum(b - 1, 0)]))
    def _():
        wgu_bf_ref[...] = wgu_ref[0, 0].astype(BF16)
        wd_bf_ref[...] = wd_ref[0, 0].astype(BF16)

    @pl.when(b < nu_ref[0])
    def _():
        x = x_ref[...].astype(BF16)
        acc = jnp.zeros(y_ref.shape, F32) + bd_ref[0]
        for c in range(n_ff // FF_CHUNK):
            lo = c * FF_CHUNK
            g_part = (jnp.dot(x, wgu_bf_ref[:, lo:lo + FF_CHUNK], preferred_element_type=F32)
                      + bgu_ref[0, :, lo:lo + FF_CHUNK])
            u_part = (jnp.dot(x, wgu_bf_ref[:, n_ff + lo:n_ff + lo + FF_CHUNK], preferred_element_type=F32)
                      + bgu_ref[0, :, n_ff + lo:n_ff + lo + FF_CHUNK])
            g_part = jnp.minimum(g_part, SWIGLU_LIMIT)
            u_part = jnp.clip(u_part, -SWIGLU_LIMIT, SWIGLU_LIMIT)
            act = (u_part + 1.0) * g_part * jax.nn.sigmoid(SWIGLU_ALPHA * g_part)
            acc = acc + jnp.dot(act.astype(BF16), wd_bf_ref[lo:lo + FF_CHUNK, :],
                                preferred_element_type=F32)
        y_ref[...] = acc


def _experts(xs, blk_expert, blk_idx, n_used, layer, w_gu, b_gu, w_down, b_down, tme):
    nr, d = xs.shape
    _, n_e, _, two_f = w_gu.shape
    n_ff = two_f // 2
    nblk = nr // tme
    rows = pl.BlockSpec((tme, d), lambda b, be, bi, nu: (bi[b], 0))
    b_gu = b_gu[layer]
    b_down = b_down[layer]
    return pl.pallas_call(
        _expert_kernel,
        grid_spec=pltpu.PrefetchScalarGridSpec(
            num_scalar_prefetch=3, grid=(nblk,),
            in_specs=[rows,
                      pl.BlockSpec((1, 1, d, two_f), lambda b, be, bi, nu: (layer, be[b], 0, 0)),
                      pl.BlockSpec((1, 1, two_f), lambda b, be, bi, nu: (be[b], 0, 0)),
                      pl.BlockSpec((1, 1, n_ff, d), lambda b, be, bi, nu: (layer, be[b], 0, 0)),
                      pl.BlockSpec((1, 1, d), lambda b, be, bi, nu: (be[b], 0, 0))],
            out_specs=rows,
            scratch_shapes=[pltpu.VMEM((d, two_f), BF16), pltpu.VMEM((n_ff, d), BF16)]),
        out_shape=jax.ShapeDtypeStruct(xs.shape, F32),
        compiler_params=_cparams("arbitrary"),
        name="moe_experts",
    )(blk_expert, blk_idx, n_used, xs, w_gu, b_gu.reshape(n_e, 1, two_f), w_down,
      b_down.reshape(n_e, 1, d))


def _combine_kernel(dest_ref, x_ref, gate_ref, mod_ref, g_ref, ys_hbm, o_ref, buf, sem, *, final_norm):
    tm = x_ref.shape[0]

    def issue(j, carry):
        for k in range(TOP_K):
            pltpu.make_async_copy(ys_hbm.at[pl.ds(dest_ref[0, 0, j * TOP_K + k], 1), :],
                                  buf.at[k, pl.ds(j, 1), :], sem).start()
        return carry

    lax.fori_loop(0, tm, issue, 0, unroll=DMA_ISSUE_UNROLL)
    for k in range(TOP_K):
        pltpu.make_async_copy(ys_hbm.at[pl.ds(0, tm), :], buf.at[k], sem).wait()
    gates = gate_ref[...]
    moe = gates[:, 0:1] * buf[0]
    for k in range(1, TOP_K):
        moe = moe + gates[:, k:k + 1] * buf[k]
    x = x_ref[...] + mod_ref[0, 5:6, :] * moe
    if final_norm:
        ms = jnp.mean(x * x, axis=-1, keepdims=True)
        x = x * lax.rsqrt(ms + NORM_EPS) * g_ref[...]
    o_ref[...] = x


def _combine(x, ys, dest_tk, gates, modl, final_g, seg_len, tm, final_norm, row0=0, rows=None):
    t, d = x.shape
    rows = t if rows is None else rows
    per_seg = seg_len // tm
    b0 = row0 // tm
    return pl.pallas_call(
        functools.partial(_combine_kernel, final_norm=final_norm),
        grid=(rows // tm,),
        in_specs=[pl.BlockSpec((1, 1, tm * TOP_K), lambda i: (i + b0, 0, 0), memory_space=pltpu.SMEM),
                  pl.BlockSpec((tm, d), lambda i: (i + b0, 0)),
                  pl.BlockSpec((tm, TOP_K), lambda i: (i + b0, 0)),
                  pl.BlockSpec((1, SUBLANES, d), lambda i: ((i + b0) // per_seg, 0, 0)),
                  pl.BlockSpec((1, d), lambda i: (0, 0)),
                  pl.BlockSpec(memory_space=pl.ANY)],
        out_specs=pl.BlockSpec((tm, d), lambda i: (i, 0)),
        out_shape=jax.ShapeDtypeStruct((rows, d), F32),
        scratch_shapes=[pltpu.VMEM((TOP_K, tm, d), F32), pltpu.SemaphoreType.DMA(())],
        compiler_params=_cparams("arbitrary"),
        name="moe_combine",
    )(dest_tk.reshape(t // tm, 1, tm * TOP_K), x, gates, modl, final_g, ys)


def _moe_dispatch_and_experts(x, modl, g, w_router, b_router, layer, w_gu, b_gu, w_down, b_down, seg_len, tm):
    t, d = x.shape
    n_e = w_router.shape[1]
    tme = math.gcd(t * TOP_K, EXPERT_ROW_TILE)
    h, idx_t, gate_t, rank_t, cnt = _router(x, modl, g, w_router.T, b_router.reshape(n_e, 1), seg_len, tm)
    counts = cnt[:, 0]
    padded = (counts + tme - 1) // tme * tme
    pad_end = jnp.cumsum(padded)
    pad_start = pad_end - padded
    nblk = (t * TOP_K) // tme + n_e
    n_used = (pad_end[-1] // tme).astype(jnp.int32).reshape(1)
    blk_idx = jnp.minimum(jnp.arange(nblk, dtype=jnp.int32), n_used[0] - 1)
    blk_expert = jnp.minimum(jnp.sum((pad_end[None, :] <= (blk_idx * tme)[:, None]).astype(jnp.int32), axis=1),
                             n_e - 1)
    start_of = jnp.sum(jnp.where(idx_t[None] == jnp.arange(n_e, dtype=jnp.int32)[:, None, None],
                                 pad_start[:, None, None], 0), axis=0)
    dest = (start_of + rank_t).astype(jnp.int32)
    dest_tk = dest.T.reshape(-1)
    xs = _dispatch(h, dest_tk, (pad_start + counts).astype(jnp.int32), (padded - counts).astype(jnp.int32),
                   nblk * tme, tm)
    ys = _experts(xs, blk_expert, blk_idx, n_used, layer, w_gu, b_gu, w_down, b_down, tme)
    return ys, dest_tk, gate_t.T


def kernel(x_prompt, x_sample, c_prompt, c_sample, norm1_g, norm2_g, final_g, w_ada, b_ada, w_qkv, w_o,
           lam_q1, lam_k1, lam_q2, lam_k2, subln_g, ssm_a_re, ssm_a_im, ssm_log_dt, ssm_b_re, ssm_b_im,
           ssm_c_re, ssm_c_im, ssm_d, ssm_w_glu, w_router, b_router, w_gu, b_gu, w_down, b_down):
    n_b, seg_len, d = x_prompt.shape
    n_bs, s_len, _ = x_sample.shape
    depth = w_ada.shape[0]
    assert s_len % seg_len == 0 and s_len // seg_len in (1, 2)
    per_sample = s_len // seg_len
    n_seg = n_b + n_bs * per_sample
    t = n_seg * seg_len
    tm = math.gcd(seg_len, TOKEN_TILE)

    seg_seq = list(range(n_b)) + [n_b + j for j in range(n_bs) for _ in range(per_sample)]
    seg_pos = [0] * n_b + [r for _ in range(n_bs) for r in range(per_sample)]
    seg_a = list(range(n_b)) + [n_b + j * per_sample for j in range(n_bs) for _ in range(per_sample)]
    seg_b = list(range(n_b)) + [n_b + j * per_sample + per_sample - 1 for j in range(n_bs) for _ in range(per_sample)]
    two = [0] * n_b + [int(per_sample == 2)] * (n_bs * per_sample)
    seq_first = [1] * n_b + [int(r == 0) for _ in range(n_bs) for r in range(per_sample)]
    seq_last = [1] * n_b + [int(r == per_sample - 1) for _ in range(n_bs) for r in range(per_sample)]
    per_seg = seg_len // tm
    pos_blk = jnp.asarray([seg_pos[s] * per_seg + r for s in range(n_seg) for r in range(per_seg)], jnp.int32)

    x = jnp.concatenate([x_prompt.reshape(n_b * seg_len, d), x_sample.reshape(n_bs * s_len, d)], axis=0)

    n_c = n_b + n_bs
    c_rows = -(-n_c // SUBLANES) * SUBLANES
    c_all = jnp.concatenate([c_prompt, c_sample, jnp.zeros((c_rows - n_c, d), F32)], axis=0)
    mod = _ada(c_all, w_ada, b_ada)
    mod = mod[:, jnp.asarray(seg_seq)].reshape(depth, n_seg, 6, d)
    mod = jnp.concatenate([mod, jnp.zeros((depth, n_seg, SUBLANES - 6, d), F32)], axis=2)

    inv_freq = ROPE_THETA ** (-jnp.arange(0, HEAD_DIM, 2, dtype=F32) / HEAD_DIM)
    ang = jnp.arange(s_len, dtype=F32)[:, None] * inv_freq[None, :]
    cos_t = jnp.tile(jnp.cos(ang), (1, 2 * LANES // HEAD_DIM))
    sin_t = jnp.tile(jnp.concatenate([-jnp.sin(ang), jnp.sin(ang)], axis=1), (1, LANES // HEAD_DIM))

    out_p = out_s = None
    for i in range(depth):
        modl = mod[i]
        g1 = norm1_g[i].reshape(1, d)
        g2 = norm2_g[i].reshape(1, d)
        j = i // 2
        if i % 2 == 0:
            lambda_init = 0.8 - 0.6 * math.exp(-0.3 * i)
            q, k, vt = _qkv(x, modl, g1, w_qkv[j], cos_t, sin_t, pos_blk, seg_len, tm)
            lam_pack = jnp.zeros((SUBLANES, 2 * HEAD_DIM), F32).at[0:4, 0:HEAD_DIM].set(
                jnp.stack([lam_q1[j], lam_k1[j], lam_q2[j], lam_k2[j]]).astype(F32))
            o = _flash(q, k, vt, lam_pack, subln_g[j].reshape(2 * HEAD_DIM, 1).astype(F32),
                       jnp.asarray(seg_a, jnp.int32), jnp.asarray(seg_b, jnp.int32),
                       jnp.asarray(two, jnp.int32), seg_len, lambda_init)
            x = _proj_res(o, w_o[j].astype(BF16), x, modl, seg_len, tm, gate_row=2)
        else:
            params = (ssm_a_re[j], ssm_a_im[j], ssm_log_dt[j], ssm_b_re[j], ssm_b_im[j], ssm_c_re[j],
                      ssm_c_im[j], ssm_d[j], ssm_w_glu[j])
            x = _s5_layer(x, modl, g1, params, seg_len, seq_first, seq_last, tm)
        ys, dest_tk, gates = _moe_dispatch_and_experts(x, modl, g2, w_router[i], b_router[i], i, w_gu, b_gu,
                                                       w_down, b_down, seg_len, tm)
        fg = final_g.reshape(1, d)
        if i == depth - 1:
            out_p = _combine(x, ys, dest_tk, gates, modl, fg, seg_len, tm, True, 0, n_b * seg_len)
            out_s = _combine(x, ys, dest_tk, gates, modl, fg, seg_len, tm, True, n_b * seg_len, n_bs * s_len)
        else:
            x = _combine(x, ys, dest_tk, gates, modl, fg, seg_len, tm, False)
    return out_p.reshape(n_b, seg_len, d), out_s.reshape(n_bs, s_len, d)
```

```python
import functools
import math

import jax
import jax.numpy as jnp
from jax import lax
from jax.experimental import pallas as pl
from jax.experimental.pallas import tpu as pltpu

F32 = jnp.float32
BF16 = jnp.bfloat16
HIGHEST = lax.Precision.HIGHEST

N_DIFF_HEADS = 8
HEAD_DIM = 64
ROPE_THETA = 10000.0
SUBLN_EPS = 1e-5
NORM_EPS = 1e-6
GROUP_CH = 16
STATE_DIM = 64
N_EXPERTS = 32
TOP_K = 4
SWIGLU_ALPHA = 1.702
SWIGLU_LIMIT = 7.0

LANES = 128
SUBLANES = 8
VMEM_LIMIT_BYTES = 56 * 1024 * 1024

TOKEN_TILE = 512
ATT_Q_TILE = 512
ATT_KV_TILE = 512
ATT_UNROLL = 8
VT_ONES_ROWS = 16
LOG2_E = math.log2(math.e)
SSM_CHUNK = 16
SSM_GROUP_TILE = 8
SSM_ROW_TILE = 256
SSM_SCAN_TILE = 64
EXPERT_ROW_TILE = 512
FF_CHUNK = 512
DMA_ISSUE_UNROLL = 8


def _cparams(*sem):
    return pltpu.CompilerParams(dimension_semantics=sem, vmem_limit_bytes=VMEM_LIMIT_BYTES)


def _rms_mod(x, g, scale, shift):
    ms = jnp.mean(x * x, axis=-1, keepdims=True)
    return x * lax.rsqrt(ms + NORM_EPS) * g * (1.0 + scale) + shift


def _ada_kernel(c_ref, w_ref, b_ref, o_ref):
    c = c_ref[...]
    cond = c * jax.nn.sigmoid(c)
    o_ref[0] = jnp.dot(cond, w_ref[0], precision=HIGHEST, preferred_element_type=F32) + b_ref[0]


def _ada(c_pad, w_ada, b_ada):
    depth, d, n6 = w_ada.shape
    rows = c_pad.shape[0]
    tn = 1536 if n6 % 1536 == 0 else n6
    return pl.pallas_call(
        _ada_kernel,
        grid=(depth, n6 // tn),
        in_specs=[pl.BlockSpec((rows, d), lambda i, j: (0, 0)),
                  pl.BlockSpec((1, d, tn), lambda i, j: (i, 0, j)),
                  pl.BlockSpec((1, 1, tn), lambda i, j: (i, 0, j))],
        out_specs=pl.BlockSpec((1, rows, tn), lambda i, j: (i, 0, j)),
        out_shape=jax.ShapeDtypeStruct((depth, rows, n6), F32),
        compiler_params=_cparams("arbitrary", "arbitrary"),
        name="ada_mod",
    )(c_pad, w_ada, b_ada.reshape(depth, 1, n6))


def _token_rows(xa_ref, xb_ref, n_a):
    return jnp.where(pl.program_id(0) < n_a, xa_ref[...], xb_ref[...])


def _token_specs(xa, xb, tm, index_args):
    n_a = xa.shape[0] // tm
    d = xa.shape[1]
    if index_args == 1:
        return n_a, [pl.BlockSpec((tm, d), lambda i: (jnp.minimum(i, n_a - 1), 0)),
                     pl.BlockSpec((tm, d), lambda i: (jnp.maximum(i - n_a, 0), 0))]
    return n_a, [pl.BlockSpec((tm, d), lambda i, p: (jnp.minimum(i, n_a - 1), 0)),
                 pl.BlockSpec((tm, d), lambda i, p: (jnp.maximum(i - n_a, 0), 0))]


def _qkv_kernel(pos_ref, xa_ref, xb_ref, mod_ref, g_ref, w_ref, wvt_ref, cos_ref, sin_ref, q_ref, k_ref,
                vt_ref, *, n_a):
    del pos_ref
    d = xa_ref.shape[1]
    h = _rms_mod(_token_rows(xa_ref, xb_ref, n_a), g_ref[...], mod_ref[0, 1:2, :],
                 mod_ref[0, 0:1, :]).astype(BF16)
    qkv = jnp.dot(h, w_ref[...], preferred_element_type=F32)
    vt = lax.dot_general(wvt_ref[...], h, (((1,), (1,)), ((), ())), preferred_element_type=F32)
    ones = jnp.ones((VT_ONES_ROWS, vt.shape[1]), BF16)
    for hd in range(vt_ref.shape[0]):
        vt_ref[hd, 0, 0:LANES, :] = vt[hd * LANES:(hd + 1) * LANES, :].astype(BF16)
        vt_ref[hd, 0, LANES:LANES + VT_ONES_ROWS, :] = ones
    cos = cos_ref[...]
    sin = sin_ref[...]
    lane = lax.broadcasted_iota(jnp.int32, cos.shape, 1)
    first_half = (lane % HEAD_DIM) < (HEAD_DIM // 2)

    def rotary(t):
        partner = jnp.where(first_half,
                            pltpu.roll(t, LANES - HEAD_DIM // 2, 1),
                            pltpu.roll(t, HEAD_DIM // 2, 1))
        return t * cos + partner * sin

    for j in range(d // LANES):
        sl = slice(j * LANES, (j + 1) * LANES)
        q_ref[:, sl] = (rotary(qkv[:, sl]) * (HEAD_DIM ** -0.5 * LOG2_E)).astype(BF16)
        k_ref[:, sl] = rotary(qkv[:, d + j * LANES:d + (j + 1) * LANES]).astype(BF16)


def _qkv(xa, xb, modl, g, w_qkv, cos_t, sin_t, pos_blk, seg_len, tm):
    d = xa.shape[1]
    t = xa.shape[0] + (0 if xb is None else xb.shape[0])
    xb = xa if xb is None else xb
    n_a, x_specs = _token_specs(xa, xb, tm, 2)
    per_seg = seg_len // tm
    n_vh = d // LANES
    tk = math.gcd(seg_len, ATT_KV_TILE)
    per_kv = tk // tm
    tok = lambda i, p: (i, 0)
    out = jax.ShapeDtypeStruct((t, d), BF16)
    w_qk = w_qkv[:, :2 * d].astype(BF16)
    w_vt = w_qkv[:, 2 * d:].T.astype(BF16)
    return pl.pallas_call(
        functools.partial(_qkv_kernel, n_a=n_a),
        grid_spec=pltpu.PrefetchScalarGridSpec(
            num_scalar_prefetch=1, grid=(t // tm,),
            in_specs=x_specs + [
                      pl.BlockSpec((1, SUBLANES, d), lambda i, p: (i // per_seg, 0, 0)),
                      pl.BlockSpec((1, d), lambda i, p: (0, 0)),
                      pl.BlockSpec((d, 2 * d), lambda i, p: (0, 0)),
                      pl.BlockSpec((d, d), lambda i, p: (0, 0)),
                      pl.BlockSpec((tm, LANES), lambda i, p: (p[i], 0)),
                      pl.BlockSpec((tm, LANES), lambda i, p: (p[i], 0))],
            out_specs=[pl.BlockSpec((tm, d), tok), pl.BlockSpec((tm, d), tok),
                       pl.BlockSpec((n_vh, 1, LANES + VT_ONES_ROWS, tm),
                                    lambda i, p: (0, i // per_kv, 0, i % per_kv))]),
        out_shape=[out, out, jax.ShapeDtypeStruct((n_vh, t // tk, LANES + VT_ONES_ROWS, tk), BF16)],
        compiler_params=_cparams("arbitrary"),
        name="qkv_rotary",
    )(pos_blk, xa, xb, modl, g, w_qk, w_vt, cos_t, sin_t)


def _flash_kernel(sa_ref, sb_ref, two_ref, q_ref, ka_ref, vta_ref, kb_ref, vtb_ref, lam_ref, g_ref,
                  o_ref, qt_ref, m_ref, acc_ref, *, tk, unroll, lambda_init):
    del sa_ref, sb_ref
    seg = pl.program_id(0)
    qt = q_ref[...].astype(F32).T
    chan = lax.broadcasted_iota(jnp.int32, qt.shape, 0)
    qt_ref[0] = jnp.where(chan < HEAD_DIM, qt, 0.0).astype(BF16)
    qt_ref[1] = jnp.where(chan >= HEAD_DIM, qt, 0.0).astype(BF16)
    m_ref[...] = jnp.full(m_ref.shape, -jnp.inf, F32)
    acc_ref[...] = jnp.zeros(acc_ref.shape, F32)

    def sweep(k_ref, vt_ref):
        units = [(tile, g) for tile in range(unroll) for g in range(2)]

        def scores(jj, u):
            tile, g = units[u]
            off = pl.multiple_of((jj * unroll + tile) * tk, tk)
            half = tk // 2
            return jnp.concatenate(
                [jnp.dot(k_ref[pl.ds(off + r * half, half), :], qt_ref[g], preferred_element_type=F32)
                 for r in range(2)], axis=0)

        def softmax(u, s):
            g = units[u][1]
            m_prev = m_ref[g]
            m_new = jnp.maximum(m_prev, jnp.max(s, axis=0, keepdims=True))
            p = jnp.exp2((s - m_new).astype(BF16))
            alpha = jnp.exp2(m_prev - m_new)
            m_ref[g] = m_new
            return p, alpha

        def values(jj, u, p, alpha):
            tile, g = units[u]
            acc_ref[g] = alpha * acc_ref[g] + jnp.dot(vt_ref[0, jj * unroll + tile], p,
                                                      preferred_element_type=F32)

        def body(jj, carry):
            s_next = scores(jj, 0)
            pending = None
            for u in range(len(units)):
                s_cur = s_next
                if u + 1 < len(units):
                    s_next = scores(jj, u + 1)
                if pending is not None:
                    values(jj, *pending)
                pending = (u,) + softmax(u, s_cur)
            values(jj, *pending)
            return carry
        lax.fori_loop(0, k_ref.shape[0] // (tk * unroll), body, 0)

    sweep(ka_ref, vta_ref)

    @pl.when(two_ref[seg] == 1)
    def _():
        sweep(kb_ref, vtb_ref)

    o1 = acc_ref[0, 0:LANES, :] / acc_ref[0, LANES:LANES + 1, :]
    o2 = acc_ref[1, 0:LANES, :] / acc_ref[1, LANES:LANES + 1, :]
    lam = (jnp.exp(jnp.sum(lam_ref[0:1, :] * lam_ref[1:2, :], axis=-1, keepdims=True))
           - jnp.exp(jnp.sum(lam_ref[2:3, :] * lam_ref[3:4, :], axis=-1, keepdims=True))
           + lambda_init)
    ot = o1 - lam * o2
    ms = jnp.mean(ot * ot, axis=0, keepdims=True)
    ot = ot * lax.rsqrt(ms + SUBLN_EPS) * g_ref[...] * (1.0 - lambda_init)
    o_ref[...] = ot.T.astype(BF16)


def _flash(q, k, vt, lam_pack, subln_g_col, seg_a, seg_b, two, seg_len, lambda_init):
    t, d = q.shape
    nseg = t // seg_len
    n_vh, _, v_rows, tk = vt.shape
    width = 2 * HEAD_DIM
    per_seg = seg_len // tk
    tq = min(ATT_Q_TILE, seg_len)
    nq = seg_len // tq
    kern = functools.partial(_flash_kernel, tk=tk, unroll=min(ATT_UNROLL, per_seg), lambda_init=lambda_init)
    k_a = pl.BlockSpec((seg_len, width), lambda s, h, i, sa, sb, tw: (sa[s], h))
    k_b = pl.BlockSpec((seg_len, width), lambda s, h, i, sa, sb, tw: (sb[s], h))
    vt_a = pl.BlockSpec((1, per_seg, v_rows, tk), lambda s, h, i, sa, sb, tw: (h, sa[s], 0, 0))
    vt_b = pl.BlockSpec((1, per_seg, v_rows, tk), lambda s, h, i, sa, sb, tw: (h, sb[s], 0, 0))
    qo = pl.BlockSpec((tq, width), lambda s, h, i, sa, sb, tw: (s * nq + i, h))
    return pl.pallas_call(
        kern,
        grid_spec=pltpu.PrefetchScalarGridSpec(
            num_scalar_prefetch=3, grid=(nseg, n_vh, nq),
            in_specs=[qo, k_a, vt_a, k_b, vt_b,
                      pl.BlockSpec((SUBLANES, width), lambda s, h, i, sa, sb, tw: (0, 0)),
                      pl.BlockSpec((width, 1), lambda s, h, i, sa, sb, tw: (0, 0))],
            out_specs=qo,
            scratch_shapes=[pltpu.VMEM((2, width, tq), BF16),
                            pltpu.VMEM((2, 1, tq), F32),
                            pltpu.VMEM((2, v_rows, tq), F32)]),
        out_shape=jax.ShapeDtypeStruct((t, d), BF16),
        compiler_params=_cparams("arbitrary", "arbitrary", "arbitrary"),
        name="diff_flash_attention",
    )(seg_a, seg_b, two, q, k, vt, k, vt, lam_pack, subln_g_col)


def _proj_res_kernel(a_ref, w_ref, xa_ref, xb_ref, mod_ref, o_ref, *, gate_row, n_a):
    y = jnp.dot(a_ref[...], w_ref[...], preferred_element_type=F32)
    o_ref[...] = _token_rows(xa_ref, xb_ref, n_a) + mod_ref[0, gate_row:gate_row + 1, :] * y


def _proj_res(a, w_bf, xa, xb, modl, seg_len, tm, gate_row):
    d = xa.shape[1]
    t = xa.shape[0] + (0 if xb is None else xb.shape[0])
    xb = xa if xb is None else xb
    n_a, x_specs = _token_specs(xa, xb, tm, 1)
    per_seg = seg_len // tm
    return pl.pallas_call(
        functools.partial(_proj_res_kernel, gate_row=gate_row, n_a=n_a),
        grid=(t // tm,),
        in_specs=[pl.BlockSpec((tm, a.shape[1]), lambda i: (i, 0)),
                  pl.BlockSpec(w_bf.shape, lambda i: (0, 0))] + x_specs + [
                  pl.BlockSpec((1, SUBLANES, d), lambda i: (i // per_seg, 0, 0))],
        out_specs=pl.BlockSpec((tm, d), lambda i: (i, 0)),
        out_shape=jax.ShapeDtypeStruct((t, d), F32),
        compiler_params=_cparams("arbitrary"),
        name="proj_residual",
    )(a, w_bf, xa, xb, modl)


def _prenorm_kernel(x_ref, mod_ref, g_ref, o_ref):
    o_ref[...] = _rms_mod(x_ref[...], g_ref[...], mod_ref[0, 1:2, :], mod_ref[0, 0:1, :])


def _prenorm(x, modl, g, seg_len, tm):
    t, d = x.shape
    per_seg = seg_len // tm
    return pl.pallas_call(
        _prenorm_kernel,
        grid=(t // tm,),
        in_specs=[pl.BlockSpec((tm, d), lambda i: (i, 0)),
                  pl.BlockSpec((1, SUBLANES, d), lambda i: (i // per_seg, 0, 0)),
                  pl.BlockSpec((1, d), lambda i: (0, 0))],
        out_specs=pl.BlockSpec((tm, d), lambda i: (i, 0)),
        out_shape=jax.ShapeDtypeStruct((t, d), F32),
        compiler_params=_cparams("arbitrary"),
        name="s5_prenorm",
    )(x, modl, g)


def _block_diag_kernel(a_ref, o_ref, *, rows_interleaved, cols_interleaved):
    gt, n_r, n_c = a_ref.shape
    gc = GROUP_CH
    o_ref[...] = jnp.zeros(o_ref.shape, o_ref.dtype)
    src = lax.broadcasted_iota(jnp.int32, (n_c, gt * n_c), 0)
    dst = lax.broadcasted_iota(jnp.int32, (n_c, gt * n_c), 1)
    for r in range(gt):
        a = a_ref[r]
        if cols_interleaved:
            place = (dst == (src // gc) * (gt * gc) + r * gc + src % gc).astype(a.dtype)
            a = jnp.dot(a, place, preferred_element_type=F32).astype(o_ref.dtype)
            cols = slice(0, gt * n_c)
        else:
            cols = slice(r * n_c, (r + 1) * n_c)
        if rows_interleaved:
            for s in range(n_r // gc):
                o_ref[0, (s * gt + r) * gc:(s * gt + r + 1) * gc, cols] = a[s * gc:(s + 1) * gc, :]
        else:
            o_ref[0, r * n_r:(r + 1) * n_r, cols] = a


def _block_diag(tab, rows_interleaved, cols_interleaved):
    n_g, n_r, n_c = tab.shape
    gt = SSM_GROUP_TILE
    return pl.pallas_call(
        functools.partial(_block_diag_kernel, rows_interleaved=rows_interleaved,
                          cols_interleaved=cols_interleaved),
        grid=(n_g // gt,),
        in_specs=[pl.BlockSpec((gt, n_r, n_c), lambda j: (j, 0, 0))],
        out_specs=pl.BlockSpec((1, gt * n_r, gt * n_c), lambda j: (j, 0, 0)),
        out_shape=jax.ShapeDtypeStruct((n_g // gt, gt * n_r, gt * n_c), tab.dtype),
        compiler_params=_cparams("arbitrary"),
        name="s5_block_diag",
    )(tab)


def _s5_tables(a_re, a_im, log_dt, b_re, b_im, c_re, c_im):
    f32 = F32
    L = SSM_CHUNK
    ar, ai = a_re.astype(f32), a_im.astype(f32)
    n_g, n_p = ar.shape[1], ar.shape[2]
    n_h = b_re.shape[-1]
    dt = jnp.exp(log_dt.astype(f32))[..., None]
    mag = jnp.exp(ar * dt)
    lr, li = mag * jnp.cos(ai * dt), mag * jnp.sin(ai * dt)
    den = ar * ar + ai * ai
    fr = ((lr - 1.0) * ar + li * ai) / den
    fi = (li * ar - (lr - 1.0) * ai) / den
    br, bi = b_re.astype(f32), b_im.astype(f32)
    bbr = fr[..., None] * br - fi[..., None] * bi
    bbi = fr[..., None] * bi + fi[..., None] * br
    cr, ci = c_re.astype(f32), c_im.astype(f32)
    kk = jnp.arange(L + 1, dtype=f32)
    pmag = jnp.exp(ar[..., None] * dt[..., None] * kk)
    pang = ai[..., None] * dt[..., None] * kk
    pr, pi = pmag * jnp.cos(pang), pmag * jnp.sin(pang)
    er = cr[..., None] * pr[:, :, None] - ci[..., None] * pi[:, :, None]
    ei = cr[..., None] * pi[:, :, None] + ci[..., None] * pr[:, :, None]
    klag = (jnp.einsum('dgopk,dgpi->dgkoi', er[..., :L], bbr, precision=HIGHEST)
            - jnp.einsum('dgopk,dgpi->dgkoi', ei[..., :L], bbi, precision=HIGHEST))
    kf, kb = klag[0], klag[1]
    kfull = jnp.concatenate([kb[:, 1:][:, ::-1], (kf[:, 0] + kb[:, 0])[:, None], kf[:, 1:]], axis=1)
    s_idx = jnp.arange(L)[:, None]
    t_idx = jnp.arange(L)[None, :]
    w = kfull[:, t_idx - s_idx + L - 1]
    w = jnp.transpose(w, (0, 1, 4, 2, 3)).reshape(n_g, L * n_h, L * n_h)
    def in_to_state(d, pw_idx):
        pwr = jnp.take(pr[d], pw_idx, axis=-1)
        pwi = jnp.take(pi[d], pw_idx, axis=-1)
        re = pwr[..., None] * bbr[d][:, :, None, :] - pwi[..., None] * bbi[d][:, :, None, :]
        im = pwr[..., None] * bbi[d][:, :, None, :] + pwi[..., None] * bbr[d][:, :, None, :]
        m = jnp.concatenate([re, im], axis=1)
        return jnp.transpose(m, (0, 2, 3, 1)).reshape(n_g, L * n_h, 2 * n_p)
    pcat = jnp.concatenate([in_to_state(0, L - 1 - jnp.arange(L)), in_to_state(1, jnp.arange(L))], axis=-1)
    def state_to_out(d, pw_idx):
        e_r = jnp.take(er[d], pw_idx, axis=-1)
        e_i = jnp.take(ei[d], pw_idx, axis=-1)
        m = jnp.concatenate([e_r, -e_i], axis=2)
        return jnp.transpose(m, (0, 2, 3, 1)).reshape(n_g, 2 * n_p, L * n_h)
    qcat = jnp.concatenate([state_to_out(0, jnp.arange(L) + 1), state_to_out(1, L - jnp.arange(L))], axis=1)

    def scan_mult(d):
        a_r, a_i = pr[d][..., L], pi[d][..., L]
        m1 = jnp.concatenate([a_r, a_r], axis=-1)
        m2 = jnp.concatenate([-a_i, a_i], axis=-1)
        lay = lambda m: jnp.transpose(m.reshape(n_g // SSM_GROUP_TILE, SSM_GROUP_TILE, 2 * n_p),
                                      (1, 0, 2)).reshape(SSM_GROUP_TILE, -1)
        return lay(m1), lay(m2)
    af1, af2 = scan_mult(0)
    ab1, ab2 = scan_mult(1)
    wbd = _block_diag(w.astype(BF16), True, True)
    pbd = _block_diag(pcat.astype(BF16), True, False)
    qbd = _block_diag(qcat.astype(BF16), False, True)
    return wbd, pbd, qbd, (af1, af2, ab1, ab2)


def _chunk_rows(h_ref, tc):
    return jnp.concatenate([h_ref[pl.ds(s, tc, stride=SSM_CHUNK), :] for s in range(SSM_CHUNK)],
                           axis=1).astype(BF16)


def _s5_states_kernel(h_ref, p_ref, sf_ref, sb_ref):
    tc, gt, half = sf_ref.shape
    st = jnp.dot(_chunk_rows(h_ref, tc), p_ref[0], preferred_element_type=F32)
    for r in range(gt):
        sf_ref[:, r, :] = st[:, r * 2 * half:r * 2 * half + half]
        sb_ref[:, r, :] = st[:, r * 2 * half + half:(r + 1) * 2 * half]


def _s5_states(h, pbd, tc):
    t, d = h.shape
    n_blk, kdim, ncol = pbd.shape
    gt = SSM_GROUP_TILE
    half = ncol // (2 * gt)
    nc = t // SSM_CHUNK
    out = jax.ShapeDtypeStruct((nc, gt, n_blk * half), F32)
    return pl.pallas_call(
        _s5_states_kernel,
        grid=(n_blk, nc // tc),
        in_specs=[pl.BlockSpec((tc * SSM_CHUNK, LANES), lambda j, c: (c, j)),
                  pl.BlockSpec((1, kdim, ncol), lambda j, c: (j, 0, 0))],
        out_specs=[pl.BlockSpec((tc, gt, half), lambda j, c: (c, 0, j))] * 2,
        out_shape=[out, out],
        compiler_params=_cparams("arbitrary", "arbitrary"),
        name="s5_chunk_states",
    )(h, pbd)


def _s5_scan_kernel(rf_ref, rb_ref, sf_ref, sb_ref, af1_ref, af2_ref, ab1_ref, ab2_ref,
                    xf_ref, xb_ref, cf_ref, cb_ref, *, tcs, nblk):
    i = pl.program_id(0)

    @pl.when(i == 0)
    def _():
        cf_ref[...] = jnp.zeros(cf_ref.shape, F32)
        cb_ref[...] = jnp.zeros(cb_ref.shape, F32)

    width = cf_ref.shape[-1]

    def swap(x):
        return jnp.concatenate(
            [pltpu.roll(x[:, j * LANES:(j + 1) * LANES], LANES // 2, 1) for j in range(width // LANES)],
            axis=1)

    af1, af2 = af1_ref[...], af2_ref[...]
    ab1, ab2 = ab1_ref[...], ab2_ref[...]

    def fwd(r, carry):
        carry = carry * (1 - rf_ref[i * tcs + r]).astype(F32)
        xf_ref[r] = carry
        return af1 * carry + af2 * swap(carry) + sf_ref[r]

    cf_ref[...] = lax.fori_loop(0, tcs, fwd, cf_ref[...])

    def bwd(rr, carry):
        r = tcs - 1 - rr
        carry = carry * (1 - rb_ref[(nblk - 1 - i) * tcs + r]).astype(F32)
        xb_ref[r] = carry
        return ab1 * carry + ab2 * swap(carry) + sb_ref[r]

    cb_ref[...] = lax.fori_loop(0, tcs, bwd, cb_ref[...])


def _s5_scan(sf, sb, mults, reset_f, reset_b, tcs):
    nc, gt, width = sf.shape
    nblk = nc // tcs
    fblk = pl.BlockSpec((tcs, gt, width), lambda i, a, b: (i, 0, 0))
    bblk = pl.BlockSpec((tcs, gt, width), lambda i, a, b: (nblk - 1 - i, 0, 0))
    tab = pl.BlockSpec((gt, width), lambda i, a, b: (0, 0))
    out = jax.ShapeDtypeStruct((nc, gt, width), F32)
    return pl.pallas_call(
        functools.partial(_s5_scan_kernel, tcs=tcs, nblk=nblk),
        grid_spec=pltpu.PrefetchScalarGridSpec(
            num_scalar_prefetch=2, grid=(nblk,),
            in_specs=[fblk, bblk, tab, tab, tab, tab],
            out_specs=[fblk, bblk],
            scratch_shapes=[pltpu.VMEM((gt, width), F32), pltpu.VMEM((gt, width), F32)]),
        out_shape=[out, out],
        compiler_params=_cparams("arbitrary"),
        name="s5_chunk_scan",
    )(reset_f, reset_b, sf, sb, *mults)


def _s5_out_kernel(h_ref, xf_ref, xb_ref, w_ref, q_ref, y_ref):
    tc, gt, _ = xf_ref.shape
    xcat = jnp.concatenate([ref[:, r, :] for r in range(gt) for ref in (xf_ref, xb_ref)],
                           axis=1).astype(BF16)
    y = (jnp.dot(_chunk_rows(h_ref, tc), w_ref[0], preferred_element_type=F32)
         + jnp.dot(xcat, q_ref[0], preferred_element_type=F32))
    for s in range(SSM_CHUNK):
        y_ref[pl.ds(s, tc, stride=SSM_CHUNK), :] = y[:, s * LANES:(s + 1) * LANES]


def _s5_out(h, xf, xb, wbd, qbd, tc):
    t, d = h.shape
    n_blk, kdim, ncol = wbd.shape
    gt = SSM_GROUP_TILE
    half = xf.shape[-1] // n_blk
    nc = t // SSM_CHUNK
    xblk = pl.BlockSpec((tc, gt, half), lambda j, c: (c, 0, j))
    tok = pl.BlockSpec((tc * SSM_CHUNK, LANES), lambda j, c: (c, j))
    wspec = lambda rows: pl.BlockSpec((1, rows, ncol), lambda j, c: (j, 0, 0), pipeline_mode=pl.Buffered(1))
    return pl.pallas_call(
        _s5_out_kernel,
        grid=(n_blk, nc // tc),
        in_specs=[tok, xblk, xblk, wspec(kdim), wspec(qbd.shape[1])],
        out_specs=tok,
        out_shape=jax.ShapeDtypeStruct((t, d), F32),
        compiler_params=_cparams("arbitrary", "arbitrary"),
        name="s5_chunk_outputs",
    )(h, xf, xb, wbd, qbd)


def _s5_glu_kernel(x_ref, y_ref, mod_ref, g_ref, dskip_ref, w_ref, o_ref):
    d = x_ref.shape[1]
    x = x_ref[...]
    h = _rms_mod(x, g_ref[...], mod_ref[0, 1:2, :], mod_ref[0, 0:1, :])
    z = jax.nn.gelu(y_ref[...] + dskip_ref[...] * h)
    vg = jnp.dot(z.astype(BF16), w_ref[...], preferred_element_type=F32)
    m = vg[:, :d] * jax.nn.sigmoid(vg[:, d:])
    o_ref[...] = x + mod_ref[0, 2:3, :] * m


def _s5_glu(x, y, modl, g, d_skip, w_bf, seg_len, tm):
    t, d = x.shape
    per_seg = seg_len // tm
    tok = pl.BlockSpec((tm, d), lambda i: (i, 0))
    vec = pl.BlockSpec((1, d), lambda i: (0, 0))
    return pl.pallas_call(
        _s5_glu_kernel,
        grid=(t // tm,),
        in_specs=[tok, tok, pl.BlockSpec((1, SUBLANES, d), lambda i: (i // per_seg, 0, 0)), vec, vec,
                  pl.BlockSpec((d, 2 * d), lambda i: (0, 0))],
        out_specs=tok,
        out_shape=jax.ShapeDtypeStruct((t, d), F32),
        compiler_params=_cparams("arbitrary"),
        name="s5_glu_residual",
    )(x, y, modl, g, d_skip, w_bf)


def _s5_layer(x, modl, g, params, seg_len, seq_first, seq_last, tm):
    a_re, a_im, log_dt, b_re, b_im, c_re, c_im, d_skip, w_glu = params
    t, d = x.shape
    L = SSM_CHUNK
    n_g = d // GROUP_CH
    nc = t // L
    wbd, pbd, qbd, mults = _s5_tables(a_re, a_im, log_dt, b_re, b_im, c_re, c_im)
    h = _prenorm(x, modl, g, seg_len, tm)
    tc = math.gcd(nc, SSM_ROW_TILE)
    sf, sb = _s5_states(h, pbd, tc)
    chunks_per_seg = seg_len // L
    reset_f = jnp.repeat(jnp.asarray(seq_first, jnp.int32), chunks_per_seg) * (
        jnp.tile(jnp.arange(chunks_per_seg) == 0, len(seq_first))).astype(jnp.int32)
    reset_b = jnp.repeat(jnp.asarray(seq_last, jnp.int32), chunks_per_seg) * (
        jnp.tile(jnp.arange(chunks_per_seg) == chunks_per_seg - 1, len(seq_last))).astype(jnp.int32)
    xf, xb = _s5_scan(sf, sb, mults, reset_f, reset_b, math.gcd(nc, SSM_SCAN_TILE))
    y = _s5_out(h, xf, xb, wbd, qbd, tc)
    return _s5_glu(x, y, modl, g, d_skip.reshape(1, d), w_glu.astype(BF16), seg_len, tm)


def _router_kernel(x_ref, mod_ref, g_ref, w_ref, b_ref, h_ref, idx_ref, gate_ref, rank_ref, cnt_ref,
                   carry_ref):
    tm = x_ref.shape[0]
    n_e = w_ref.shape[0]

    @pl.when(pl.program_id(0) == 0)
    def _():
        carry_ref[...] = jnp.zeros(carry_ref.shape, F32)

    h = _rms_mod(x_ref[...], g_ref[...], mod_ref[0, 4:5, :], mod_ref[0, 3:4, :])
    h_ref[...] = h
    logits = lax.dot_general(w_ref[...], h, (((1,), (1,)), ((), ())), precision=HIGHEST,
                             preferred_element_type=F32) + b_ref[...]
    e_iota = lax.broadcasted_iota(jnp.int32, (n_e, tm), 0)
    work = logits
    chosen = jnp.zeros((n_e, tm), F32)
    vals, idxs = [], []
    for _ in range(TOP_K):
        m = jnp.max(work, axis=0, keepdims=True)
        ix = jnp.min(jnp.where(work == m, e_iota, n_e), axis=0, keepdims=True)
        hit = e_iota == ix
        work = jnp.where(hit, -jnp.inf, work)
        chosen = jnp.where(hit, 1.0, chosen)
        vals.append(m)
        idxs.append(ix)
    v = jnp.concatenate(vals, axis=0)
    ex = jnp.exp(v - v[0:1])
    gate_ref[...] = ex / jnp.sum(ex, axis=0, keepdims=True)
    idx_ref[...] = jnp.concatenate(idxs, axis=0)
    row = lax.broadcasted_iota(jnp.int32, (tm, tm), 0)
    col = lax.broadcasted_iota(jnp.int32, (tm, tm), 1)
    before = (row < col).astype(BF16)
    cum = jnp.dot(chosen.astype(BF16), before, preferred_element_type=F32) + carry_ref[...]
    ranks = [jnp.sum(jnp.where(e_iota == ix, cum, 0.0), axis=0, keepdims=True) for ix in idxs]
    rank_ref[...] = jnp.concatenate(ranks, axis=0).astype(jnp.int32)
    carry_ref[...] = carry_ref[...] + jnp.sum(chosen, axis=1, keepdims=True)
    cnt_ref[...] = jnp.broadcast_to(carry_ref[...], cnt_ref.shape).astype(jnp.int32)


def _router(x, modl, g, w_router_t, b_router, seg_len, tm):
    t, d = x.shape
    n_e = w_router_t.shape[0]
    per_seg = seg_len // tm
    sel = pl.BlockSpec((TOP_K, tm), lambda i: (0, i))
    return pl.pallas_call(
        _router_kernel,
        grid=(t // tm,),
        in_specs=[pl.BlockSpec((tm, d), lambda i: (i, 0)),
                  pl.BlockSpec((1, SUBLANES, d), lambda i: (i // per_seg, 0, 0)),
                  pl.BlockSpec((1, d), lambda i: (0, 0)),
                  pl.BlockSpec((n_e, d), lambda i: (0, 0)),
                  pl.BlockSpec((n_e, 1), lambda i: (0, 0))],
        out_specs=[pl.BlockSpec((tm, d), lambda i: (i, 0)), sel, sel, sel,
                   pl.BlockSpec((n_e, LANES), lambda i: (0, 0))],
        out_shape=[jax.ShapeDtypeStruct((t, d), F32),
                   jax.ShapeDtypeStruct((TOP_K, t), jnp.int32),
                   jax.ShapeDtypeStruct((TOP_K, t), F32),
                   jax.ShapeDtypeStruct((TOP_K, t), jnp.int32),
                   jax.ShapeDtypeStruct((n_e, LANES), jnp.int32)],
        scratch_shapes=[pltpu.VMEM((n_e, 1), F32)],
        compiler_params=_cparams("arbitrary"),
        name="moe_router",
    )(x, modl, g, w_router_t, b_router)


def _dispatch_kernel(pad_from_ref, pad_n_ref, dest_ref, h_ref, xs_hbm, zrow_ref, sem, zsem):
    tm = h_ref.shape[0]

    def zero_row(r):
        return pltpu.make_async_copy(zrow_ref, xs_hbm.at[pl.ds(r, 1), :], zsem)

    @pl.when(pl.program_id(0) == 0)
    def _():
        zrow_ref[...] = jnp.zeros(zrow_ref.shape, F32)
        for e in range(pad_n_ref.shape[0]):
            def fill(r, carry, e=e):
                zero_row(pad_from_ref[e] + r).start()
                return carry
            lax.fori_loop(0, pad_n_ref[e], fill, 0)
        for e in range(pad_n_ref.shape[0]):
            def drain(r, carry):
                zero_row(0).wait()
                return carry
            lax.fori_loop(0, pad_n_ref[e], drain, 0)

    def issue(j, carry):
        for k in range(TOP_K):
            pltpu.make_async_copy(h_ref.at[pl.ds(j, 1), :],
                                  xs_hbm.at[pl.ds(dest_ref[0, 0, j * TOP_K + k], 1), :], sem).start()
        return carry

    lax.fori_loop(0, tm, issue, 0, unroll=DMA_ISSUE_UNROLL)
    for _ in range(TOP_K):
        pltpu.make_async_copy(h_ref, xs_hbm.at[pl.ds(0, tm), :], sem).wait()


def _dispatch(h, dest_tk, pad_from, pad_n, n_rows, tm):
    t, d = h.shape
    return pl.pallas_call(
        _dispatch_kernel,
        grid_spec=pltpu.PrefetchScalarGridSpec(
            num_scalar_prefetch=2, grid=(t // tm,),
            in_specs=[pl.BlockSpec((1, 1, tm * TOP_K), lambda i, pf, pn: (i, 0, 0), memory_space=pltpu.SMEM),
                      pl.BlockSpec((tm, d), lambda i, pf, pn: (i, 0))],
            out_specs=pl.BlockSpec(memory_space=pl.ANY),
            scratch_shapes=[pltpu.VMEM((1, d), F32), pltpu.SemaphoreType.DMA(()),
                            pltpu.SemaphoreType.DMA(())]),
        out_shape=jax.ShapeDtypeStruct((n_rows, d), F32),
        compiler_params=_cparams("arbitrary"),
        name="moe_dispatch",
    )(pad_from, pad_n, dest_tk.reshape(t // tm, 1, tm * TOP_K), h)


def _expert_kernel(be_ref, bi_ref, nu_ref, x_ref, wgu_ref, bgu_ref, wd_ref, bd_ref, y_ref,
                   wgu_bf_ref, wd_bf_ref):
    del bi_ref
    n_ff = wd_ref.shape[2]
    b = pl.program_id(0)

    @pl.when((b == 0) | (be_ref[b] != be_ref[jnp.maximum(b - 1, 0)]))
    def _():
        wgu_bf_ref[...] = wgu_ref[0, 0].astype(BF16)
        wd_bf_ref[...] = wd_ref[0, 0].astype(BF16)

    @pl.when(b < nu_ref[0])
    def _():
        x = x_ref[...].astype(BF16)
        acc = jnp.zeros(y_ref.shape, F32) + bd_ref[0]
        for c in range(n_ff // FF_CHUNK):
            lo = c * FF_CHUNK
            g_part = (jnp.dot(x, wgu_bf_ref[:, lo:lo + FF_CHUNK], preferred_element_type=F32)
                      + bgu_ref[0, :, lo:lo + FF_CHUNK])
            u_part = (jnp.dot(x, wgu_bf_ref[:, n_ff + lo:n_ff + lo + FF_CHUNK], preferred_element_type=F32)
                      + bgu_ref[0, :, n_ff + lo:n_ff + lo + FF_CHUNK])
            g_part = jnp.minimum(g_part, SWIGLU_LIMIT)
            u_part = jnp.clip(u_part, -SWIGLU_LIMIT, SWIGLU_LIMIT)
            act = (u_part + 1.0) * g_part * jax.nn.sigmoid(SWIGLU_ALPHA * g_part)
            acc = acc + jnp.dot(act.astype(BF16), wd_bf_ref[lo:lo + FF_CHUNK, :],
                                preferred_element_type=F32)
        y_ref[...] = acc


def _experts(xs, blk_expert, blk_idx, n_used, layer, w_gu, b_gu, w_down, b_down, tme):
    nr, d = xs.shape
    _, n_e, _, two_f = w_gu.shape
    n_ff = two_f // 2
    nblk = nr // tme
    rows = pl.BlockSpec((tme, d), lambda b, be, bi, nu: (bi[b], 0))
    b_gu = b_gu[layer]
    b_down = b_down[layer]
    return pl.pallas_call(
        _expert_kernel,
        grid_spec=pltpu.PrefetchScalarGridSpec(
            num_scalar_prefetch=3, grid=(nblk,),
            in_specs=[rows,
                      pl.BlockSpec((1, 1, d, two_f), lambda b, be, bi, nu: (layer, be[b], 0, 0)),
                      pl.BlockSpec((1, 1, two_f), lambda b, be, bi, nu: (be[b], 0, 0)),
                      pl.BlockSpec((1, 1, n_ff, d), lambda b, be, bi, nu: (layer, be[b], 0, 0)),
                      pl.BlockSpec((1, 1, d), lambda b, be, bi, nu: (be[b], 0, 0))],
            out_specs=rows,
            scratch_shapes=[pltpu.VMEM((d, two_f), BF16), pltpu.VMEM((n_ff, d), BF16)]),
        out_shape=jax.ShapeDtypeStruct(xs.shape, F32),
        compiler_params=_cparams("arbitrary"),
        name="moe_experts",
    )(blk_expert, blk_idx, n_used, xs, w_gu, b_gu.reshape(n_e, 1, two_f), w_down,
      b_down.reshape(n_e, 1, d))


def _combine_kernel(dest_ref, x_ref, gate_ref, mod_ref, g_ref, ys_hbm, o_ref, buf, sem, *, final_norm):
    tm = x_ref.shape[0]

    def issue(j, carry):
        for k in range(TOP_K):
            pltpu.make_async_copy(ys_hbm.at[pl.ds(dest_ref[0, 0, j * TOP_K + k], 1), :],
                                  buf.at[k, pl.ds(j, 1), :], sem).start()
        return carry

    lax.fori_loop(0, tm, issue, 0, unroll=DMA_ISSUE_UNROLL)
    for k in range(TOP_K):
        pltpu.make_async_copy(ys_hbm.at[pl.ds(0, tm), :], buf.at[k], sem).wait()
    gates = gate_ref[...]
    moe = gates[:, 0:1] * buf[0]
    for k in range(1, TOP_K):
        moe = moe + gates[:, k:k + 1] * buf[k]
    x = x_ref[...] + mod_ref[0, 5:6, :] * moe
    if final_norm:
        ms = jnp.mean(x * x, axis=-1, keepdims=True)
        x = x * lax.rsqrt(ms + NORM_EPS) * g_ref[...]
    o_ref[...] = x


def _combine(x, ys, dest_tk, gates, modl, final_g, seg_len, tm, final_norm, row0=0, rows=None):
    t, d = x.shape
    rows = t if rows is None else rows
    per_seg = seg_len // tm
    b0 = row0 // tm
    return pl.pallas_call(
        functools.partial(_combine_kernel, final_norm=final_norm),
        grid=(rows // tm,),
        in_specs=[pl.BlockSpec((1, 1, tm * TOP_K), lambda i: (i + b0, 0, 0), memory_space=pltpu.SMEM),
                  pl.BlockSpec((tm, d), lambda i: (i + b0, 0)),
                  pl.BlockSpec((tm, TOP_K), lambda i: (i + b0, 0)),
                  pl.BlockSpec((1, SUBLANES, d), lambda i: ((i + b0) // per_seg, 0, 0)),
                  pl.BlockSpec((1, d), lambda i: (0, 0)),
                  pl.BlockSpec(memory_space=pl.ANY)],
        out_specs=pl.BlockSpec((tm, d), lambda i: (i, 0)),
        out_shape=jax.ShapeDtypeStruct((rows, d), F32),
        scratch_shapes=[pltpu.VMEM((TOP_K, tm, d), F32), pltpu.SemaphoreType.DMA(())],
        compiler_params=_cparams("arbitrary"),
        name="moe_combine",
    )(dest_tk.reshape(t // tm, 1, tm * TOP_K), x, gates, modl, final_g, ys)


def _moe_dispatch_and_experts(x, modl, g, w_router, b_router, layer, w_gu, b_gu, w_down, b_down, seg_len, tm):
    t, d = x.shape
    n_e = w_router.shape[1]
    tme = math.gcd(t * TOP_K, EXPERT_ROW_TILE)
    h, idx_t, gate_t, rank_t, cnt = _router(x, modl, g, w_router.T, b_router.reshape(n_e, 1), seg_len, tm)
    counts = cnt[:, 0]
    padded = (counts + tme - 1) // tme * tme
    pad_end = jnp.cumsum(padded)
    pad_start = pad_end - padded
    nblk = (t * TOP_K) // tme + n_e
    n_used = (pad_end[-1] // tme).astype(jnp.int32).reshape(1)
    blk_idx = jnp.minimum(jnp.arange(nblk, dtype=jnp.int32), n_used[0] - 1)
    blk_expert = jnp.minimum(jnp.sum((pad_end[None, :] <= (blk_idx * tme)[:, None]).astype(jnp.int32), axis=1),
                             n_e - 1)
    start_of = jnp.sum(jnp.where(idx_t[None] == jnp.arange(n_e, dtype=jnp.int32)[:, None, None],
                                 pad_start[:, None, None], 0), axis=0)
    dest = (start_of + rank_t).astype(jnp.int32)
    dest_tk = dest.T.reshape(-1)
    xs = _dispatch(h, dest_tk, (pad_start + counts).astype(jnp.int32), (padded - counts).astype(jnp.int32),
                   nblk * tme, tm)
    ys = _experts(xs, blk_expert, blk_idx, n_used, layer, w_gu, b_gu, w_down, b_down, tme)
    return ys, dest_tk, gate_t.T


def kernel(x_prompt, x_sample, c_prompt, c_sample, norm1_g, norm2_g, final_g, w_ada, b_ada, w_qkv, w_o,
           lam_q1, lam_k1, lam_q2, lam_k2, subln_g, ssm_a_re, ssm_a_im, ssm_log_dt, ssm_b_re, ssm_b_im,
           ssm_c_re, ssm_c_im, ssm_d, ssm_w_glu, w_router, b_router, w_gu, b_gu, w_down, b_down):
    n_b, seg_len, d = x_prompt.shape
    n_bs, s_len, _ = x_sample.shape
    depth = w_ada.shape[0]
    assert s_len % seg_len == 0 and s_len // seg_len in (1, 2)
    per_sample = s_len // seg_len
    n_seg = n_b + n_bs * per_sample
    t = n_seg * seg_len
    tm = math.gcd(seg_len, TOKEN_TILE)

    seg_seq = list(range(n_b)) + [n_b + j for j in range(n_bs) for _ in range(per_sample)]
    seg_pos = [0] * n_b + [r for _ in range(n_bs) for r in range(per_sample)]
    seg_a = list(range(n_b)) + [n_b + j * per_sample for j in range(n_bs) for _ in range(per_sample)]
    seg_b = list(range(n_b)) + [n_b + j * per_sample + per_sample - 1 for j in range(n_bs) for _ in range(per_sample)]
    two = [0] * n_b + [int(per_sample == 2)] * (n_bs * per_sample)
    seq_first = [1] * n_b + [int(r == 0) for _ in range(n_bs) for r in range(per_sample)]
    seq_last = [1] * n_b + [int(r == per_sample - 1) for _ in range(n_bs) for r in range(per_sample)]
    per_seg = seg_len // tm
    pos_blk = jnp.asarray([seg_pos[s] * per_seg + r for s in range(n_seg) for r in range(per_seg)], jnp.int32)

    x = None
    x_parts = (x_prompt.reshape(n_b * seg_len, d), x_sample.reshape(n_bs * s_len, d))

    n_c = n_b + n_bs
    c_rows = -(-n_c // SUBLANES) * SUBLANES
    c_all = jnp.concatenate([c_prompt, c_sample, jnp.zeros((c_rows - n_c, d), F32)], axis=0)
    mod = _ada(c_all, w_ada, b_ada)
    mod = mod[:, jnp.asarray(seg_seq)].reshape(depth, n_seg, 6, d)
    mod = jnp.concatenate([mod, jnp.zeros((depth, n_seg, SUBLANES - 6, d), F32)], axis=2)

    inv_freq = ROPE_THETA ** (-jnp.arange(0, HEAD_DIM, 2, dtype=F32) / HEAD_DIM)
    ang = jnp.arange(s_len, dtype=F32)[:, None] * inv_freq[None, :]
    cos_t = jnp.tile(jnp.cos(ang), (1, 2 * LANES // HEAD_DIM))
    sin_t = jnp.tile(jnp.concatenate([-jnp.sin(ang), jnp.sin(ang)], axis=1), (1, LANES // HEAD_DIM))

    out_p = out_s = None
    for i in range(depth):
        modl = mod[i]
        g1 = norm1_g[i].reshape(1, d)
        g2 = norm2_g[i].reshape(1, d)
        j = i // 2
        if i % 2 == 0:
            lambda_init = 0.8 - 0.6 * math.exp(-0.3 * i)
            xa, xb = x_parts if x is None else (x, None)
            q, k, vt = _qkv(xa, xb, modl, g1, w_qkv[j], cos_t, sin_t, pos_blk, seg_len, tm)
            lam_pack = jnp.zeros((SUBLANES, 2 * HEAD_DIM), F32).at[0:4, 0:HEAD_DIM].set(
                jnp.stack([lam_q1[j], lam_k1[j], lam_q2[j], lam_k2[j]]).astype(F32))
            o = _flash(q, k, vt, lam_pack, subln_g[j].reshape(2 * HEAD_DIM, 1).astype(F32),
                       jnp.asarray(seg_a, jnp.int32), jnp.asarray(seg_b, jnp.int32),
                       jnp.asarray(two, jnp.int32), seg_len, lambda_init)
            x = _proj_res(o, w_o[j].astype(BF16), xa, xb, modl, seg_len, tm, gate_row=2)
        else:
            if x is None:
                x = jnp.concatenate(x_parts, axis=0)
            params = (ssm_a_re[j], ssm_a_im[j], ssm_log_dt[j], ssm_b_re[j], ssm_b_im[j], ssm_c_re[j],
                      ssm_c_im[j], ssm_d[j], ssm_w_glu[j])
            x = _s5_layer(x, modl, g1, params, seg_len, seq_first, seq_last, tm)
        ys, dest_tk, gates = _moe_dispatch_and_experts(x, modl, g2, w_router[i], b_router[i], i, w_gu, b_gu,
                                                       w_down, b_down, seg_len, tm)
        fg = final_g.reshape(1, d)
        if i == depth - 1:
            out_p = _combine(x, ys, dest_tk, gates, modl, fg, seg_len, tm, True, 0, n_b * seg_len)
            out_s = _combine(x, ys, dest_tk, gates, modl, fg, seg_len, tm, True, n_b * seg_len, n_bs * s_len)
        else:
            x = _combine(x, ys, dest_tk, gates, modl, fg, seg_len, tm, False)
    return out_p.reshape(n_b, seg_len, d), out_s.reshape(n_bs, s_len, d)
```

```python
import functools
import math

import jax
import jax.numpy as jnp
from jax import lax
from jax.experimental import pallas as pl
from jax.experimental.pallas import tpu as pltpu

F32 = jnp.float32
BF16 = jnp.bfloat16
HIGHEST = lax.Precision.HIGHEST

N_DIFF_HEADS = 8
HEAD_DIM = 64
ROPE_THETA = 10000.0
SUBLN_EPS = 1e-5
NORM_EPS = 1e-6
GROUP_CH = 16
STATE_DIM = 64
N_EXPERTS = 32
TOP_K = 4
SWIGLU_ALPHA = 1.702
SWIGLU_LIMIT = 7.0

LANES = 128
SUBLANES = 8
VMEM_LIMIT_BYTES = 56 * 1024 * 1024

TOKEN_TILE = 512
ATT_Q_TILE = 512
ATT_KV_TILE = 512
ATT_UNROLL = 8
VT_ONES_ROWS = 16
LOG2_E = math.log2(math.e)
SSM_CHUNK = 16
SSM_GROUP_TILE = 8
SSM_ROW_TILE = 256
SSM_SCAN_TILE = 64
EXPERT_ROW_TILE = 512
FF_CHUNK = 512
DMA_ISSUE_UNROLL = 8


def _cparams(*sem):
    return pltpu.CompilerParams(dimension_semantics=sem, vmem_limit_bytes=VMEM_LIMIT_BYTES)


def _rms_mod(x, g, scale, shift):
    ms = jnp.mean(x * x, axis=-1, keepdims=True)
    return x * lax.rsqrt(ms + NORM_EPS) * g * (1.0 + scale) + shift


def _ada_kernel(c_ref, w_ref, b_ref, o_ref):
    c = c_ref[...]
    cond = c * jax.nn.sigmoid(c)
    o_ref[0] = jnp.dot(cond, w_ref[0], precision=HIGHEST, preferred_element_type=F32) + b_ref[0]


def _ada(c_pad, w_ada, b_ada):
    depth, d, n6 = w_ada.shape
    rows = c_pad.shape[0]
    tn = 1536 if n6 % 1536 == 0 else n6
    return pl.pallas_call(
        _ada_kernel,
        grid=(depth, n6 // tn),
        in_specs=[pl.BlockSpec((rows, d), lambda i, j: (0, 0)),
                  pl.BlockSpec((1, d, tn), lambda i, j: (i, 0, j)),
                  pl.BlockSpec((1, 1, tn), lambda i, j: (i, 0, j))],
        out_specs=pl.BlockSpec((1, rows, tn), lambda i, j: (i, 0, j)),
        out_shape=jax.ShapeDtypeStruct((depth, rows, n6), F32),
        compiler_params=_cparams("arbitrary", "arbitrary"),
        name="ada_mod",
    )(c_pad, w_ada, b_ada.reshape(depth, 1, n6))


def _token_rows(xa_ref, xb_ref, n_a):
    return jnp.where(pl.program_id(0) < n_a, xa_ref[...], xb_ref[...])


def _token_specs(xa, xb, tm, index_args):
    n_a = xa.shape[0] // tm
    d = xa.shape[1]
    if index_args == 1:
        return n_a, [pl.BlockSpec((tm, d), lambda i: (jnp.minimum(i, n_a - 1), 0)),
                     pl.BlockSpec((tm, d), lambda i: (jnp.maximum(i - n_a, 0), 0))]
    return n_a, [pl.BlockSpec((tm, d), lambda i, p: (jnp.minimum(i, n_a - 1), 0)),
                 pl.BlockSpec((tm, d), lambda i, p: (jnp.maximum(i - n_a, 0), 0))]


def _qkv_kernel(pos_ref, xa_ref, xb_ref, mod_ref, g_ref, w_ref, wvt_ref, cos_ref, sin_ref, q_ref, k_ref,
                vt_ref, *, n_a):
    del pos_ref
    d = xa_ref.shape[1]
    h = _rms_mod(_token_rows(xa_ref, xb_ref, n_a), g_ref[...], mod_ref[0, 1:2, :],
                 mod_ref[0, 0:1, :]).astype(BF16)
    qkv = jnp.dot(h, w_ref[...], preferred_element_type=F32)
    vt = lax.dot_general(wvt_ref[...], h, (((1,), (1,)), ((), ())), preferred_element_type=F32)
    ones = jnp.ones((VT_ONES_ROWS, vt.shape[1]), BF16)
    for hd in range(vt_ref.shape[0]):
        vt_ref[hd, 0, 0:LANES, :] = vt[hd * LANES:(hd + 1) * LANES, :].astype(BF16)
        vt_ref[hd, 0, LANES:LANES + VT_ONES_ROWS, :] = ones
    cos = cos_ref[...]
    sin = sin_ref[...]
    lane = lax.broadcasted_iota(jnp.int32, cos.shape, 1)
    first_half = (lane % HEAD_DIM) < (HEAD_DIM // 2)

    def rotary(t):
        partner = jnp.where(first_half,
                            pltpu.roll(t, LANES - HEAD_DIM // 2, 1),
                            pltpu.roll(t, HEAD_DIM // 2, 1))
        return t * cos + partner * sin

    for j in range(d // LANES):
        sl = slice(j * LANES, (j + 1) * LANES)
        q_ref[:, sl] = (rotary(qkv[:, sl]) * (HEAD_DIM ** -0.5 * LOG2_E)).astype(BF16)
        k_ref[:, sl] = rotary(qkv[:, d + j * LANES:d + (j + 1) * LANES]).astype(BF16)


def _qkv(xa, xb, modl, g, w_qkv, cos_t, sin_t, pos_blk, seg_len, tm):
    d = xa.shape[1]
    t = xa.shape[0] + (0 if xb is None else xb.shape[0])
    xb = xa if xb is None else xb
    n_a, x_specs = _token_specs(xa, xb, tm, 2)
    per_seg = seg_len // tm
    n_vh = d // LANES
    tk = math.gcd(seg_len, ATT_KV_TILE)
    per_kv = tk // tm
    tok = lambda i, p: (i, 0)
    out = jax.ShapeDtypeStruct((t, d), BF16)
    w_qk = w_qkv[:, :2 * d].astype(BF16)
    w_vt = w_qkv[:, 2 * d:].T.astype(BF16)
    return pl.pallas_call(
        functools.partial(_qkv_kernel, n_a=n_a),
        grid_spec=pltpu.PrefetchScalarGridSpec(
            num_scalar_prefetch=1, grid=(t // tm,),
            in_specs=x_specs + [
                      pl.BlockSpec((1, SUBLANES, d), lambda i, p: (i // per_seg, 0, 0)),
                      pl.BlockSpec((1, d), lambda i, p: (0, 0)),
                      pl.BlockSpec((d, 2 * d), lambda i, p: (0, 0)),
                      pl.BlockSpec((d, d), lambda i, p: (0, 0)),
                      pl.BlockSpec((tm, LANES), lambda i, p: (p[i], 0)),
                      pl.BlockSpec((tm, LANES), lambda i, p: (p[i], 0))],
            out_specs=[pl.BlockSpec((tm, d), tok), pl.BlockSpec((tm, d), tok),
                       pl.BlockSpec((n_vh, 1, LANES + VT_ONES_ROWS, tm),
                                    lambda i, p: (0, i // per_kv, 0, i % per_kv))]),
        out_shape=[out, out, jax.ShapeDtypeStruct((n_vh, t // tk, LANES + VT_ONES_ROWS, tk), BF16)],
        compiler_params=_cparams("arbitrary"),
        name="qkv_rotary",
    )(pos_blk, xa, xb, modl, g, w_qk, w_vt, cos_t, sin_t)


def _flash_kernel(sa_ref, sb_ref, two_ref, q_ref, ka_ref, vta_ref, kb_ref, vtb_ref, lam_ref, g_ref,
                  o_ref, qt_ref, m_ref, acc_ref, *, tk, unroll, lambda_init):
    del sa_ref, sb_ref
    seg = pl.program_id(0)
    qt = q_ref[...].astype(F32).T
    chan = lax.broadcasted_iota(jnp.int32, qt.shape, 0)
    qt_ref[0] = jnp.where(chan < HEAD_DIM, qt, 0.0).astype(BF16)
    qt_ref[1] = jnp.where(chan >= HEAD_DIM, qt, 0.0).astype(BF16)
    m_ref[...] = jnp.full(m_ref.shape, -jnp.inf, F32)
    acc_ref[...] = jnp.zeros(acc_ref.shape, F32)

    def sweep(k_ref, vt_ref):
        units = [(tile, g) for tile in range(unroll) for g in range(2)]

        half = tk // 2

        def half_scores(jj, u, r):
            tile, g = units[u]
            off = pl.multiple_of((jj * unroll + tile) * tk, tk)
            return jnp.dot(k_ref[pl.ds(off + r * half, half), :], qt_ref[g], preferred_element_type=F32)

        def values(jj, u, p, alpha):
            tile, g = units[u]
            acc_ref[g] = alpha * acc_ref[g] + jnp.dot(vt_ref[0, jj * unroll + tile], p,
                                                      preferred_element_type=F32)

        def body(jj, carry):
            s_next = [half_scores(jj, 0, 0), half_scores(jj, 0, 1)]
            pending = None
            for u in range(len(units)):
                g = units[u][1]
                s_cur = s_next
                more = u + 1 < len(units)
                if more:
                    s_next = [half_scores(jj, u + 1, 0)]
                m_prev = m_ref[g]
                m_new = m_prev
                for s in s_cur:
                    m_new = jnp.maximum(m_new, jnp.max(s, axis=0, keepdims=True))
                m_ref[g] = m_new
                if pending is not None:
                    values(jj, *pending)
                if more:
                    s_next.append(half_scores(jj, u + 1, 1))
                p = jnp.concatenate([jnp.exp2((s - m_new).astype(BF16)) for s in s_cur], axis=0)
                pending = (u, p, jnp.exp2(m_prev - m_new))
            values(jj, *pending)
            return carry
        lax.fori_loop(0, k_ref.shape[0] // (tk * unroll), body, 0)

    sweep(ka_ref, vta_ref)

    @pl.when(two_ref[seg] == 1)
    def _():
        sweep(kb_ref, vtb_ref)

    o1 = acc_ref[0, 0:LANES, :] / acc_ref[0, LANES:LANES + 1, :]
    o2 = acc_ref[1, 0:LANES, :] / acc_ref[1, LANES:LANES + 1, :]
    lam = (jnp.exp(jnp.sum(lam_ref[0:1, :] * lam_ref[1:2, :], axis=-1, keepdims=True))
           - jnp.exp(jnp.sum(lam_ref[2:3, :] * lam_ref[3:4, :], axis=-1, keepdims=True))
           + lambda_init)
    ot = o1 - lam * o2
    ms = jnp.mean(ot * ot, axis=0, keepdims=True)
    ot = ot * lax.rsqrt(ms + SUBLN_EPS) * g_ref[...] * (1.0 - lambda_init)
    o_ref[...] = ot.T.astype(BF16)


def _flash(q, k, vt, lam_pack, subln_g_col, seg_a, seg_b, two, seg_len, lambda_init):
    t, d = q.shape
    nseg = t // seg_len
    n_vh, _, v_rows, tk = vt.shape
    width = 2 * HEAD_DIM
    per_seg = seg_len // tk
    tq = min(ATT_Q_TILE, seg_len)
    nq = seg_len // tq
    kern = functools.partial(_flash_kernel, tk=tk, unroll=min(ATT_UNROLL, per_seg), lambda_init=lambda_init)
    k_a = pl.BlockSpec((seg_len, width), lambda s, h, i, sa, sb, tw: (sa[s], h))
    k_b = pl.BlockSpec((seg_len, width), lambda s, h, i, sa, sb, tw: (sb[s], h))
    vt_a = pl.BlockSpec((1, per_seg, v_rows, tk), lambda s, h, i, sa, sb, tw: (h, sa[s], 0, 0))
    vt_b = pl.BlockSpec((1, per_seg, v_rows, tk), lambda s, h, i, sa, sb, tw: (h, sb[s], 0, 0))
    qo = pl.BlockSpec((tq, width), lambda s, h, i, sa, sb, tw: (s * nq + i, h))
    return pl.pallas_call(
        kern,
        grid_spec=pltpu.PrefetchScalarGridSpec(
            num_scalar_prefetch=3, grid=(nseg, n_vh, nq),
            in_specs=[qo, k_a, vt_a, k_b, vt_b,
                      pl.BlockSpec((SUBLANES, width), lambda s, h, i, sa, sb, tw: (0, 0)),
                      pl.BlockSpec((width, 1), lambda s, h, i, sa, sb, tw: (0, 0))],
            out_specs=qo,
            scratch_shapes=[pltpu.VMEM((2, width, tq), BF16),
                            pltpu.VMEM((2, 1, tq), F32),
                            pltpu.VMEM((2, v_rows, tq), F32)]),
        out_shape=jax.ShapeDtypeStruct((t, d), BF16),
        compiler_params=_cparams("arbitrary", "arbitrary", "arbitrary"),
        name="diff_flash_attention",
    )(seg_a, seg_b, two, q, k, vt, k, vt, lam_pack, subln_g_col)


def _proj_res_kernel(a_ref, w_ref, xa_ref, xb_ref, mod_ref, o_ref, *, gate_row, n_a):
    y = jnp.dot(a_ref[...], w_ref[...], preferred_element_type=F32)
    o_ref[...] = _token_rows(xa_ref, xb_ref, n_a) + mod_ref[0, gate_row:gate_row + 1, :] * y


def _proj_res(a, w_bf, xa, xb, modl, seg_len, tm, gate_row):
    d = xa.shape[1]
    t = xa.shape[0] + (0 if xb is None else xb.shape[0])
    xb = xa if xb is None else xb
    n_a, x_specs = _token_specs(xa, xb, tm, 1)
    per_seg = seg_len // tm
    return pl.pallas_call(
        functools.partial(_proj_res_kernel, gate_row=gate_row, n_a=n_a),
        grid=(t // tm,),
        in_specs=[pl.BlockSpec((tm, a.shape[1]), lambda i: (i, 0)),
                  pl.BlockSpec(w_bf.shape, lambda i: (0, 0))] + x_specs + [
                  pl.BlockSpec((1, SUBLANES, d), lambda i: (i // per_seg, 0, 0))],
        out_specs=pl.BlockSpec((tm, d), lambda i: (i, 0)),
        out_shape=jax.ShapeDtypeStruct((t, d), F32),
        compiler_params=_cparams("arbitrary"),
        name="proj_residual",
    )(a, w_bf, xa, xb, modl)


def _prenorm_kernel(x_ref, mod_ref, g_ref, o_ref):
    o_ref[...] = _rms_mod(x_ref[...], g_ref[...], mod_ref[0, 1:2, :], mod_ref[0, 0:1, :])


def _prenorm(x, modl, g, seg_len, tm):
    t, d = x.shape
    per_seg = seg_len // tm
    return pl.pallas_call(
        _prenorm_kernel,
        grid=(t // tm,),
        in_specs=[pl.BlockSpec((tm, d), lambda i: (i, 0)),
                  pl.BlockSpec((1, SUBLANES, d), lambda i: (i // per_seg, 0, 0)),
                  pl.BlockSpec((1, d), lambda i: (0, 0))],
        out_specs=pl.BlockSpec((tm, d), lambda i: (i, 0)),
        out_shape=jax.ShapeDtypeStruct((t, d), F32),
        compiler_params=_cparams("arbitrary"),
        name="s5_prenorm",
    )(x, modl, g)


def _block_diag_kernel(a_ref, o_ref, *, rows_interleaved, cols_interleaved):
    gt, n_r, n_c = a_ref.shape
    gc = GROUP_CH
    o_ref[...] = jnp.zeros(o_ref.shape, o_ref.dtype)
    src = lax.broadcasted_iota(jnp.int32, (n_c, gt * n_c), 0)
    dst = lax.broadcasted_iota(jnp.int32, (n_c, gt * n_c), 1)
    for r in range(gt):
        a = a_ref[r]
        if cols_interleaved:
            place = (dst == (src // gc) * (gt * gc) + r * gc + src % gc).astype(a.dtype)
            a = jnp.dot(a, place, preferred_element_type=F32).astype(o_ref.dtype)
            cols = slice(0, gt * n_c)
        else:
            cols = slice(r * n_c, (r + 1) * n_c)
        if rows_interleaved:
            for s in range(n_r // gc):
                o_ref[0, (s * gt + r) * gc:(s * gt + r + 1) * gc, cols] = a[s * gc:(s + 1) * gc, :]
        else:
            o_ref[0, r * n_r:(r + 1) * n_r, cols] = a


def _block_diag(tab, rows_interleaved, cols_interleaved):
    n_g, n_r, n_c = tab.shape
    gt = SSM_GROUP_TILE
    return pl.pallas_call(
        functools.partial(_block_diag_kernel, rows_interleaved=rows_interleaved,
                          cols_interleaved=cols_interleaved),
        grid=(n_g // gt,),
        in_specs=[pl.BlockSpec((gt, n_r, n_c), lambda j: (j, 0, 0))],
        out_specs=pl.BlockSpec((1, gt * n_r, gt * n_c), lambda j: (j, 0, 0)),
        out_shape=jax.ShapeDtypeStruct((n_g // gt, gt * n_r, gt * n_c), tab.dtype),
        compiler_params=_cparams("arbitrary"),
        name="s5_block_diag",
    )(tab)


def _s5_tables(a_re, a_im, log_dt, b_re, b_im, c_re, c_im):
    f32 = F32
    L = SSM_CHUNK
    ar, ai = a_re.astype(f32), a_im.astype(f32)
    n_g, n_p = ar.shape[1], ar.shape[2]
    n_h = b_re.shape[-1]
    dt = jnp.exp(log_dt.astype(f32))[..., None]
    mag = jnp.exp(ar * dt)
    lr, li = mag * jnp.cos(ai * dt), mag * jnp.sin(ai * dt)
    den = ar * ar + ai * ai
    fr = ((lr - 1.0) * ar + li * ai) / den
    fi = (li * ar - (lr - 1.0) * ai) / den
    br, bi = b_re.astype(f32), b_im.astype(f32)
    bbr = fr[..., None] * br - fi[..., None] * bi
    bbi = fr[..., None] * bi + fi[..., None] * br
    cr, ci = c_re.astype(f32), c_im.astype(f32)
    kk = jnp.arange(L + 1, dtype=f32)
    pmag = jnp.exp(ar[..., None] * dt[..., None] * kk)
    pang = ai[..., None] * dt[..., None] * kk
    pr, pi = pmag * jnp.cos(pang), pmag * jnp.sin(pang)
    er = cr[..., None] * pr[:, :, None] - ci[..., None] * pi[:, :, None]
    ei = cr[..., None] * pi[:, :, None] + ci[..., None] * pr[:, :, None]
    klag = (jnp.einsum('dgopk,dgpi->dgkoi', er[..., :L], bbr, precision=HIGHEST)
            - jnp.einsum('dgopk,dgpi->dgkoi', ei[..., :L], bbi, precision=HIGHEST))
    kf, kb = klag[0], klag[1]
    kfull = jnp.concatenate([kb[:, 1:][:, ::-1], (kf[:, 0] + kb[:, 0])[:, None], kf[:, 1:]], axis=1)
    s_idx = jnp.arange(L)[:, None]
    t_idx = jnp.arange(L)[None, :]
    w = kfull[:, t_idx - s_idx + L - 1]
    w = jnp.transpose(w, (0, 1, 4, 2, 3)).reshape(n_g, L * n_h, L * n_h)
    def in_to_state(d, pw_idx):
        pwr = jnp.take(pr[d], pw_idx, axis=-1)
        pwi = jnp.take(pi[d], pw_idx, axis=-1)
        re = pwr[..., None] * bbr[d][:, :, None, :] - pwi[..., None] * bbi[d][:, :, None, :]
        im = pwr[..., None] * bbi[d][:, :, None, :] + pwi[..., None] * bbr[d][:, :, None, :]
        m = jnp.concatenate([re, im], axis=1)
        return jnp.transpose(m, (0, 2, 3, 1)).reshape(n_g, L * n_h, 2 * n_p)
    pcat = jnp.concatenate([in_to_state(0, L - 1 - jnp.arange(L)), in_to_state(1, jnp.arange(L))], axis=-1)
    def state_to_out(d, pw_idx):
        e_r = jnp.take(er[d], pw_idx, axis=-1)
        e_i = jnp.take(ei[d], pw_idx, axis=-1)
        m = jnp.concatenate([e_r, -e_i], axis=2)
        return jnp.transpose(m, (0, 2, 3, 1)).reshape(n_g, 2 * n_p, L * n_h)
    qcat = jnp.concatenate([state_to_out(0, jnp.arange(L) + 1), state_to_out(1, L - jnp.arange(L))], axis=1)

    def scan_mult(d):
        a_r, a_i = pr[d][..., L], pi[d][..., L]
        m1 = jnp.concatenate([a_r, a_r], axis=-1)
        m2 = jnp.concatenate([-a_i, a_i], axis=-1)
        lay = lambda m: jnp.transpose(m.reshape(n_g // SSM_GROUP_TILE, SSM_GROUP_TILE, 2 * n_p),
                                      (1, 0, 2)).reshape(SSM_GROUP_TILE, -1)
        return lay(m1), lay(m2)
    af1, af2 = scan_mult(0)
    ab1, ab2 = scan_mult(1)
    wbd = _block_diag(w.astype(BF16), True, True)
    pbd = _block_diag(pcat.astype(BF16), True, False)
    qbd = _block_diag(qcat.astype(BF16), False, True)
    return wbd, pbd, qbd, (af1, af2, ab1, ab2)


def _chunk_rows(h_ref, tc):
    return jnp.concatenate([h_ref[pl.ds(s, tc, stride=SSM_CHUNK), :] for s in range(SSM_CHUNK)],
                           axis=1).astype(BF16)


def _s5_states_kernel(h_ref, p_ref, sf_ref, sb_ref):
    tc, gt, half = sf_ref.shape
    st = jnp.dot(_chunk_rows(h_ref, tc), p_ref[0], preferred_element_type=F32)
    for r in range(gt):
        sf_ref[:, r, :] = st[:, r * 2 * half:r * 2 * half + half]
        sb_ref[:, r, :] = st[:, r * 2 * half + half:(r + 1) * 2 * half]


def _s5_states(h, pbd, tc):
    t, d = h.shape
    n_blk, kdim, ncol = pbd.shape
    gt = SSM_GROUP_TILE
    half = ncol // (2 * gt)
    nc = t // SSM_CHUNK
    out = jax.ShapeDtypeStruct((nc, gt, n_blk * half), F32)
    return pl.pallas_call(
        _s5_states_kernel,
        grid=(n_blk, nc // tc),
        in_specs=[pl.BlockSpec((tc * SSM_CHUNK, LANES), lambda j, c: (c, j)),
                  pl.BlockSpec((1, kdim, ncol), lambda j, c: (j, 0, 0))],
        out_specs=[pl.BlockSpec((tc, gt, half), lambda j, c: (c, 0, j))] * 2,
        out_shape=[out, out],
        compiler_params=_cparams("arbitrary", "arbitrary"),
        name="s5_chunk_states",
    )(h, pbd)


def _s5_scan_kernel(rf_ref, rb_ref, sf_ref, sb_ref, af1_ref, af2_ref, ab1_ref, ab2_ref,
                    xf_ref, xb_ref, cf_ref, cb_ref, *, tcs, nblk):
    i = pl.program_id(0)

    @pl.when(i == 0)
    def _():
        cf_ref[...] = jnp.zeros(cf_ref.shape, F32)
        cb_ref[...] = jnp.zeros(cb_ref.shape, F32)

    width = cf_ref.shape[-1]

    def swap(x):
        return jnp.concatenate(
            [pltpu.roll(x[:, j * LANES:(j + 1) * LANES], LANES // 2, 1) for j in range(width // LANES)],
            axis=1)

    af1, af2 = af1_ref[...], af2_ref[...]
    ab1, ab2 = ab1_ref[...], ab2_ref[...]

    def step(r, carry):
        cf, cfs, cb, cbs = carry
        keep = (1 - rf_ref[i * tcs + r]).astype(F32)
        cf, cfs = cf * keep, cfs * keep
        xf_ref[r] = cf
        s = sf_ref[r]
        cf, cfs = af1 * cf + af2 * cfs + s, af1 * cfs - af2 * cf + swap(s)
        rb = tcs - 1 - r
        keep = (1 - rb_ref[(nblk - 1 - i) * tcs + rb]).astype(F32)
        cb, cbs = cb * keep, cbs * keep
        xb_ref[rb] = cb
        s = sb_ref[rb]
        cb, cbs = ab1 * cb + ab2 * cbs + s, ab1 * cbs - ab2 * cb + swap(s)
        return cf, cfs, cb, cbs

    cf0, cb0 = cf_ref[...], cb_ref[...]
    cf, _, cb, _ = lax.fori_loop(0, tcs, step, (cf0, swap(cf0), cb0, swap(cb0)), unroll=2)
    cf_ref[...] = cf
    cb_ref[...] = cb


def _s5_scan(sf, sb, mults, reset_f, reset_b, tcs):
    nc, gt, width = sf.shape
    nblk = nc // tcs
    fblk = pl.BlockSpec((tcs, gt, width), lambda i, a, b: (i, 0, 0))
    bblk = pl.BlockSpec((tcs, gt, width), lambda i, a, b: (nblk - 1 - i, 0, 0))
    tab = pl.BlockSpec((gt, width), lambda i, a, b: (0, 0))
    out = jax.ShapeDtypeStruct((nc, gt, width), F32)
    return pl.pallas_call(
        functools.partial(_s5_scan_kernel, tcs=tcs, nblk=nblk),
        grid_spec=pltpu.PrefetchScalarGridSpec(
            num_scalar_prefetch=2, grid=(nblk,),
            in_specs=[fblk, bblk, tab, tab, tab, tab],
            out_specs=[fblk, bblk],
            scratch_shapes=[pltpu.VMEM((gt, width), F32), pltpu.VMEM((gt, width), F32)]),
        out_shape=[out, out],
        compiler_params=_cparams("arbitrary"),
        name="s5_chunk_scan",
    )(reset_f, reset_b, sf, sb, *mults)


def _s5_out_kernel(h_ref, xf_ref, xb_ref, w_ref, q_ref, y_ref):
    tc, gt, _ = xf_ref.shape
    xcat = jnp.concatenate([ref[:, r, :] for r in range(gt) for ref in (xf_ref, xb_ref)],
                           axis=1).astype(BF16)
    y = (jnp.dot(_chunk_rows(h_ref, tc), w_ref[0], preferred_element_type=F32)
         + jnp.dot(xcat, q_ref[0], preferred_element_type=F32))
    for s in range(SSM_CHUNK):
        y_ref[pl.ds(s, tc, stride=SSM_CHUNK), :] = y[:, s * LANES:(s + 1) * LANES]


def _s5_out(h, xf, xb, wbd, qbd, tc):
    t, d = h.shape
    n_blk, kdim, ncol = wbd.shape
    gt = SSM_GROUP_TILE
    half = xf.shape[-1] // n_blk
    nc = t // SSM_CHUNK
    xblk = pl.BlockSpec((tc, gt, half), lambda j, c: (c, 0, j))
    tok = pl.BlockSpec((tc * SSM_CHUNK, LANES), lambda j, c: (c, j))
    wspec = lambda rows: pl.BlockSpec((1, rows, ncol), lambda j, c: (j, 0, 0), pipeline_mode=pl.Buffered(1))
    return pl.pallas_call(
        _s5_out_kernel,
        grid=(n_blk, nc // tc),
        in_specs=[tok, xblk, xblk, wspec(kdim), wspec(qbd.shape[1])],
        out_specs=tok,
        out_shape=jax.ShapeDtypeStruct((t, d), F32),
        compiler_params=_cparams("arbitrary", "arbitrary"),
        name="s5_chunk_outputs",
    )(h, xf, xb, wbd, qbd)


def _s5_glu_kernel(x_ref, y_ref, mod_ref, g_ref, dskip_ref, w_ref, o_ref):
    d = x_ref.shape[1]
    x = x_ref[...]
    h = _rms_mod(x, g_ref[...], mod_ref[0, 1:2, :], mod_ref[0, 0:1, :])
    z = jax.nn.gelu(y_ref[...] + dskip_ref[...] * h)
    vg = jnp.dot(z.astype(BF16), w_ref[...], preferred_element_type=F32)
    m = vg[:, :d] * jax.nn.sigmoid(vg[:, d:])
    o_ref[...] = x + mod_ref[0, 2:3, :] * m


def _s5_glu(x, y, modl, g, d_skip, w_bf, seg_len, tm):
    t, d = x.shape
    per_seg = seg_len // tm
    tok = pl.BlockSpec((tm, d), lambda i: (i, 0))
    vec = pl.BlockSpec((1, d), lambda i: (0, 0))
    return pl.pallas_call(
        _s5_glu_kernel,
        grid=(t // tm,),
        in_specs=[tok, tok, pl.BlockSpec((1, SUBLANES, d), lambda i: (i // per_seg, 0, 0)), vec, vec,
                  pl.BlockSpec((d, 2 * d), lambda i: (0, 0))],
        out_specs=tok,
        out_shape=jax.ShapeDtypeStruct((t, d), F32),
        compiler_params=_cparams("arbitrary"),
        name="s5_glu_residual",
    )(x, y, modl, g, d_skip, w_bf)


def _s5_layer(x, modl, g, params, seg_len, seq_first, seq_last, tm):
    a_re, a_im, log_dt, b_re, b_im, c_re, c_im, d_skip, w_glu = params
    t, d = x.shape
    L = SSM_CHUNK
    n_g = d // GROUP_CH
    nc = t // L
    wbd, pbd, qbd, mults = _s5_tables(a_re, a_im, log_dt, b_re, b_im, c_re, c_im)
    h = _prenorm(x, modl, g, seg_len, tm)
    tc = math.gcd(nc, SSM_ROW_TILE)
    sf, sb = _s5_states(h, pbd, tc)
    chunks_per_seg = seg_len // L
    reset_f = jnp.repeat(jnp.asarray(seq_first, jnp.int32), chunks_per_seg) * (
        jnp.tile(jnp.arange(chunks_per_seg) == 0, len(seq_first))).astype(jnp.int32)
    reset_b = jnp.repeat(jnp.asarray(seq_last, jnp.int32), chunks_per_seg) * (
        jnp.tile(jnp.arange(chunks_per_seg) == chunks_per_seg - 1, len(seq_last))).astype(jnp.int32)
    xf, xb = _s5_scan(sf, sb, mults, reset_f, reset_b, math.gcd(nc, SSM_SCAN_TILE))
    y = _s5_out(h, xf, xb, wbd, qbd, tc)
    return _s5_glu(x, y, modl, g, d_skip.reshape(1, d), w_glu.astype(BF16), seg_len, tm)


def _router_kernel(x_ref, mod_ref, g_ref, w_ref, b_ref, h_ref, idx_ref, gate_ref, rank_ref, cnt_ref,
                   carry_ref):
    tm = x_ref.shape[0]
    n_e = w_ref.shape[0]

    @pl.when(pl.program_id(0) == 0)
    def _():
        carry_ref[...] = jnp.zeros(carry_ref.shape, F32)

    h = _rms_mod(x_ref[...], g_ref[...], mod_ref[0, 4:5, :], mod_ref[0, 3:4, :])
    h_ref[...] = h
    logits = lax.dot_general(w_ref[...], h, (((1,), (1,)), ((), ())), precision=HIGHEST,
                             preferred_element_type=F32) + b_ref[...]
    e_iota = lax.broadcasted_iota(jnp.int32, (n_e, tm), 0)
    work = logits
    chosen = jnp.zeros((n_e, tm), F32)
    vals, idxs = [], []
    for _ in range(TOP_K):
        m = jnp.max(work, axis=0, keepdims=True)
        ix = jnp.min(jnp.where(work == m, e_iota, n_e), axis=0, keepdims=True)
        hit = e_iota == ix
        work = jnp.where(hit, -jnp.inf, work)
        chosen = jnp.where(hit, 1.0, chosen)
        vals.append(m)
        idxs.append(ix)
    v = jnp.concatenate(vals, axis=0)
    ex = jnp.exp(v - v[0:1])
    gate_ref[...] = ex / jnp.sum(ex, axis=0, keepdims=True)
    idx_ref[...] = jnp.concatenate(idxs, axis=0)
    row = lax.broadcasted_iota(jnp.int32, (tm, tm), 0)
    col = lax.broadcasted_iota(jnp.int32, (tm, tm), 1)
    before = (row < col).astype(BF16)
    cum = jnp.dot(chosen.astype(BF16), before, preferred_element_type=F32) + carry_ref[...]
    ranks = [jnp.sum(jnp.where(e_iota == ix, cum, 0.0), axis=0, keepdims=True) for ix in idxs]
    rank_ref[...] = jnp.concatenate(ranks, axis=0).astype(jnp.int32)
    carry_ref[...] = carry_ref[...] + jnp.sum(chosen, axis=1, keepdims=True)
    cnt_ref[...] = jnp.broadcast_to(carry_ref[...], cnt_ref.shape).astype(jnp.int32)


def _router(x, modl, g, w_router_t, b_router, seg_len, tm):
    t, d = x.shape
    n_e = w_router_t.shape[0]
    per_seg = seg_len // tm
    sel = pl.BlockSpec((TOP_K, tm), lambda i: (0, i))
    return pl.pallas_call(
        _router_kernel,
        grid=(t // tm,),
        in_specs=[pl.BlockSpec((tm, d), lambda i: (i, 0)),
                  pl.BlockSpec((1, SUBLANES, d), lambda i: (i // per_seg, 0, 0)),
                  pl.BlockSpec((1, d), lambda i: (0, 0)),
                  pl.BlockSpec((n_e, d), lambda i: (0, 0)),
                  pl.BlockSpec((n_e, 1), lambda i: (0, 0))],
        out_specs=[pl.BlockSpec((tm, d), lambda i: (i, 0)), sel, sel, sel,
                   pl.BlockSpec((n_e, LANES), lambda i: (0, 0))],
        out_shape=[jax.ShapeDtypeStruct((t, d), F32),
                   jax.ShapeDtypeStruct((TOP_K, t), jnp.int32),
                   jax.ShapeDtypeStruct((TOP_K, t), F32),
                   jax.ShapeDtypeStruct((TOP_K, t), jnp.int32),
                   jax.ShapeDtypeStruct((n_e, LANES), jnp.int32)],
        scratch_shapes=[pltpu.VMEM((n_e, 1), F32)],
        compiler_params=_cparams("arbitrary"),
        name="moe_router",
    )(x, modl, g, w_router_t, b_router)


def _dispatch_kernel(pad_from_ref, pad_n_ref, dest_ref, h_ref, xs_hbm, zrow_ref, sem, zsem):
    tm = h_ref.shape[0]

    def zero_row(r):
        return pltpu.make_async_copy(zrow_ref, xs_hbm.at[pl.ds(r, 1), :], zsem)

    @pl.when(pl.program_id(0) == 0)
    def _():
        zrow_ref[...] = jnp.zeros(zrow_ref.shape, F32)
        for e in range(pad_n_ref.shape[0]):
            def fill(r, carry, e=e):
                zero_row(pad_from_ref[e] + r).start()
                return carry
            lax.fori_loop(0, pad_n_ref[e], fill, 0)
        for e in range(pad_n_ref.shape[0]):
            def drain(r, carry):
                zero_row(0).wait()
                return carry
            lax.fori_loop(0, pad_n_ref[e], drain, 0)

    def issue(j, carry):
        for k in range(TOP_K):
            pltpu.make_async_copy(h_ref.at[pl.ds(j, 1), :],
                                  xs_hbm.at[pl.ds(dest_ref[0, 0, j * TOP_K + k], 1), :], sem).start()
        return carry

    lax.fori_loop(0, tm, issue, 0, unroll=DMA_ISSUE_UNROLL)
    for _ in range(TOP_K):
        pltpu.make_async_copy(h_ref, xs_hbm.at[pl.ds(0, tm), :], sem).wait()


def _dispatch(h, dest_tk, pad_from, pad_n, n_rows, tm):
    t, d = h.shape
    return pl.pallas_call(
        _dispatch_kernel,
        grid_spec=pltpu.PrefetchScalarGridSpec(
            num_scalar_prefetch=2, grid=(t // tm,),
            in_specs=[pl.BlockSpec((1, 1, tm * TOP_K), lambda i, pf, pn: (i, 0, 0), memory_space=pltpu.SMEM),
                      pl.BlockSpec((tm, d), lambda i, pf, pn: (i, 0))],
            out_specs=pl.BlockSpec(memory_space=pl.ANY),
            scratch_shapes=[pltpu.VMEM((1, d), F32), pltpu.SemaphoreType.DMA(()),
                            pltpu.SemaphoreType.DMA(())]),
        out_shape=jax.ShapeDtypeStruct((n_rows, d), F32),
        compiler_params=_cparams("arbitrary"),
        name="moe_dispatch",
    )(pad_from, pad_n, dest_tk.reshape(t // tm, 1, tm * TOP_K), h)


def _expert_kernel(be_ref, bi_ref, nu_ref, x_ref, wgu_ref, bgu_ref, wd_ref, bd_ref, y_ref,
                   wgu_bf_ref, wd_bf_ref):
    del bi_ref
    n_ff = wd_ref.shape[2]
    b = pl.program_id(0)

    @pl.when((b == 0) | (be_ref[b] != be_ref[jnp.maximum(b - 1, 0)]))
    def _():
        wgu_bf_ref[...] = wgu_ref[0, 0].astype(BF16)
        wd_bf_ref[...] = wd_ref[0, 0].astype(BF16)

    @pl.when(b < nu_ref[0])
    def _():
        x = x_ref[...].astype(BF16)
        acc = jnp.zeros(y_ref.shape, F32) + bd_ref[0]
        for c in range(n_ff // FF_CHUNK):
            lo = c * FF_CHUNK
            g_part = (jnp.dot(x, wgu_bf_ref[:, lo:lo + FF_CHUNK], preferred_element_type=F32)
                      + bgu_ref[0, :, lo:lo + FF_CHUNK])
            u_part = (jnp.dot(x, wgu_bf_ref[:, n_ff + lo:n_ff + lo + FF_CHUNK], preferred_element_type=F32)
                      + bgu_ref[0, :, n_ff + lo:n_ff + lo + FF_CHUNK])
            g_part = jnp.minimum(g_part, SWIGLU_LIMIT)
            u_part = jnp.clip(u_part, -SWIGLU_LIMIT, SWIGLU_LIMIT)
            act = (u_part + 1.0) * g_part * jax.nn.sigmoid(SWIGLU_ALPHA * g_part)
            acc = acc + jnp.dot(act.astype(BF16), wd_bf_ref[lo:lo + FF_CHUNK, :],
                                preferred_element_type=F32)
        y_ref[...] = acc


def _experts(xs, blk_expert, blk_idx, n_used, layer, w_gu, b_gu, w_down, b_down, tme):
    nr, d = xs.shape
    _, n_e, _, two_f = w_gu.shape
    n_ff = two_f // 2
    nblk = nr // tme
    rows = pl.BlockSpec((tme, d), lambda b, be, bi, nu: (bi[b], 0))
    b_gu = b_gu[layer]
    b_down = b_down[layer]
    return pl.pallas_call(
        _expert_kernel,
        grid_spec=pltpu.PrefetchScalarGridSpec(
            num_scalar_prefetch=3, grid=(nblk,),
            in_specs=[rows,
                      pl.BlockSpec((1, 1, d, two_f), lambda b, be, bi, nu: (layer, be[b], 0, 0)),
                      pl.BlockSpec((1, 1, two_f), lambda b, be, bi, nu: (be[b], 0, 0)),
                      pl.BlockSpec((1, 1, n_ff, d), lambda b, be, bi, nu: (layer, be[b], 0, 0)),
                      pl.BlockSpec((1, 1, d), lambda b, be, bi, nu: (be[b], 0, 0))],
            out_specs=rows,
            scratch_shapes=[pltpu.VMEM((d, two_f), BF16), pltpu.VMEM((n_ff, d), BF16)]),
        out_shape=jax.ShapeDtypeStruct(xs.shape, F32),
        compiler_params=_cparams("arbitrary"),
        name="moe_experts",
    )(blk_expert, blk_idx, n_used, xs, w_gu, b_gu.reshape(n_e, 1, two_f), w_down,
      b_down.reshape(n_e, 1, d))


def _combine_kernel(dest_ref, x_ref, gate_ref, mod_ref, g_ref, ys_hbm, o_ref, buf, sem, *, final_norm):
    tm = x_ref.shape[0]

    def issue(j, carry):
        for k in range(TOP_K):
            pltpu.make_async_copy(ys_hbm.at[pl.ds(dest_ref[0, 0, j * TOP_K + k], 1), :],
                                  buf.at[k, pl.ds(j, 1), :], sem).start()
        return carry

    lax.fori_loop(0, tm, issue, 0, unroll=DMA_ISSUE_UNROLL)
    for k in range(TOP_K):
        pltpu.make_async_copy(ys_hbm.at[pl.ds(0, tm), :], buf.at[k], sem).wait()
    gates = gate_ref[...]
    moe = gates[:, 0:1] * buf[0]
    for k in range(1, TOP_K):
        moe = moe + gates[:, k:k + 1] * buf[k]
    x = x_ref[...] + mod_ref[0, 5:6, :] * moe
    if final_norm:
        ms = jnp.mean(x * x, axis=-1, keepdims=True)
        x = x * lax.rsqrt(ms + NORM_EPS) * g_ref[...]
    o_ref[...] = x


def _combine(x, ys, dest_tk, gates, modl, final_g, seg_len, tm, final_norm, row0=0, rows=None):
    t, d = x.shape
    rows = t if rows is None else rows
    per_seg = seg_len // tm
    b0 = row0 // tm
    return pl.pallas_call(
        functools.partial(_combine_kernel, final_norm=final_norm),
        grid=(rows // tm,),
        in_specs=[pl.BlockSpec((1, 1, tm * TOP_K), lambda i: (i + b0, 0, 0), memory_space=pltpu.SMEM),
                  pl.BlockSpec((tm, d), lambda i: (i + b0, 0)),
                  pl.BlockSpec((tm, TOP_K), lambda i: (i + b0, 0)),
                  pl.BlockSpec((1, SUBLANES, d), lambda i: ((i + b0) // per_seg, 0, 0)),
                  pl.BlockSpec((1, d), lambda i: (0, 0)),
                  pl.BlockSpec(memory_space=pl.ANY)],
        out_specs=pl.BlockSpec((tm, d), lambda i: (i, 0)),
        out_shape=jax.ShapeDtypeStruct((rows, d), F32),
        scratch_shapes=[pltpu.VMEM((TOP_K, tm, d), F32), pltpu.SemaphoreType.DMA(())],
        compiler_params=_cparams("arbitrary"),
        name="moe_combine",
    )(dest_tk.reshape(t // tm, 1, tm * TOP_K), x, gates, modl, final_g, ys)


def _moe_dispatch_and_experts(x, modl, g, w_router, b_router, layer, w_gu, b_gu, w_down, b_down, seg_len, tm):
    t, d = x.shape
    n_e = w_router.shape[1]
    tme = math.gcd(t * TOP_K, EXPERT_ROW_TILE)
    h, idx_t, gate_t, rank_t, cnt = _router(x, modl, g, w_router.T, b_router.reshape(n_e, 1), seg_len, tm)
    counts = cnt[:, 0]
    padded = (counts + tme - 1) // tme * tme
    pad_end = jnp.cumsum(padded)
    pad_start = pad_end - padded
    nblk = (t * TOP_K) // tme + n_e
    n_used = (pad_end[-1] // tme).astype(jnp.int32).reshape(1)
    blk_idx = jnp.minimum(jnp.arange(nblk, dtype=jnp.int32), n_used[0] - 1)
    blk_expert = jnp.minimum(jnp.sum((pad_end[None, :] <= (blk_idx * tme)[:, None]).astype(jnp.int32), axis=1),
                             n_e - 1)
    start_of = jnp.sum(jnp.where(idx_t[None] == jnp.arange(n_e, dtype=jnp.int32)[:, None, None],
                                 pad_start[:, None, None], 0), axis=0)
    dest = (start_of + rank_t).astype(jnp.int32)
    dest_tk = dest.T.reshape(-1)
    xs = _dispatch(h, dest_tk, (pad_start + counts).astype(jnp.int32), (padded - counts).astype(jnp.int32),
                   nblk * tme, tm)
    ys = _experts(xs, blk_expert, blk_idx, n_used, layer, w_gu, b_gu, w_down, b_down, tme)
    return ys, dest_tk, gate_t.T


def kernel(x_prompt, x_sample, c_prompt, c_sample, norm1_g, norm2_g, final_g, w_ada, b_ada, w_qkv, w_o,
           lam_q1, lam_k1, lam_q2, lam_k2, subln_g, ssm_a_re, ssm_a_im, ssm_log_dt, ssm_b_re, ssm_b_im,
           ssm_c_re, ssm_c_im, ssm_d, ssm_w_glu, w_router, b_router, w_gu, b_gu, w_down, b_down):
    n_b, seg_len, d = x_prompt.shape
    n_bs, s_len, _ = x_sample.shape
    depth = w_ada.shape[0]
    assert s_len % seg_len == 0 and s_len // seg_len in (1, 2)
    per_sample = s_len // seg_len
    n_seg = n_b + n_bs * per_sample
    t = n_seg * seg_len
    tm = math.gcd(seg_len, TOKEN_TILE)

    seg_seq = list(range(n_b)) + [n_b + j for j in range(n_bs) for _ in range(per_sample)]
    seg_pos = [0] * n_b + [r for _ in range(n_bs) for r in range(per_sample)]
    seg_a = list(range(n_b)) + [n_b + j * per_sample for j in range(n_bs) for _ in range(per_sample)]
    seg_b = list(range(n_b)) + [n_b + j * per_sample + per_sample - 1 for j in range(n_bs) for _ in range(per_sample)]
    two = [0] * n_b + [int(per_sample == 2)] * (n_bs * per_sample)
    seq_first = [1] * n_b + [int(r == 0) for _ in range(n_bs) for r in range(per_sample)]
    seq_last = [1] * n_b + [int(r == per_sample - 1) for _ in range(n_bs) for r in range(per_sample)]
    per_seg = seg_len // tm
    pos_blk = jnp.asarray([seg_pos[s] * per_seg + r for s in range(n_seg) for r in range(per_seg)], jnp.int32)

    x = None
    x_parts = (x_prompt.reshape(n_b * seg_len, d), x_sample.reshape(n_bs * s_len, d))

    n_c = n_b + n_bs
    c_rows = -(-n_c // SUBLANES) * SUBLANES
    c_all = jnp.concatenate([c_prompt, c_sample, jnp.zeros((c_rows - n_c, d), F32)], axis=0)
    mod = _ada(c_all, w_ada, b_ada)
    mod = mod[:, jnp.asarray(seg_seq)].reshape(depth, n_seg, 6, d)
    mod = jnp.concatenate([mod, jnp.zeros((depth, n_seg, SUBLANES - 6, d), F32)], axis=2)

    inv_freq = ROPE_THETA ** (-jnp.arange(0, HEAD_DIM, 2, dtype=F32) / HEAD_DIM)
    ang = jnp.arange(s_len, dtype=F32)[:, None] * inv_freq[None, :]
    cos_t = jnp.tile(jnp.cos(ang), (1, 2 * LANES // HEAD_DIM))
    sin_t = jnp.tile(jnp.concatenate([-jnp.sin(ang), jnp.sin(ang)], axis=1), (1, LANES // HEAD_DIM))

    out_p = out_s = None
    for i in range(depth):
        modl = mod[i]
        g1 = norm1_g[i].reshape(1, d)
        g2 = norm2_g[i].reshape(1, d)
        j = i // 2
        if i % 2 == 0:
            lambda_init = 0.8 - 0.6 * math.exp(-0.3 * i)
            xa, xb = x_parts if x is None else (x, None)
            q, k, vt = _qkv(xa, xb, modl, g1, w_qkv[j], cos_t, sin_t, pos_blk, seg_len, tm)
            lam_pack = jnp.zeros((SUBLANES, 2 * HEAD_DIM), F32).at[0:4, 0:HEAD_DIM].set(
                jnp.stack([lam_q1[j], lam_k1[j], lam_q2[j], lam_k2[j]]).astype(F32))
            o = _flash(q, k, vt, lam_pack, subln_g[j].reshape(2 * HEAD_DIM, 1).astype(F32),
                       jnp.asarray(seg_a, jnp.int32), jnp.asarray(seg_b, jnp.int32),
                       jnp.asarray(two, jnp.int32), seg_len, lambda_init)
            x = _proj_res(o, w_o[j].astype(BF16), xa, xb, modl, seg_len, tm, gate_row=2)
        else:
            if x is None:
                x = jnp.concatenate(x_parts, axis=0)
            params = (ssm_a_re[j], ssm_a_im[j], ssm_log_dt[j], ssm_b_re[j], ssm_b_im[j], ssm_c_re[j],
                      ssm_c_im[j], ssm_d[j], ssm_w_glu[j])
            x = _s5_layer(x, modl, g1, params, seg_len, seq_first, seq_last, tm)
        ys, dest_tk, gates = _moe_dispatch_and_experts(x, modl, g2, w_router[i], b_router[i], i, w_gu, b_gu,
                                                       w_down, b_down, seg_len, tm)
        fg = final_g.reshape(1, d)
        if i == depth - 1:
            out_p = _combine(x, ys, dest_tk, gates, modl, fg, seg_len, tm, True, 0, n_b * seg_len)
            out_s = _combine(x, ys, dest_tk, gates, modl, fg, seg_len, tm, True, n_b * seg_len, n_bs * s_len)
        else:
            x = _combine(x, ys, dest_tk, gates, modl, fg, seg_len, tm, False)
    return out_p.reshape(n_b, seg_len, d), out_s.reshape(n_bs, s_len, d)
```

```python
import functools
import math

import jax
import jax.numpy as jnp
from jax import lax
from jax.experimental import pallas as pl
from jax.experimental.pallas import tpu as pltpu

F32 = jnp.float32
BF16 = jnp.bfloat16
HIGHEST = lax.Precision.HIGHEST

N_DIFF_HEADS = 8
HEAD_DIM = 64
ROPE_THETA = 10000.0
SUBLN_EPS = 1e-5
NORM_EPS = 1e-6
GROUP_CH = 16
STATE_DIM = 64
N_EXPERTS = 32
TOP_K = 4
SWIGLU_ALPHA = 1.702
SWIGLU_LIMIT = 7.0

LANES = 128
SUBLANES = 8
VMEM_LIMIT_BYTES = 56 * 1024 * 1024

TOKEN_TILE = 512
ATT_Q_TILE = 512
ATT_KV_TILE = 512
ATT_UNROLL = 8
VT_ONES_ROWS = 16
LOG2_E = math.log2(math.e)
SSM_CHUNK = 16
SSM_GROUP_TILE = 8
SSM_ROW_TILE = 256
SSM_SCAN_TILE = 64
EXPERT_ROW_TILE = 512
FF_CHUNK = 512
DMA_ISSUE_UNROLL = 8


def _cparams(*sem):
    return pltpu.CompilerParams(dimension_semantics=sem, vmem_limit_bytes=VMEM_LIMIT_BYTES)


def _rms_mod(x, g, scale, shift):
    ms = jnp.mean(x * x, axis=-1, keepdims=True)
    return x * lax.rsqrt(ms + NORM_EPS) * g * (1.0 + scale) + shift


def _ada_kernel(c_ref, w_ref, b_ref, o_ref):
    c = c_ref[...]
    cond = c * jax.nn.sigmoid(c)
    o_ref[0] = jnp.dot(cond, w_ref[0], precision=HIGHEST, preferred_element_type=F32) + b_ref[0]


def _ada(c_pad, w_ada, b_ada):
    depth, d, n6 = w_ada.shape
    rows = c_pad.shape[0]
    tn = 1536 if n6 % 1536 == 0 else n6
    return pl.pallas_call(
        _ada_kernel,
        grid=(depth, n6 // tn),
        in_specs=[pl.BlockSpec((rows, d), lambda i, j: (0, 0)),
                  pl.BlockSpec((1, d, tn), lambda i, j: (i, 0, j)),
                  pl.BlockSpec((1, 1, tn), lambda i, j: (i, 0, j))],
        out_specs=pl.BlockSpec((1, rows, tn), lambda i, j: (i, 0, j)),
        out_shape=jax.ShapeDtypeStruct((depth, rows, n6), F32),
        compiler_params=_cparams("arbitrary", "arbitrary"),
        name="ada_mod",
    )(c_pad, w_ada, b_ada.reshape(depth, 1, n6))


def _token_rows(xa_ref, xb_ref, n_a):
    return jnp.where(pl.program_id(0) < n_a, xa_ref[...], xb_ref[...])


def _token_specs(xa, xb, tm, index_args):
    n_a = xa.shape[0] // tm
    d = xa.shape[1]
    if index_args == 1:
        return n_a, [pl.BlockSpec((tm, d), lambda i: (jnp.minimum(i, n_a - 1), 0)),
                     pl.BlockSpec((tm, d), lambda i: (jnp.maximum(i - n_a, 0), 0))]
    return n_a, [pl.BlockSpec((tm, d), lambda i, p: (jnp.minimum(i, n_a - 1), 0)),
                 pl.BlockSpec((tm, d), lambda i, p: (jnp.maximum(i - n_a, 0), 0))]


def _qkv_kernel(pos_ref, xa_ref, xb_ref, mod_ref, g_ref, w_ref, wvt_ref, cos_ref, sin_ref, q_ref, k_ref,
                vt_ref, *, n_a):
    del pos_ref
    d = xa_ref.shape[1]
    h = _rms_mod(_token_rows(xa_ref, xb_ref, n_a), g_ref[...], mod_ref[0, 1:2, :],
                 mod_ref[0, 0:1, :]).astype(BF16)
    qkv = jnp.dot(h, w_ref[...], preferred_element_type=F32)
    vt = lax.dot_general(wvt_ref[...], h, (((1,), (1,)), ((), ())), preferred_element_type=F32)
    ones = jnp.ones((VT_ONES_ROWS, vt.shape[1]), BF16)
    for hd in range(vt_ref.shape[0]):
        vt_ref[hd, 0, 0:LANES, :] = vt[hd * LANES:(hd + 1) * LANES, :].astype(BF16)
        vt_ref[hd, 0, LANES:LANES + VT_ONES_ROWS, :] = ones
    cos = cos_ref[...]
    sin = sin_ref[...]
    lane = lax.broadcasted_iota(jnp.int32, cos.shape, 1)
    first_half = (lane % HEAD_DIM) < (HEAD_DIM // 2)

    def rotary(t):
        partner = jnp.where(first_half,
                            pltpu.roll(t, LANES - HEAD_DIM // 2, 1),
                            pltpu.roll(t, HEAD_DIM // 2, 1))
        return t * cos + partner * sin

    for j in range(d // LANES):
        sl = slice(j * LANES, (j + 1) * LANES)
        q_ref[:, sl] = (rotary(qkv[:, sl]) * (HEAD_DIM ** -0.5 * LOG2_E)).astype(BF16)
        k_ref[:, sl] = rotary(qkv[:, d + j * LANES:d + (j + 1) * LANES]).astype(BF16)


def _qkv(xa, xb, modl, g, w_qkv, cos_t, sin_t, pos_blk, seg_len, tm):
    d = xa.shape[1]
    t = xa.shape[0] + (0 if xb is None else xb.shape[0])
    xb = xa if xb is None else xb
    n_a, x_specs = _token_specs(xa, xb, tm, 2)
    per_seg = seg_len // tm
    n_vh = d // LANES
    tk = math.gcd(seg_len, ATT_KV_TILE)
    per_kv = tk // tm
    tok = lambda i, p: (i, 0)
    out = jax.ShapeDtypeStruct((t, d), BF16)
    w_qk = w_qkv[:, :2 * d].astype(BF16)
    w_vt = w_qkv[:, 2 * d:].T.astype(BF16)
    return pl.pallas_call(
        functools.partial(_qkv_kernel, n_a=n_a),
        grid_spec=pltpu.PrefetchScalarGridSpec(
            num_scalar_prefetch=1, grid=(t // tm,),
            in_specs=x_specs + [
                      pl.BlockSpec((1, SUBLANES, d), lambda i, p: (i // per_seg, 0, 0)),
                      pl.BlockSpec((1, d), lambda i, p: (0, 0)),
                      pl.BlockSpec((d, 2 * d), lambda i, p: (0, 0)),
                      pl.BlockSpec((d, d), lambda i, p: (0, 0)),
                      pl.BlockSpec((tm, LANES), lambda i, p: (p[i], 0)),
                      pl.BlockSpec((tm, LANES), lambda i, p: (p[i], 0))],
            out_specs=[pl.BlockSpec((tm, d), tok), pl.BlockSpec((tm, d), tok),
                       pl.BlockSpec((n_vh, 1, LANES + VT_ONES_ROWS, tm),
                                    lambda i, p: (0, i // per_kv, 0, i % per_kv))]),
        out_shape=[out, out, jax.ShapeDtypeStruct((n_vh, t // tk, LANES + VT_ONES_ROWS, tk), BF16)],
        compiler_params=_cparams("arbitrary"),
        name="qkv_rotary",
    )(pos_blk, xa, xb, modl, g, w_qk, w_vt, cos_t, sin_t)


def _flash_kernel(sa_ref, sb_ref, two_ref, q_ref, ka_ref, vta_ref, kb_ref, vtb_ref, lam_ref, g_ref,
                  o_ref, qt_ref, m_ref, acc_ref, *, tk, unroll, lambda_init):
    del sa_ref, sb_ref
    seg = pl.program_id(0)
    qt = q_ref[...].astype(F32).T
    chan = lax.broadcasted_iota(jnp.int32, qt.shape, 0)
    qt_ref[0] = jnp.where(chan < HEAD_DIM, qt, 0.0).astype(BF16)
    qt_ref[1] = jnp.where(chan >= HEAD_DIM, qt, 0.0).astype(BF16)
    m_ref[...] = jnp.full(m_ref.shape, -jnp.inf, F32)
    acc_ref[...] = jnp.zeros(acc_ref.shape, F32)

    def sweep(k_ref, vt_ref):
        units = [(tile, g) for tile in range(unroll) for g in range(2)]

        half = tk // 2

        def half_scores(jj, u, r):
            tile, g = units[u]
            off = pl.multiple_of((jj * unroll + tile) * tk, tk)
            return jnp.dot(k_ref[pl.ds(off + r * half, half), :], qt_ref[g], preferred_element_type=F32)

        def values(jj, u, p, alpha):
            tile, g = units[u]
            acc_ref[g] = alpha * acc_ref[g] + jnp.dot(vt_ref[0, jj * unroll + tile], p,
                                                      preferred_element_type=F32)

        def body(jj, carry):
            s_next = [half_scores(jj, 0, 0), half_scores(jj, 0, 1)]
            pending = None
            for u in range(len(units)):
                g = units[u][1]
                s_cur = s_next
                more = u + 1 < len(units)
                if more:
                    s_next = [half_scores(jj, u + 1, 0)]
                m_prev = m_ref[g]
                m_new = m_prev
                for s in s_cur:
                    m_new = jnp.maximum(m_new, jnp.max(s, axis=0, keepdims=True))
                m_ref[g] = m_new
                if pending is not None:
                    values(jj, *pending)
                if more:
                    s_next.append(half_scores(jj, u + 1, 1))
                p = jnp.concatenate([jnp.exp2((s - m_new).astype(BF16)) for s in s_cur], axis=0)
                pending = (u, p, jnp.exp2(m_prev - m_new))
            values(jj, *pending)
            return carry
        lax.fori_loop(0, k_ref.shape[0] // (tk * unroll), body, 0)

    sweep(ka_ref, vta_ref)

    @pl.when(two_ref[seg] == 1)
    def _():
        sweep(kb_ref, vtb_ref)

    o1 = acc_ref[0, 0:LANES, :] / acc_ref[0, LANES:LANES + 1, :]
    o2 = acc_ref[1, 0:LANES, :] / acc_ref[1, LANES:LANES + 1, :]
    lam = (jnp.exp(jnp.sum(lam_ref[0:1, :] * lam_ref[1:2, :], axis=-1, keepdims=True))
           - jnp.exp(jnp.sum(lam_ref[2:3, :] * lam_ref[3:4, :], axis=-1, keepdims=True))
           + lambda_init)
    ot = o1 - lam * o2
    ms = jnp.mean(ot * ot, axis=0, keepdims=True)
    ot = ot * lax.rsqrt(ms + SUBLN_EPS) * g_ref[...] * (1.0 - lambda_init)
    o_ref[...] = ot.T.astype(BF16)


def _flash(q, k, vt, lam_pack, subln_g_col, seg_a, seg_b, two, seg_len, lambda_init):
    t, d = q.shape
    nseg = t // seg_len
    n_vh, _, v_rows, tk = vt.shape
    width = 2 * HEAD_DIM
    per_seg = seg_len // tk
    tq = min(ATT_Q_TILE, seg_len)
    nq = seg_len // tq
    kern = functools.partial(_flash_kernel, tk=tk, unroll=min(ATT_UNROLL, per_seg), lambda_init=lambda_init)
    k_a = pl.BlockSpec((seg_len, width), lambda s, h, i, sa, sb, tw: (sa[s], h))
    k_b = pl.BlockSpec((seg_len, width), lambda s, h, i, sa, sb, tw: (sb[s], h))
    vt_a = pl.BlockSpec((1, per_seg, v_rows, tk), lambda s, h, i, sa, sb, tw: (h, sa[s], 0, 0))
    vt_b = pl.BlockSpec((1, per_seg, v_rows, tk), lambda s, h, i, sa, sb, tw: (h, sb[s], 0, 0))
    qo = pl.BlockSpec((tq, width), lambda s, h, i, sa, sb, tw: (s * nq + i, h))
    return pl.pallas_call(
        kern,
        grid_spec=pltpu.PrefetchScalarGridSpec(
            num_scalar_prefetch=3, grid=(nseg, n_vh, nq),
            in_specs=[qo, k_a, vt_a, k_b, vt_b,
                      pl.BlockSpec((SUBLANES, width), lambda s, h, i, sa, sb, tw: (0, 0)),
                      pl.BlockSpec((width, 1), lambda s, h, i, sa, sb, tw: (0, 0))],
            out_specs=qo,
            scratch_shapes=[pltpu.VMEM((2, width, tq), BF16),
                            pltpu.VMEM((2, 1, tq), F32),
                            pltpu.VMEM((2, v_rows, tq), F32)]),
        out_shape=jax.ShapeDtypeStruct((t, d), BF16),
        compiler_params=_cparams("arbitrary", "arbitrary", "arbitrary"),
        name="diff_flash_attention",
    )(seg_a, seg_b, two, q, k, vt, k, vt, lam_pack, subln_g_col)


def _proj_res_kernel(a_ref, w_ref, xa_ref, xb_ref, mod_ref, o_ref, *, gate_row, n_a):
    y = jnp.dot(a_ref[...], w_ref[...], preferred_element_type=F32)
    o_ref[...] = _token_rows(xa_ref, xb_ref, n_a) + mod_ref[0, gate_row:gate_row + 1, :] * y


def _proj_res(a, w_bf, xa, xb, modl, seg_len, tm, gate_row):
    d = xa.shape[1]
    t = xa.shape[0] + (0 if xb is None else xb.shape[0])
    xb = xa if xb is None else xb
    n_a, x_specs = _token_specs(xa, xb, tm, 1)
    per_seg = seg_len // tm
    return pl.pallas_call(
        functools.partial(_proj_res_kernel, gate_row=gate_row, n_a=n_a),
        grid=(t // tm,),
        in_specs=[pl.BlockSpec((tm, a.shape[1]), lambda i: (i, 0)),
                  pl.BlockSpec(w_bf.shape, lambda i: (0, 0))] + x_specs + [
                  pl.BlockSpec((1, SUBLANES, d), lambda i: (i // per_seg, 0, 0))],
        out_specs=pl.BlockSpec((tm, d), lambda i: (i, 0)),
        out_shape=jax.ShapeDtypeStruct((t, d), F32),
        compiler_params=_cparams("arbitrary"),
        name="proj_residual",
    )(a, w_bf, xa, xb, modl)


def _prenorm_kernel(x_ref, mod_ref, g_ref, o_ref):
    o_ref[...] = _rms_mod(x_ref[...], g_ref[...], mod_ref[0, 1:2, :], mod_ref[0, 0:1, :])


def _prenorm(x, modl, g, seg_len, tm):
    t, d = x.shape
    per_seg = seg_len // tm
    return pl.pallas_call(
        _prenorm_kernel,
        grid=(t // tm,),
        in_specs=[pl.BlockSpec((tm, d), lambda i: (i, 0)),
                  pl.BlockSpec((1, SUBLANES, d), lambda i: (i // per_seg, 0, 0)),
                  pl.BlockSpec((1, d), lambda i: (0, 0))],
        out_specs=pl.BlockSpec((tm, d), lambda i: (i, 0)),
        out_shape=jax.ShapeDtypeStruct((t, d), F32),
        compiler_params=_cparams("arbitrary"),
        name="s5_prenorm",
    )(x, modl, g)


def _block_diag_kernel(a_ref, o_ref, *, rows_interleaved, cols_interleaved):
    gt, n_r, n_c = a_ref.shape
    gc = GROUP_CH
    o_ref[...] = jnp.zeros(o_ref.shape, o_ref.dtype)
    src = lax.broadcasted_iota(jnp.int32, (n_c, gt * n_c), 0)
    dst = lax.broadcasted_iota(jnp.int32, (n_c, gt * n_c), 1)
    for r in range(gt):
        a = a_ref[r]
        if cols_interleaved:
            place = (dst == (src // gc) * (gt * gc) + r * gc + src % gc).astype(a.dtype)
            a = jnp.dot(a, place, preferred_element_type=F32).astype(o_ref.dtype)
            cols = slice(0, gt * n_c)
        else:
            cols = slice(r * n_c, (r + 1) * n_c)
        if rows_interleaved:
            for s in range(n_r // gc):
                o_ref[0, (s * gt + r) * gc:(s * gt + r + 1) * gc, cols] = a[s * gc:(s + 1) * gc, :]
        else:
            o_ref[0, r * n_r:(r + 1) * n_r, cols] = a


def _block_diag(tab, rows_interleaved, cols_interleaved):
    n_g, n_r, n_c = tab.shape
    gt = SSM_GROUP_TILE
    return pl.pallas_call(
        functools.partial(_block_diag_kernel, rows_interleaved=rows_interleaved,
                          cols_interleaved=cols_interleaved),
        grid=(n_g // gt,),
        in_specs=[pl.BlockSpec((gt, n_r, n_c), lambda j: (j, 0, 0))],
        out_specs=pl.BlockSpec((1, gt * n_r, gt * n_c), lambda j: (j, 0, 0)),
        out_shape=jax.ShapeDtypeStruct((n_g // gt, gt * n_r, gt * n_c), tab.dtype),
        compiler_params=_cparams("arbitrary"),
        name="s5_block_diag",
    )(tab)


def _s5_tables(a_re, a_im, log_dt, b_re, b_im, c_re, c_im):
    f32 = F32
    L = SSM_CHUNK
    ar, ai = a_re.astype(f32), a_im.astype(f32)
    n_g, n_p = ar.shape[1], ar.shape[2]
    n_h = b_re.shape[-1]
    dt = jnp.exp(log_dt.astype(f32))[..., None]
    mag = jnp.exp(ar * dt)
    lr, li = mag * jnp.cos(ai * dt), mag * jnp.sin(ai * dt)
    den = ar * ar + ai * ai
    fr = ((lr - 1.0) * ar + li * ai) / den
    fi = (li * ar - (lr - 1.0) * ai) / den
    br, bi = b_re.astype(f32), b_im.astype(f32)
    bbr = fr[..., None] * br - fi[..., None] * bi
    bbi = fr[..., None] * bi + fi[..., None] * br
    cr, ci = c_re.astype(f32), c_im.astype(f32)
    kk = jnp.arange(L + 1, dtype=f32)
    pmag = jnp.exp(ar[..., None] * dt[..., None] * kk)
    pang = ai[..., None] * dt[..., None] * kk
    pr, pi = pmag * jnp.cos(pang), pmag * jnp.sin(pang)
    er = cr[..., None] * pr[:, :, None] - ci[..., None] * pi[:, :, None]
    ei = cr[..., None] * pi[:, :, None] + ci[..., None] * pr[:, :, None]
    klag = (jnp.einsum('dgopk,dgpi->dgkoi', er[..., :L], bbr, precision=HIGHEST)
            - jnp.einsum('dgopk,dgpi->dgkoi', ei[..., :L], bbi, precision=HIGHEST))
    kf, kb = klag[0], klag[1]
    kfull = jnp.concatenate([kb[:, 1:][:, ::-1], (kf[:, 0] + kb[:, 0])[:, None], kf[:, 1:]], axis=1)
    s_idx = jnp.arange(L)[:, None]
    t_idx = jnp.arange(L)[None, :]
    w = kfull[:, t_idx - s_idx + L - 1]
    w = jnp.transpose(w, (0, 1, 4, 2, 3)).reshape(n_g, L * n_h, L * n_h)
    def in_to_state(d, pw_idx):
        pwr = jnp.take(pr[d], pw_idx, axis=-1)
        pwi = jnp.take(pi[d], pw_idx, axis=-1)
        re = pwr[..., None] * bbr[d][:, :, None, :] - pwi[..., None] * bbi[d][:, :, None, :]
        im = pwr[..., None] * bbi[d][:, :, None, :] + pwi[..., None] * bbr[d][:, :, None, :]
        m = jnp.concatenate([re, im], axis=1)
        return jnp.transpose(m, (0, 2, 3, 1)).reshape(n_g, L * n_h, 2 * n_p)
    pcat = jnp.concatenate([in_to_state(0, L - 1 - jnp.arange(L)), in_to_state(1, jnp.arange(L))], axis=-1)
    def state_to_out(d, pw_idx):
        e_r = jnp.take(er[d], pw_idx, axis=-1)
        e_i = jnp.take(ei[d], pw_idx, axis=-1)
        m = jnp.concatenate([e_r, -e_i], axis=2)
        return jnp.transpose(m, (0, 2, 3, 1)).reshape(n_g, 2 * n_p, L * n_h)
    qcat = jnp.concatenate([state_to_out(0, jnp.arange(L) + 1), state_to_out(1, L - jnp.arange(L))], axis=1)

    def scan_mult(d):
        a_r, a_i = pr[d][..., L], pi[d][..., L]
        m1 = jnp.concatenate([a_r, a_r], axis=-1)
        m2 = jnp.concatenate([-a_i, a_i], axis=-1)
        lay = lambda m: jnp.transpose(m.reshape(n_g // SSM_GROUP_TILE, SSM_GROUP_TILE, 2 * n_p),
                                      (1, 0, 2)).reshape(SSM_GROUP_TILE, -1)
        return lay(m1), lay(m2)
    af1, af2 = scan_mult(0)
    ab1, ab2 = scan_mult(1)
    wbd = _block_diag(w.astype(BF16), True, True)
    pbd = _block_diag(pcat.astype(BF16), True, False)
    qbd = _block_diag(qcat.astype(BF16), False, True)
    return wbd, pbd, qbd, (af1, af2, ab1, ab2)


def _chunk_rows(h_ref, tc):
    return jnp.concatenate([h_ref[pl.ds(s, tc, stride=SSM_CHUNK), :] for s in range(SSM_CHUNK)],
                           axis=1).astype(BF16)


def _s5_states_kernel(h_ref, p_ref, sf_ref, sb_ref):
    tc, gt, half = sf_ref.shape
    st = jnp.dot(_chunk_rows(h_ref, tc), p_ref[0], preferred_element_type=F32)
    for r in range(gt):
        sf_ref[:, r, :] = st[:, r * 2 * half:r * 2 * half + half]
        sb_ref[:, r, :] = st[:, r * 2 * half + half:(r + 1) * 2 * half]


def _s5_states(h, pbd, tc):
    t, d = h.shape
    n_blk, kdim, ncol = pbd.shape
    gt = SSM_GROUP_TILE
    half = ncol // (2 * gt)
    nc = t // SSM_CHUNK
    out = jax.ShapeDtypeStruct((nc, gt, n_blk * half), F32)
    return pl.pallas_call(
        _s5_states_kernel,
        grid=(n_blk, nc // tc),
        in_specs=[pl.BlockSpec((tc * SSM_CHUNK, LANES), lambda j, c: (c, j)),
                  pl.BlockSpec((1, kdim, ncol), lambda j, c: (j, 0, 0))],
        out_specs=[pl.BlockSpec((tc, gt, half), lambda j, c: (c, 0, j))] * 2,
        out_shape=[out, out],
        compiler_params=_cparams("arbitrary", "arbitrary"),
        name="s5_chunk_states",
    )(h, pbd)


def _s5_scan_kernel(rf_ref, rb_ref, sf_ref, sb_ref, af1_ref, af2_ref, ab1_ref, ab2_ref,
                    xf_ref, xb_ref, cf_ref, cb_ref, *, tcs, nblk):
    i = pl.program_id(0)

    @pl.when(i == 0)
    def _():
        cf_ref[...] = jnp.zeros(cf_ref.shape, F32)
        cb_ref[...] = jnp.zeros(cb_ref.shape, F32)

    width = cf_ref.shape[-1]

    def swap(x):
        return jnp.concatenate(
            [pltpu.roll(x[:, j * LANES:(j + 1) * LANES], LANES // 2, 1) for j in range(width // LANES)],
            axis=1)

    af1, af2 = af1_ref[...], af2_ref[...]
    ab1, ab2 = ab1_ref[...], ab2_ref[...]

    def step(r, carry):
        cf, cfs, cb, cbs = carry
        keep = (1 - rf_ref[i * tcs + r]).astype(F32)
        cf, cfs = cf * keep, cfs * keep
        xf_ref[r] = cf
        s = sf_ref[r]
        cf, cfs = af1 * cf + af2 * cfs + s, af1 * cfs - af2 * cf + swap(s)
        rb = tcs - 1 - r
        keep = (1 - rb_ref[(nblk - 1 - i) * tcs + rb]).astype(F32)
        cb, cbs = cb * keep, cbs * keep
        xb_ref[rb] = cb
        s = sb_ref[rb]
        cb, cbs = ab1 * cb + ab2 * cbs + s, ab1 * cbs - ab2 * cb + swap(s)
        return cf, cfs, cb, cbs

    cf0, cb0 = cf_ref[...], cb_ref[...]
    cf, _, cb, _ = lax.fori_loop(0, tcs, step, (cf0, swap(cf0), cb0, swap(cb0)), unroll=2)
    cf_ref[...] = cf
    cb_ref[...] = cb


def _s5_scan(sf, sb, mults, reset_f, reset_b, tcs):
    nc, gt, width = sf.shape
    nblk = nc // tcs
    fblk = pl.BlockSpec((tcs, gt, width), lambda i, a, b: (i, 0, 0))
    bblk = pl.BlockSpec((tcs, gt, width), lambda i, a, b: (nblk - 1 - i, 0, 0))
    tab = pl.BlockSpec((gt, width), lambda i, a, b: (0, 0))
    out = jax.ShapeDtypeStruct((nc, gt, width), F32)
    return pl.pallas_call(
        functools.partial(_s5_scan_kernel, tcs=tcs, nblk=nblk),
        grid_spec=pltpu.PrefetchScalarGridSpec(
            num_scalar_prefetch=2, grid=(nblk,),
            in_specs=[fblk, bblk, tab, tab, tab, tab],
            out_specs=[fblk, bblk],
            scratch_shapes=[pltpu.VMEM((gt, width), F32), pltpu.VMEM((gt, width), F32)]),
        out_shape=[out, out],
        compiler_params=_cparams("arbitrary"),
        name="s5_chunk_scan",
    )(reset_f, reset_b, sf, sb, *mults)


def _s5_out_kernel(h_ref, xf_ref, xb_ref, w_ref, q_ref, y_ref):
    tc, gt, _ = xf_ref.shape
    xcat = jnp.concatenate([ref[:, r, :] for r in range(gt) for ref in (xf_ref, xb_ref)],
                           axis=1).astype(BF16)
    y = (jnp.dot(_chunk_rows(h_ref, tc), w_ref[0], preferred_element_type=F32)
         + jnp.dot(xcat, q_ref[0], preferred_element_type=F32))
    for s in range(SSM_CHUNK):
        y_ref[pl.ds(s, tc, stride=SSM_CHUNK), :] = y[:, s * LANES:(s + 1) * LANES]


def _s5_out(h, xf, xb, wbd, qbd, tc):
    t, d = h.shape
    n_blk, kdim, ncol = wbd.shape
    gt = SSM_GROUP_TILE
    half = xf.shape[-1] // n_blk
    nc = t // SSM_CHUNK
    xblk = pl.BlockSpec((tc, gt, half), lambda j, c: (c, 0, j))
    tok = pl.BlockSpec((tc * SSM_CHUNK, LANES), lambda j, c: (c, j))
    wspec = lambda rows: pl.BlockSpec((1, rows, ncol), lambda j, c: (j, 0, 0), pipeline_mode=pl.Buffered(1))
    return pl.pallas_call(
        _s5_out_kernel,
        grid=(n_blk, nc // tc),
        in_specs=[tok, xblk, xblk, wspec(kdim), wspec(qbd.shape[1])],
        out_specs=tok,
        out_shape=jax.ShapeDtypeStruct((t, d), F32),
        compiler_params=_cparams("arbitrary", "arbitrary"),
        name="s5_chunk_outputs",
    )(h, xf, xb, wbd, qbd)


def _s5_glu_kernel(x_ref, y_ref, mod_ref, g_ref, dskip_ref, w_ref, o_ref):
    d = x_ref.shape[1]
    x = x_ref[...]
    h = _rms_mod(x, g_ref[...], mod_ref[0, 1:2, :], mod_ref[0, 0:1, :])
    z = jax.nn.gelu(y_ref[...] + dskip_ref[...] * h)
    vg = jnp.dot(z.astype(BF16), w_ref[...], preferred_element_type=F32)
    m = vg[:, :d] * jax.nn.sigmoid(vg[:, d:])
    o_ref[...] = x + mod_ref[0, 2:3, :] * m


def _s5_glu(x, y, modl, g, d_skip, w_bf, seg_len, tm):
    t, d = x.shape
    per_seg = seg_len // tm
    tok = pl.BlockSpec((tm, d), lambda i: (i, 0))
    vec = pl.BlockSpec((1, d), lambda i: (0, 0))
    return pl.pallas_call(
        _s5_glu_kernel,
        grid=(t // tm,),
        in_specs=[tok, tok, pl.BlockSpec((1, SUBLANES, d), lambda i: (i // per_seg, 0, 0)), vec, vec,
                  pl.BlockSpec((d, 2 * d), lambda i: (0, 0))],
        out_specs=tok,
        out_shape=jax.ShapeDtypeStruct((t, d), F32),
        compiler_params=_cparams("arbitrary"),
        name="s5_glu_residual",
    )(x, y, modl, g, d_skip, w_bf)


def _s5_layer(x, modl, g, params, seg_len, seq_first, seq_last, tm):
    a_re, a_im, log_dt, b_re, b_im, c_re, c_im, d_skip, w_glu = params
    t, d = x.shape
    L = SSM_CHUNK
    n_g = d // GROUP_CH
    nc = t // L
    wbd, pbd, qbd, mults = _s5_tables(a_re, a_im, log_dt, b_re, b_im, c_re, c_im)
    h = _prenorm(x, modl, g, seg_len, tm)
    tc = math.gcd(nc, SSM_ROW_TILE)
    sf, sb = _s5_states(h, pbd, tc)
    chunks_per_seg = seg_len // L
    reset_f = jnp.repeat(jnp.asarray(seq_first, jnp.int32), chunks_per_seg) * (
        jnp.tile(jnp.arange(chunks_per_seg) == 0, len(seq_first))).astype(jnp.int32)
    reset_b = jnp.repeat(jnp.asarray(seq_last, jnp.int32), chunks_per_seg) * (
        jnp.tile(jnp.arange(chunks_per_seg) == chunks_per_seg - 1, len(seq_last))).astype(jnp.int32)
    xf, xb = _s5_scan(sf, sb, mults, reset_f, reset_b, math.gcd(nc, SSM_SCAN_TILE))
    y = _s5_out(h, xf, xb, wbd, qbd, tc)
    return _s5_glu(x, y, modl, g, d_skip.reshape(1, d), w_glu.astype(BF16), seg_len, tm)


def _router_kernel(x_ref, mod_ref, g_ref, w_ref, b_ref, h_ref, idx_ref, gate_ref, rank_ref, cnt_ref,
                   carry_ref):
    tm = x_ref.shape[0]
    n_e = w_ref.shape[0]

    @pl.when(pl.program_id(0) == 0)
    def _():
        carry_ref[...] = jnp.zeros(carry_ref.shape, F32)

    h = _rms_mod(x_ref[...], g_ref[...], mod_ref[0, 4:5, :], mod_ref[0, 3:4, :])
    h_ref[...] = h
    logits = lax.dot_general(w_ref[...], h, (((1,), (1,)), ((), ())), precision=HIGHEST,
                             preferred_element_type=F32) + b_ref[...]
    e_iota = lax.broadcasted_iota(jnp.int32, (n_e, tm), 0)
    work = logits
    chosen = jnp.zeros((n_e, tm), F32)
    vals, idxs = [], []
    for _ in range(TOP_K):
        m = jnp.max(work, axis=0, keepdims=True)
        ix = jnp.min(jnp.where(work == m, e_iota, n_e), axis=0, keepdims=True)
        hit = e_iota == ix
        work = jnp.where(hit, -jnp.inf, work)
        chosen = jnp.where(hit, 1.0, chosen)
        vals.append(m)
        idxs.append(ix)
    v = jnp.concatenate(vals, axis=0)
    ex = jnp.exp(v - v[0:1])
    gate_ref[...] = ex / jnp.sum(ex, axis=0, keepdims=True)
    idx_ref[...] = jnp.concatenate(idxs, axis=0)
    row = lax.broadcasted_iota(jnp.int32, (tm, tm), 0)
    col = lax.broadcasted_iota(jnp.int32, (tm, tm), 1)
    before = (row < col).astype(BF16)
    cum = jnp.dot(chosen.astype(BF16), before, preferred_element_type=F32) + carry_ref[...]
    ranks = [jnp.sum(jnp.where(e_iota == ix, cum, 0.0), axis=0, keepdims=True) for ix in idxs]
    rank_ref[...] = jnp.concatenate(ranks, axis=0).astype(jnp.int32)
    carry_ref[...] = carry_ref[...] + jnp.sum(chosen, axis=1, keepdims=True)
    cnt_ref[...] = jnp.broadcast_to(carry_ref[...], cnt_ref.shape).astype(jnp.int32)


def _router(x, modl, g, w_router_t, b_router, seg_len, tm):
    t, d = x.shape
    n_e = w_router_t.shape[0]
    per_seg = seg_len // tm
    sel = pl.BlockSpec((TOP_K, tm), lambda i: (0, i))
    return pl.pallas_call(
        _router_kernel,
        grid=(t // tm,),
        in_specs=[pl.BlockSpec((tm, d), lambda i: (i, 0)),
                  pl.BlockSpec((1, SUBLANES, d), lambda i: (i // per_seg, 0, 0)),
                  pl.BlockSpec((1, d), lambda i: (0, 0)),
                  pl.BlockSpec((n_e, d), lambda i: (0, 0)),
                  pl.BlockSpec((n_e, 1), lambda i: (0, 0))],
        out_specs=[pl.BlockSpec((tm, d), lambda i: (i, 0)), sel, sel, sel,
                   pl.BlockSpec((n_e, LANES), lambda i: (0, 0))],
        out_shape=[jax.ShapeDtypeStruct((t, d), F32),
                   jax.ShapeDtypeStruct((TOP_K, t), jnp.int32),
                   jax.ShapeDtypeStruct((TOP_K, t), F32),
                   jax.ShapeDtypeStruct((TOP_K, t), jnp.int32),
                   jax.ShapeDtypeStruct((n_e, LANES), jnp.int32)],
        scratch_shapes=[pltpu.VMEM((n_e, 1), F32)],
        compiler_params=_cparams("arbitrary"),
        name="moe_router",
    )(x, modl, g, w_router_t, b_router)


def _dispatch_kernel(pad_from_ref, pad_n_ref, dest_ref, h_ref, xs_hbm, zrow_ref, sem, zsem):
    tm = h_ref.shape[0]

    def zero_row(r):
        return pltpu.make_async_copy(zrow_ref, xs_hbm.at[pl.ds(r, 1), :], zsem)

    @pl.when(pl.program_id(0) == 0)
    def _():
        zrow_ref[...] = jnp.zeros(zrow_ref.shape, F32)
        for e in range(pad_n_ref.shape[0]):
            def fill(r, carry, e=e):
                zero_row(pad_from_ref[e] + r).start()
                return carry
            lax.fori_loop(0, pad_n_ref[e], fill, 0)
        for e in range(pad_n_ref.shape[0]):
            def drain(r, carry):
                zero_row(0).wait()
                return carry
            lax.fori_loop(0, pad_n_ref[e], drain, 0)

    def issue(j, carry):
        for k in range(TOP_K):
            pltpu.make_async_copy(h_ref.at[pl.ds(j, 1), :],
                                  xs_hbm.at[pl.ds(dest_ref[0, 0, j * TOP_K + k], 1), :], sem).start()
        return carry

    lax.fori_loop(0, tm, issue, 0, unroll=DMA_ISSUE_UNROLL)
    for _ in range(TOP_K):
        pltpu.make_async_copy(h_ref, xs_hbm.at[pl.ds(0, tm), :], sem).wait()


def _dispatch(h, dest_tk, pad_from, pad_n, n_rows, tm):
    t, d = h.shape
    return pl.pallas_call(
        _dispatch_kernel,
        grid_spec=pltpu.PrefetchScalarGridSpec(
            num_scalar_prefetch=2, grid=(t // tm,),
            in_specs=[pl.BlockSpec((1, 1, tm * TOP_K), lambda i, pf, pn: (i, 0, 0), memory_space=pltpu.SMEM),
                      pl.BlockSpec((tm, d), lambda i, pf, pn: (i, 0))],
            out_specs=pl.BlockSpec(memory_space=pl.ANY),
            scratch_shapes=[pltpu.VMEM((1, d), F32), pltpu.SemaphoreType.DMA(()),
                            pltpu.SemaphoreType.DMA(())]),
        out_shape=jax.ShapeDtypeStruct((n_rows, d), F32),
        compiler_params=_cparams("arbitrary"),
        name="moe_dispatch",
    )(pad_from, pad_n, dest_tk.reshape(t // tm, 1, tm * TOP_K), h)


def _expert_kernel(be_ref, bi_ref, nu_ref, x_ref, wgu_ref, bgu_ref, wd_ref, bd_ref, y_ref,
                   wgu_bf_ref, wd_bf_ref):
    del bi_ref
    n_ff = wd_ref.shape[2]
    b = pl.program_id(0)

    @pl.when((b == 0) | (be_ref[b] != be_ref[jnp.maximum(b - 1, 0)]))
    def _():
        wgu_bf_ref[...] = wgu_ref[0, 0].astype(BF16)
        wd_bf_ref[...] = wd_ref[0, 0].astype(BF16)

    @pl.when(b < nu_ref[0])
    def _():
        x = x_ref[...].astype(BF16)
        acc = jnp.zeros(y_ref.shape, F32) + bd_ref[0]
        n_chunks = n_ff // FF_CHUNK

        def gate_up(c):
            lo = c * FF_CHUNK
            return (jnp.dot(x, wgu_bf_ref[:, lo:lo + FF_CHUNK], preferred_element_type=F32)
                    + bgu_ref[0, :, lo:lo + FF_CHUNK],
                    jnp.dot(x, wgu_bf_ref[:, n_ff + lo:n_ff + lo + FF_CHUNK], preferred_element_type=F32)
                    + bgu_ref[0, :, n_ff + lo:n_ff + lo + FF_CHUNK])

        nxt = gate_up(0)
        for c in range(n_chunks):
            lo = c * FF_CHUNK
            g_part, u_part = nxt
            if c + 1 < n_chunks:
                nxt = gate_up(c + 1)
            g_part = jnp.minimum(g_part, SWIGLU_LIMIT)
            u_part = jnp.clip(u_part, -SWIGLU_LIMIT, SWIGLU_LIMIT)
            act = (u_part + 1.0) * g_part * jax.nn.sigmoid(SWIGLU_ALPHA * g_part)
            acc = acc + jnp.dot(act.astype(BF16), wd_bf_ref[lo:lo + FF_CHUNK, :],
                                preferred_element_type=F32)
        y_ref[...] = acc


def _experts(xs, blk_expert, blk_idx, n_used, layer, w_gu, b_gu, w_down, b_down, tme):
    nr, d = xs.shape
    _, n_e, _, two_f = w_gu.shape
    n_ff = two_f // 2
    nblk = nr // tme
    rows = pl.BlockSpec((tme, d), lambda b, be, bi, nu: (bi[b], 0))
    b_gu = b_gu[layer]
    b_down = b_down[layer]
    return pl.pallas_call(
        _expert_kernel,
        grid_spec=pltpu.PrefetchScalarGridSpec(
            num_scalar_prefetch=3, grid=(nblk,),
            in_specs=[rows,
                      pl.BlockSpec((1, 1, d, two_f), lambda b, be, bi, nu: (layer, be[b], 0, 0)),
                      pl.BlockSpec((1, 1, two_f), lambda b, be, bi, nu: (be[b], 0, 0)),
                      pl.BlockSpec((1, 1, n_ff, d), lambda b, be, bi, nu: (layer, be[b], 0, 0)),
                      pl.BlockSpec((1, 1, d), lambda b, be, bi, nu: (be[b], 0, 0))],
            out_specs=rows,
            scratch_shapes=[pltpu.VMEM((d, two_f), BF16), pltpu.VMEM((n_ff, d), BF16)]),
        out_shape=jax.ShapeDtypeStruct(xs.shape, F32),
        compiler_params=_cparams("arbitrary"),
        name="moe_experts",
    )(blk_expert, blk_idx, n_used, xs, w_gu, b_gu.reshape(n_e, 1, two_f), w_down,
      b_down.reshape(n_e, 1, d))


def _combine_kernel(dest_ref, dest_next_ref, x_ref, gate_ref, mod_ref, g_ref, ys_hbm, o_ref, buf, sem, *,
                    final_norm):
    tm = x_ref.shape[0]
    i = pl.program_id(0)
    cur = i % 2
    nxt = 1 - cur
    group = DMA_ISSUE_UNROLL

    def issue_token(idx_ref, j, slot):
        for k in range(TOP_K):
            pltpu.make_async_copy(ys_hbm.at[pl.ds(idx_ref[0, 0, j * TOP_K + k], 1), :],
                                  buf.at[slot, k, pl.ds(j, 1), :], sem.at[slot]).start()

    def wait_tile(slot):
        for k in range(TOP_K):
            pltpu.make_async_copy(ys_hbm.at[pl.ds(0, tm), :], buf.at[slot, k], sem.at[slot]).wait()

    @pl.when(i == 0)
    def _():
        def first(j, carry):
            issue_token(dest_ref, j, cur)
            return carry
        lax.fori_loop(0, tm, first, 0, unroll=group)

    wait_tile(cur)
    gate2 = mod_ref[0, 5:6, :]

    def body(jb, carry):
        base = pl.multiple_of(jb * group, group)
        for jj in range(group):
            issue_token(dest_next_ref, base + jj, nxt)
        rows = pl.ds(base, group)
        gates = gate_ref[rows, :]
        moe = gates[:, 0:1] * buf[cur, 0, rows, :]
        for k in range(1, TOP_K):
            moe = moe + gates[:, k:k + 1] * buf[cur, k, rows, :]
        x = x_ref[rows, :] + gate2 * moe
        if final_norm:
            ms = jnp.mean(x * x, axis=-1, keepdims=True)
            x = x * lax.rsqrt(ms + NORM_EPS) * g_ref[...]
        o_ref[rows, :] = x
        return carry

    lax.fori_loop(0, tm // group, body, 0)

    @pl.when(i == pl.num_programs(0) - 1)
    def _():
        wait_tile(nxt)


def _combine(x, ys, dest_tk, gates, modl, final_g, seg_len, tm, final_norm, row0=0, rows=None):
    t, d = x.shape
    rows = t if rows is None else rows
    per_seg = seg_len // tm
    b0 = row0 // tm
    n_steps = rows // tm
    return pl.pallas_call(
        functools.partial(_combine_kernel, final_norm=final_norm),
        grid=(n_steps,),
        in_specs=[pl.BlockSpec((1, 1, tm * TOP_K), lambda i: (i + b0, 0, 0), memory_space=pltpu.SMEM),
                  pl.BlockSpec((1, 1, tm * TOP_K), lambda i: (jnp.minimum(i + 1, n_steps - 1) + b0, 0, 0),
                               memory_space=pltpu.SMEM),
                  pl.BlockSpec((tm, d), lambda i: (i + b0, 0)),
                  pl.BlockSpec((tm, TOP_K), lambda i: (i + b0, 0)),
                  pl.BlockSpec((1, SUBLANES, d), lambda i: ((i + b0) // per_seg, 0, 0)),
                  pl.BlockSpec((1, d), lambda i: (0, 0)),
                  pl.BlockSpec(memory_space=pl.ANY)],
        out_specs=pl.BlockSpec((tm, d), lambda i: (i, 0)),
        out_shape=jax.ShapeDtypeStruct((rows, d), F32),
        scratch_shapes=[pltpu.VMEM((2, TOP_K, tm, d), F32), pltpu.SemaphoreType.DMA((2,))],
        compiler_params=_cparams("arbitrary"),
        name="moe_combine",
    )(dest_tk.reshape(t // tm, 1, tm * TOP_K), dest_tk.reshape(t // tm, 1, tm * TOP_K), x, gates, modl,
      final_g, ys)


def _moe_dispatch_and_experts(x, modl, g, w_router, b_router, layer, w_gu, b_gu, w_down, b_down, seg_len, tm):
    t, d = x.shape
    n_e = w_router.shape[1]
    tme = math.gcd(t * TOP_K, EXPERT_ROW_TILE)
    h, idx_t, gate_t, rank_t, cnt = _router(x, modl, g, w_router.T, b_router.reshape(n_e, 1), seg_len, tm)
    counts = cnt[:, 0]
    padded = (counts + tme - 1) // tme * tme
    pad_end = jnp.cumsum(padded)
    pad_start = pad_end - padded
    nblk = (t * TOP_K) // tme + n_e
    n_used = (pad_end[-1] // tme).astype(jnp.int32).reshape(1)
    blk_idx = jnp.minimum(jnp.arange(nblk, dtype=jnp.int32), n_used[0] - 1)
    blk_expert = jnp.minimum(jnp.sum((pad_end[None, :] <= (blk_idx * tme)[:, None]).astype(jnp.int32), axis=1),
                             n_e - 1)
    start_of = jnp.sum(jnp.where(idx_t[None] == jnp.arange(n_e, dtype=jnp.int32)[:, None, None],
                                 pad_start[:, None, None], 0), axis=0)
    dest = (start_of + rank_t).astype(jnp.int32)
    dest_tk = dest.T.reshape(-1)
    xs = _dispatch(h, dest_tk, (pad_start + counts).astype(jnp.int32), (padded - counts).astype(jnp.int32),
                   nblk * tme, tm)
    ys = _experts(xs, blk_expert, blk_idx, n_used, layer, w_gu, b_gu, w_down, b_down, tme)
    return ys, dest_tk, gate_t.T


def kernel(x_prompt, x_sample, c_prompt, c_sample, norm1_g, norm2_g, final_g, w_ada, b_ada, w_qkv, w_o,
           lam_q1, lam_k1, lam_q2, lam_k2, subln_g, ssm_a_re, ssm_a_im, ssm_log_dt, ssm_b_re, ssm_b_im,
           ssm_c_re, ssm_c_im, ssm_d, ssm_w_glu, w_router, b_router, w_gu, b_gu, w_down, b_down):
    n_b, seg_len, d = x_prompt.shape
    n_bs, s_len, _ = x_sample.shape
    depth = w_ada.shape[0]
    assert s_len % seg_len == 0 and s_len // seg_len in (1, 2)
    per_sample = s_len // seg_len
    n_seg = n_b + n_bs * per_sample
    t = n_seg * seg_len
    tm = math.gcd(seg_len, TOKEN_TILE)

    seg_seq = list(range(n_b)) + [n_b + j for j in range(n_bs) for _ in range(per_sample)]
    seg_pos = [0] * n_b + [r for _ in range(n_bs) for r in range(per_sample)]
    seg_a = list(range(n_b)) + [n_b + j * per_sample for j in range(n_bs) for _ in range(per_sample)]
    seg_b = list(range(n_b)) + [n_b + j * per_sample + per_sample - 1 for j in range(n_bs) for _ in range(per_sample)]
    two = [0] * n_b + [int(per_sample == 2)] * (n_bs * per_sample)
    seq_first = [1] * n_b + [int(r == 0) for _ in range(n_bs) for r in range(per_sample)]
    seq_last = [1] * n_b + [int(r == per_sample - 1) for _ in range(n_bs) for r in range(per_sample)]
    per_seg = seg_len // tm
    pos_blk = jnp.asarray([seg_pos[s] * per_seg + r for s in range(n_seg) for r in range(per_seg)], jnp.int32)

    x = None
    x_parts = (x_prompt.reshape(n_b * seg_len, d), x_sample.reshape(n_bs * s_len, d))

    n_c = n_b + n_bs
    c_rows = -(-n_c // SUBLANES) * SUBLANES
    c_all = jnp.concatenate([c_prompt, c_sample, jnp.zeros((c_rows - n_c, d), F32)], axis=0)
    mod = _ada(c_all, w_ada, b_ada)
    mod = mod[:, jnp.asarray(seg_seq)].reshape(depth, n_seg, 6, d)
    mod = jnp.concatenate([mod, jnp.zeros((depth, n_seg, SUBLANES - 6, d), F32)], axis=2)

    inv_freq = ROPE_THETA ** (-jnp.arange(0, HEAD_DIM, 2, dtype=F32) / HEAD_DIM)
    ang = jnp.arange(s_len, dtype=F32)[:, None] * inv_freq[None, :]
    cos_t = jnp.tile(jnp.cos(ang), (1, 2 * LANES // HEAD_DIM))
    sin_t = jnp.tile(jnp.concatenate([-jnp.sin(ang), jnp.sin(ang)], axis=1), (1, LANES // HEAD_DIM))

    out_p = out_s = None
    for i in range(depth):
        modl = mod[i]
        g1 = norm1_g[i].reshape(1, d)
        g2 = norm2_g[i].reshape(1, d)
        j = i // 2
        if i % 2 == 0:
            lambda_init = 0.8 - 0.6 * math.exp(-0.3 * i)
            xa, xb = x_parts if x is None else (x, None)
            q, k, vt = _qkv(xa, xb, modl, g1, w_qkv[j], cos_t, sin_t, pos_blk, seg_len, tm)
            lam_pack = jnp.zeros((SUBLANES, 2 * HEAD_DIM), F32).at[0:4, 0:HEAD_DIM].set(
                jnp.stack([lam_q1[j], lam_k1[j], lam_q2[j], lam_k2[j]]).astype(F32))
            o = _flash(q, k, vt, lam_pack, subln_g[j].reshape(2 * HEAD_DIM, 1).astype(F32),
                       jnp.asarray(seg_a, jnp.int32), jnp.asarray(seg_b, jnp.int32),
                       jnp.asarray(two, jnp.int32), seg_len, lambda_init)
            x = _proj_res(o, w_o[j].astype(BF16), xa, xb, modl, seg_len, tm, gate_row=2)
        else:
            if x is None:
                x = jnp.concatenate(x_parts, axis=0)
            params = (ssm_a_re[j], ssm_a_im[j], ssm_log_dt[j], ssm_b_re[j], ssm_b_im[j], ssm_c_re[j],
                      ssm_c_im[j], ssm_d[j], ssm_w_glu[j])
            x = _s5_layer(x, modl, g1, params, seg_len, seq_first, seq_last, tm)
        ys, dest_tk, gates = _moe_dispatch_and_experts(x, modl, g2, w_router[i], b_router[i], i, w_gu, b_gu,
                                                       w_down, b_down, seg_len, tm)
        fg = final_g.reshape(1, d)
        if i == depth - 1:
            out_p = _combine(x, ys, dest_tk, gates, modl, fg, seg_len, tm, True, 0, n_b * seg_len)
            out_s = _combine(x, ys, dest_tk, gates, modl, fg, seg_len, tm, True, n_b * seg_len, n_bs * s_len)
        else:
            x = _combine(x, ys, dest_tk, gates, modl, fg, seg_len, tm, False)
    return out_p.reshape(n_b, seg_len, d), out_s.reshape(n_bs, s_len, d)
```

```python
import functools
import math

import jax
import jax.numpy as jnp
from jax import lax
from jax.experimental import pallas as pl
from jax.experimental.pallas import tpu as pltpu

F32 = jnp.float32
BF16 = jnp.bfloat16
HIGHEST = lax.Precision.HIGHEST

N_DIFF_HEADS = 8
HEAD_DIM = 64
ROPE_THETA = 10000.0
SUBLN_EPS = 1e-5
NORM_EPS = 1e-6
GROUP_CH = 16
STATE_DIM = 64
N_EXPERTS = 32
TOP_K = 4
SWIGLU_ALPHA = 1.702
SWIGLU_LIMIT = 7.0

LANES = 128
SUBLANES = 8
VMEM_LIMIT_BYTES = 56 * 1024 * 1024

TOKEN_TILE = 512
ATT_Q_TILE = 512
ATT_KV_TILE = 512
ATT_UNROLL = 8
VT_ONES_ROWS = 16
LOG2_E = math.log2(math.e)
SSM_CHUNK = 16
SSM_GROUP_TILE = 8
SSM_ROW_TILE = 256
SSM_SCAN_TILE = 64
EXPERT_ROW_TILE = 512
FF_CHUNK = 512
ROW_DMA_TILE = 1024
DMA_ISSUE_UNROLL = 8


def _cparams(*sem):
    return pltpu.CompilerParams(dimension_semantics=sem, vmem_limit_bytes=VMEM_LIMIT_BYTES)


def _rms_mod(x, g, scale, shift):
    ms = jnp.mean(x * x, axis=-1, keepdims=True)
    return x * lax.rsqrt(ms + NORM_EPS) * g * (1.0 + scale) + shift


def _ada_kernel(c_ref, w_ref, b_ref, o_ref):
    c = c_ref[...]
    cond = c * jax.nn.sigmoid(c)
    o_ref[0] = jnp.dot(cond, w_ref[0], precision=HIGHEST, preferred_element_type=F32) + b_ref[0]


def _ada(c_pad, w_ada, b_ada):
    depth, d, n6 = w_ada.shape
    rows = c_pad.shape[0]
    tn = 1536 if n6 % 1536 == 0 else n6
    return pl.pallas_call(
        _ada_kernel,
        grid=(depth, n6 // tn),
        in_specs=[pl.BlockSpec((rows, d), lambda i, j: (0, 0)),
                  pl.BlockSpec((1, d, tn), lambda i, j: (i, 0, j)),
                  pl.BlockSpec((1, 1, tn), lambda i, j: (i, 0, j))],
        out_specs=pl.BlockSpec((1, rows, tn), lambda i, j: (i, 0, j)),
        out_shape=jax.ShapeDtypeStruct((depth, rows, n6), F32),
        compiler_params=_cparams("arbitrary", "arbitrary"),
        name="ada_mod",
    )(c_pad, w_ada, b_ada.reshape(depth, 1, n6))


def _token_rows(xa_ref, xb_ref, n_a):
    return jnp.where(pl.program_id(0) < n_a, xa_ref[...], xb_ref[...])


def _token_specs(xa, xb, tm, index_args):
    n_a = xa.shape[0] // tm
    d = xa.shape[1]
    if index_args == 1:
        return n_a, [pl.BlockSpec((tm, d), lambda i: (jnp.minimum(i, n_a - 1), 0)),
                     pl.BlockSpec((tm, d), lambda i: (jnp.maximum(i - n_a, 0), 0))]
    return n_a, [pl.BlockSpec((tm, d), lambda i, p: (jnp.minimum(i, n_a - 1), 0)),
                 pl.BlockSpec((tm, d), lambda i, p: (jnp.maximum(i - n_a, 0), 0))]


def _qkv_kernel(pos_ref, xa_ref, xb_ref, mod_ref, g_ref, w_ref, wvt_ref, cos_ref, sin_ref, q_ref, k_ref,
                vt_ref, *, n_a):
    del pos_ref
    d = xa_ref.shape[1]
    h = _rms_mod(_token_rows(xa_ref, xb_ref, n_a), g_ref[...], mod_ref[0, 1:2, :],
                 mod_ref[0, 0:1, :]).astype(BF16)
    qkv = jnp.dot(h, w_ref[...], preferred_element_type=F32)
    vt = lax.dot_general(wvt_ref[...], h, (((1,), (1,)), ((), ())), preferred_element_type=F32)
    ones = jnp.ones((VT_ONES_ROWS, vt.shape[1]), BF16)
    for hd in range(vt_ref.shape[0]):
        vt_ref[hd, 0, 0:LANES, :] = vt[hd * LANES:(hd + 1) * LANES, :].astype(BF16)
        vt_ref[hd, 0, LANES:LANES + VT_ONES_ROWS, :] = ones
    cos = cos_ref[...]
    sin = sin_ref[...]
    lane = lax.broadcasted_iota(jnp.int32, cos.shape, 1)
    first_half = (lane % HEAD_DIM) < (HEAD_DIM // 2)

    def rotary(t):
        partner = jnp.where(first_half,
                            pltpu.roll(t, LANES - HEAD_DIM // 2, 1),
                            pltpu.roll(t, HEAD_DIM // 2, 1))
        return t * cos + partner * sin

    for j in range(d // LANES):
        sl = slice(j * LANES, (j + 1) * LANES)
        q_ref[:, sl] = (rotary(qkv[:, sl]) * (HEAD_DIM ** -0.5 * LOG2_E)).astype(BF16)
        k_ref[:, sl] = rotary(qkv[:, d + j * LANES:d + (j + 1) * LANES]).astype(BF16)


def _qkv(xa, xb, modl, g, w_qkv, cos_t, sin_t, pos_blk, seg_len, tm):
    d = xa.shape[1]
    t = xa.shape[0] + (0 if xb is None else xb.shape[0])
    xb = xa if xb is None else xb
    n_a, x_specs = _token_specs(xa, xb, tm, 2)
    per_seg = seg_len // tm
    n_vh = d // LANES
    tk = math.gcd(seg_len, ATT_KV_TILE)
    per_kv = tk // tm
    tok = lambda i, p: (i, 0)
    out = jax.ShapeDtypeStruct((t, d), BF16)
    w_qk = w_qkv[:, :2 * d].astype(BF16)
    w_vt = w_qkv[:, 2 * d:].T.astype(BF16)
    return pl.pallas_call(
        functools.partial(_qkv_kernel, n_a=n_a),
        grid_spec=pltpu.PrefetchScalarGridSpec(
            num_scalar_prefetch=1, grid=(t // tm,),
            in_specs=x_specs + [
                      pl.BlockSpec((1, SUBLANES, d), lambda i, p: (i // per_seg, 0, 0)),
                      pl.BlockSpec((1, d), lambda i, p: (0, 0)),
                      pl.BlockSpec((d, 2 * d), lambda i, p: (0, 0)),
                      pl.BlockSpec((d, d), lambda i, p: (0, 0)),
                      pl.BlockSpec((tm, LANES), lambda i, p: (p[i], 0)),
                      pl.BlockSpec((tm, LANES), lambda i, p: (p[i], 0))],
            out_specs=[pl.BlockSpec((tm, d), tok), pl.BlockSpec((tm, d), tok),
                       pl.BlockSpec((n_vh, 1, LANES + VT_ONES_ROWS, tm),
                                    lambda i, p: (0, i // per_kv, 0, i % per_kv))]),
        out_shape=[out, out, jax.ShapeDtypeStruct((n_vh, t // tk, LANES + VT_ONES_ROWS, tk), BF16)],
        compiler_params=_cparams("arbitrary"),
        name="qkv_rotary",
    )(pos_blk, xa, xb, modl, g, w_qk, w_vt, cos_t, sin_t)


def _flash_kernel(sa_ref, sb_ref, two_ref, q_ref, ka_ref, vta_ref, kb_ref, vtb_ref, lam_ref, g_ref,
                  o_ref, qt_ref, m_ref, acc_ref, *, tk, unroll, lambda_init):
    del sa_ref, sb_ref
    seg = pl.program_id(0)
    qt = q_ref[...].astype(F32).T
    chan = lax.broadcasted_iota(jnp.int32, qt.shape, 0)
    qt_ref[0] = jnp.where(chan < HEAD_DIM, qt, 0.0).astype(BF16)
    qt_ref[1] = jnp.where(chan >= HEAD_DIM, qt, 0.0).astype(BF16)
    m_ref[...] = jnp.full(m_ref.shape, -jnp.inf, F32)
    acc_ref[...] = jnp.zeros(acc_ref.shape, F32)

    def sweep(k_ref, vt_ref):
        units = [(tile, g) for tile in range(unroll) for g in range(2)]

        half = tk // 2

        def half_scores(jj, u, r):
            tile, g = units[u]
            off = pl.multiple_of((jj * unroll + tile) * tk, tk)
            return jnp.dot(k_ref[pl.ds(off + r * half, half), :], qt_ref[g], preferred_element_type=F32)

        def values(jj, u, p, alpha):
            tile, g = units[u]
            acc_ref[g] = alpha * acc_ref[g] + jnp.dot(vt_ref[0, jj * unroll + tile], p,
                                                      preferred_element_type=F32)

        def body(jj, carry):
            s_next = [half_scores(jj, 0, 0), half_scores(jj, 0, 1)]
            pending = None
            for u in range(len(units)):
                g = units[u][1]
                s_cur = s_next
                more = u + 1 < len(units)
                if more:
                    s_next = [half_scores(jj, u + 1, 0)]
                m_prev = m_ref[g]
                m_new = m_prev
                for s in s_cur:
                    m_new = jnp.maximum(m_new, jnp.max(s, axis=0, keepdims=True))
                m_ref[g] = m_new
                if pending is not None:
                    values(jj, *pending)
                if more:
                    s_next.append(half_scores(jj, u + 1, 1))
                p = jnp.concatenate([jnp.exp2((s - m_new).astype(BF16)) for s in s_cur], axis=0)
                pending = (u, p, jnp.exp2(m_prev - m_new))
            values(jj, *pending)
            return carry
        lax.fori_loop(0, k_ref.shape[0] // (tk * unroll), body, 0)

    sweep(ka_ref, vta_ref)

    @pl.when(two_ref[seg] == 1)
    def _():
        sweep(kb_ref, vtb_ref)

    o1 = acc_ref[0, 0:LANES, :] / acc_ref[0, LANES:LANES + 1, :]
    o2 = acc_ref[1, 0:LANES, :] / acc_ref[1, LANES:LANES + 1, :]
    lam = (jnp.exp(jnp.sum(lam_ref[0:1, :] * lam_ref[1:2, :], axis=-1, keepdims=True))
           - jnp.exp(jnp.sum(lam_ref[2:3, :] * lam_ref[3:4, :], axis=-1, keepdims=True))
           + lambda_init)
    ot = o1 - lam * o2
    ms = jnp.mean(ot * ot, axis=0, keepdims=True)
    ot = ot * lax.rsqrt(ms + SUBLN_EPS) * g_ref[...] * (1.0 - lambda_init)
    o_ref[...] = ot.T.astype(BF16)


def _flash(q, k, vt, lam_pack, subln_g_col, seg_a, seg_b, two, seg_len, lambda_init):
    t, d = q.shape
    nseg = t // seg_len
    n_vh, _, v_rows, tk = vt.shape
    width = 2 * HEAD_DIM
    per_seg = seg_len // tk
    tq = min(ATT_Q_TILE, seg_len)
    nq = seg_len // tq
    kern = functools.partial(_flash_kernel, tk=tk, unroll=min(ATT_UNROLL, per_seg), lambda_init=lambda_init)
    k_a = pl.BlockSpec((seg_len, width), lambda s, h, i, sa, sb, tw: (sa[s], h))
    k_b = pl.BlockSpec((seg_len, width), lambda s, h, i, sa, sb, tw: (sb[s], h))
    vt_a = pl.BlockSpec((1, per_seg, v_rows, tk), lambda s, h, i, sa, sb, tw: (h, sa[s], 0, 0))
    vt_b = pl.BlockSpec((1, per_seg, v_rows, tk), lambda s, h, i, sa, sb, tw: (h, sb[s], 0, 0))
    qo = pl.BlockSpec((tq, width), lambda s, h, i, sa, sb, tw: (s * nq + i, h))
    return pl.pallas_call(
        kern,
        grid_spec=pltpu.PrefetchScalarGridSpec(
            num_scalar_prefetch=3, grid=(nseg, n_vh, nq),
            in_specs=[qo, k_a, vt_a, k_b, vt_b,
                      pl.BlockSpec((SUBLANES, width), lambda s, h, i, sa, sb, tw: (0, 0)),
                      pl.BlockSpec((width, 1), lambda s, h, i, sa, sb, tw: (0, 0))],
            out_specs=qo,
            scratch_shapes=[pltpu.VMEM((2, width, tq), BF16),
                            pltpu.VMEM((2, 1, tq), F32),
                            pltpu.VMEM((2, v_rows, tq), F32)]),
        out_shape=jax.ShapeDtypeStruct((t, d), BF16),
        compiler_params=_cparams("arbitrary", "arbitrary", "arbitrary"),
        name="diff_flash_attention",
    )(seg_a, seg_b, two, q, k, vt, k, vt, lam_pack, subln_g_col)


def _proj_res_kernel(a_ref, w_ref, xa_ref, xb_ref, mod_ref, o_ref, *, gate_row, n_a):
    y = jnp.dot(a_ref[...], w_ref[...], preferred_element_type=F32)
    o_ref[...] = _token_rows(xa_ref, xb_ref, n_a) + mod_ref[0, gate_row:gate_row + 1, :] * y


def _proj_res(a, w_bf, xa, xb, modl, seg_len, tm, gate_row):
    d = xa.shape[1]
    t = xa.shape[0] + (0 if xb is None else xb.shape[0])
    xb = xa if xb is None else xb
    n_a, x_specs = _token_specs(xa, xb, tm, 1)
    per_seg = seg_len // tm
    return pl.pallas_call(
        functools.partial(_proj_res_kernel, gate_row=gate_row, n_a=n_a),
        grid=(t // tm,),
        in_specs=[pl.BlockSpec((tm, a.shape[1]), lambda i: (i, 0)),
                  pl.BlockSpec(w_bf.shape, lambda i: (0, 0))] + x_specs + [
                  pl.BlockSpec((1, SUBLANES, d), lambda i: (i // per_seg, 0, 0))],
        out_specs=pl.BlockSpec((tm, d), lambda i: (i, 0)),
        out_shape=jax.ShapeDtypeStruct((t, d), F32),
        compiler_params=_cparams("arbitrary"),
        name="proj_residual",
    )(a, w_bf, xa, xb, modl)


def _prenorm_kernel(x_ref, mod_ref, g_ref, o_ref):
    o_ref[...] = _rms_mod(x_ref[...], g_ref[...], mod_ref[0, 1:2, :], mod_ref[0, 0:1, :])


def _prenorm(x, modl, g, seg_len, tm):
    t, d = x.shape
    per_seg = seg_len // tm
    return pl.pallas_call(
        _prenorm_kernel,
        grid=(t // tm,),
        in_specs=[pl.BlockSpec((tm, d), lambda i: (i, 0)),
                  pl.BlockSpec((1, SUBLANES, d), lambda i: (i // per_seg, 0, 0)),
                  pl.BlockSpec((1, d), lambda i: (0, 0))],
        out_specs=pl.BlockSpec((tm, d), lambda i: (i, 0)),
        out_shape=jax.ShapeDtypeStruct((t, d), F32),
        compiler_params=_cparams("arbitrary"),
        name="s5_prenorm",
    )(x, modl, g)


def _block_diag_kernel(a_ref, o_ref, *, rows_interleaved, cols_interleaved):
    gt, n_r, n_c = a_ref.shape
    gc = GROUP_CH
    o_ref[...] = jnp.zeros(o_ref.shape, o_ref.dtype)
    src = lax.broadcasted_iota(jnp.int32, (n_c, gt * n_c), 0)
    dst = lax.broadcasted_iota(jnp.int32, (n_c, gt * n_c), 1)
    for r in range(gt):
        a = a_ref[r]
        if cols_interleaved:
            place = (dst == (src // gc) * (gt * gc) + r * gc + src % gc).astype(a.dtype)
            a = jnp.dot(a, place, preferred_element_type=F32).astype(o_ref.dtype)
            cols = slice(0, gt * n_c)
        else:
            cols = slice(r * n_c, (r + 1) * n_c)
        if rows_interleaved:
            for s in range(n_r // gc):
                o_ref[0, (s * gt + r) * gc:(s * gt + r + 1) * gc, cols] = a[s * gc:(s + 1) * gc, :]
        else:
            o_ref[0, r * n_r:(r + 1) * n_r, cols] = a


def _block_diag(tab, rows_interleaved, cols_interleaved):
    n_g, n_r, n_c = tab.shape
    gt = SSM_GROUP_TILE
    return pl.pallas_call(
        functools.partial(_block_diag_kernel, rows_interleaved=rows_interleaved,
                          cols_interleaved=cols_interleaved),
        grid=(n_g // gt,),
        in_specs=[pl.BlockSpec((gt, n_r, n_c), lambda j: (j, 0, 0))],
        out_specs=pl.BlockSpec((1, gt * n_r, gt * n_c), lambda j: (j, 0, 0)),
        out_shape=jax.ShapeDtypeStruct((n_g // gt, gt * n_r, gt * n_c), tab.dtype),
        compiler_params=_cparams("arbitrary"),
        name="s5_block_diag",
    )(tab)


def _s5_tables(a_re, a_im, log_dt, b_re, b_im, c_re, c_im):
    f32 = F32
    L = SSM_CHUNK
    ar, ai = a_re.astype(f32), a_im.astype(f32)
    n_g, n_p = ar.shape[1], ar.shape[2]
    n_h = b_re.shape[-1]
    dt = jnp.exp(log_dt.astype(f32))[..., None]
    mag = jnp.exp(ar * dt)
    lr, li = mag * jnp.cos(ai * dt), mag * jnp.sin(ai * dt)
    den = ar * ar + ai * ai
    fr = ((lr - 1.0) * ar + li * ai) / den
    fi = (li * ar - (lr - 1.0) * ai) / den
    br, bi = b_re.astype(f32), b_im.astype(f32)
    bbr = fr[..., None] * br - fi[..., None] * bi
    bbi = fr[..., None] * bi + fi[..., None] * br
    cr, ci = c_re.astype(f32), c_im.astype(f32)
    kk = jnp.arange(L + 1, dtype=f32)
    pmag = jnp.exp(ar[..., None] * dt[..., None] * kk)
    pang = ai[..., None] * dt[..., None] * kk
    pr, pi = pmag * jnp.cos(pang), pmag * jnp.sin(pang)
    er = cr[..., None] * pr[:, :, None] - ci[..., None] * pi[:, :, None]
    ei = cr[..., None] * pi[:, :, None] + ci[..., None] * pr[:, :, None]
    klag = (jnp.einsum('dgopk,dgpi->dgkoi', er[..., :L], bbr, precision=HIGHEST)
            - jnp.einsum('dgopk,dgpi->dgkoi', ei[..., :L], bbi, precision=HIGHEST))
    kf, kb = klag[0], klag[1]
    kfull = jnp.concatenate([kb[:, 1:][:, ::-1], (kf[:, 0] + kb[:, 0])[:, None], kf[:, 1:]], axis=1)
    s_idx = jnp.arange(L)[:, None]
    t_idx = jnp.arange(L)[None, :]
    w = kfull[:, t_idx - s_idx + L - 1]
    w = jnp.transpose(w, (0, 1, 4, 2, 3)).reshape(n_g, L * n_h, L * n_h)
    def in_to_state(d, pw_idx):
        pwr = jnp.take(pr[d], pw_idx, axis=-1)
        pwi = jnp.take(pi[d], pw_idx, axis=-1)
        re = pwr[..., None] * bbr[d][:, :, None, :] - pwi[..., None] * bbi[d][:, :, None, :]
        im = pwr[..., None] * bbi[d][:, :, None, :] + pwi[..., None] * bbr[d][:, :, None, :]
        m = jnp.concatenate([re, im], axis=1)
        return jnp.transpose(m, (0, 2, 3, 1)).reshape(n_g, L * n_h, 2 * n_p)
    pcat = jnp.concatenate([in_to_state(0, L - 1 - jnp.arange(L)), in_to_state(1, jnp.arange(L))], axis=-1)
    def state_to_out(d, pw_idx):
        e_r = jnp.take(er[d], pw_idx, axis=-1)
        e_i = jnp.take(ei[d], pw_idx, axis=-1)
        m = jnp.concatenate([e_r, -e_i], axis=2)
        return jnp.transpose(m, (0, 2, 3, 1)).reshape(n_g, 2 * n_p, L * n_h)
    qcat = jnp.concatenate([state_to_out(0, jnp.arange(L) + 1), state_to_out(1, L - jnp.arange(L))], axis=1)

    def scan_mult(d):
        a_r, a_i = pr[d][..., L], pi[d][..., L]
        m1 = jnp.concatenate([a_r, a_r], axis=-1)
        m2 = jnp.concatenate([-a_i, a_i], axis=-1)
        lay = lambda m: jnp.transpose(m.reshape(n_g // SSM_GROUP_TILE, SSM_GROUP_TILE, 2 * n_p),
                                      (1, 0, 2)).reshape(SSM_GROUP_TILE, -1)
        return lay(m1), lay(m2)
    af1, af2 = scan_mult(0)
    ab1, ab2 = scan_mult(1)
    wbd = _block_diag(w.astype(BF16), True, True)
    pbd = _block_diag(pcat.astype(BF16), True, False)
    qbd = _block_diag(qcat.astype(BF16), False, True)
    return wbd, pbd, qbd, (af1, af2, ab1, ab2)


def _chunk_rows(h_ref, tc):
    return jnp.concatenate([h_ref[pl.ds(s, tc, stride=SSM_CHUNK), :] for s in range(SSM_CHUNK)],
                           axis=1).astype(BF16)


def _s5_states_kernel(h_ref, p_ref, sf_ref, sb_ref):
    tc, gt, half = sf_ref.shape
    st = jnp.dot(_chunk_rows(h_ref, tc), p_ref[0], preferred_element_type=F32)
    for r in range(gt):
        sf_ref[:, r, :] = st[:, r * 2 * half:r * 2 * half + half]
        sb_ref[:, r, :] = st[:, r * 2 * half + half:(r + 1) * 2 * half]


def _s5_states(h, pbd, tc):
    t, d = h.shape
    n_blk, kdim, ncol = pbd.shape
    gt = SSM_GROUP_TILE
    half = ncol // (2 * gt)
    nc = t // SSM_CHUNK
    out = jax.ShapeDtypeStruct((nc, gt, n_blk * half), F32)
    return pl.pallas_call(
        _s5_states_kernel,
        grid=(n_blk, nc // tc),
        in_specs=[pl.BlockSpec((tc * SSM_CHUNK, LANES), lambda j, c: (c, j)),
                  pl.BlockSpec((1, kdim, ncol), lambda j, c: (j, 0, 0))],
        out_specs=[pl.BlockSpec((tc, gt, half), lambda j, c: (c, 0, j))] * 2,
        out_shape=[out, out],
        compiler_params=_cparams("arbitrary", "arbitrary"),
        name="s5_chunk_states",
    )(h, pbd)


def _s5_scan_kernel(rf_ref, rb_ref, sf_ref, sb_ref, af1_ref, af2_ref, ab1_ref, ab2_ref,
                    xf_ref, xb_ref, cf_ref, cb_ref, *, tcs, nblk):
    i = pl.program_id(0)

    @pl.when(i == 0)
    def _():
        cf_ref[...] = jnp.zeros(cf_ref.shape, F32)
        cb_ref[...] = jnp.zeros(cb_ref.shape, F32)

    width = cf_ref.shape[-1]

    def swap(x):
        return jnp.concatenate(
            [pltpu.roll(x[:, j * LANES:(j + 1) * LANES], LANES // 2, 1) for j in range(width // LANES)],
            axis=1)

    af1, af2 = af1_ref[...], af2_ref[...]
    ab1, ab2 = ab1_ref[...], ab2_ref[...]

    def step(r, carry):
        cf, cfs, cb, cbs = carry
        keep = (1 - rf_ref[i * tcs + r]).astype(F32)
        cf, cfs = cf * keep, cfs * keep
        xf_ref[r] = cf
        s = sf_ref[r]
        cf, cfs = af1 * cf + af2 * cfs + s, af1 * cfs - af2 * cf + swap(s)
        rb = tcs - 1 - r
        keep = (1 - rb_ref[(nblk - 1 - i) * tcs + rb]).astype(F32)
        cb, cbs = cb * keep, cbs * keep
        xb_ref[rb] = cb
        s = sb_ref[rb]
        cb, cbs = ab1 * cb + ab2 * cbs + s, ab1 * cbs - ab2 * cb + swap(s)
        return cf, cfs, cb, cbs

    cf0, cb0 = cf_ref[...], cb_ref[...]
    cf, _, cb, _ = lax.fori_loop(0, tcs, step, (cf0, swap(cf0), cb0, swap(cb0)), unroll=2)
    cf_ref[...] = cf
    cb_ref[...] = cb


def _s5_scan(sf, sb, mults, reset_f, reset_b, tcs):
    nc, gt, width = sf.shape
    nblk = nc // tcs
    fblk = pl.BlockSpec((tcs, gt, width), lambda i, a, b: (i, 0, 0))
    bblk = pl.BlockSpec((tcs, gt, width), lambda i, a, b: (nblk - 1 - i, 0, 0))
    tab = pl.BlockSpec((gt, width), lambda i, a, b: (0, 0))
    out = jax.ShapeDtypeStruct((nc, gt, width), F32)
    return pl.pallas_call(
        functools.partial(_s5_scan_kernel, tcs=tcs, nblk=nblk),
        grid_spec=pltpu.PrefetchScalarGridSpec(
            num_scalar_prefetch=2, grid=(nblk,),
            in_specs=[fblk, bblk, tab, tab, tab, tab],
            out_specs=[fblk, bblk],
            scratch_shapes=[pltpu.VMEM((gt, width), F32), pltpu.VMEM((gt, width), F32)]),
        out_shape=[out, out],
        compiler_params=_cparams("arbitrary"),
        name="s5_chunk_scan",
    )(reset_f, reset_b, sf, sb, *mults)


def _s5_out_kernel(h_ref, xf_ref, xb_ref, w_ref, q_ref, y_ref):
    tc, gt, _ = xf_ref.shape
    xcat = jnp.concatenate([ref[:, r, :] for r in range(gt) for ref in (xf_ref, xb_ref)],
                           axis=1).astype(BF16)
    y = (jnp.dot(_chunk_rows(h_ref, tc), w_ref[0], preferred_element_type=F32)
         + jnp.dot(xcat, q_ref[0], preferred_element_type=F32))
    for s in range(SSM_CHUNK):
        y_ref[pl.ds(s, tc, stride=SSM_CHUNK), :] = y[:, s * LANES:(s + 1) * LANES]


def _s5_out(h, xf, xb, wbd, qbd, tc):
    t, d = h.shape
    n_blk, kdim, ncol = wbd.shape
    gt = SSM_GROUP_TILE
    half = xf.shape[-1] // n_blk
    nc = t // SSM_CHUNK
    xblk = pl.BlockSpec((tc, gt, half), lambda j, c: (c, 0, j))
    tok = pl.BlockSpec((tc * SSM_CHUNK, LANES), lambda j, c: (c, j))
    wspec = lambda rows: pl.BlockSpec((1, rows, ncol), lambda j, c: (j, 0, 0), pipeline_mode=pl.Buffered(1))
    return pl.pallas_call(
        _s5_out_kernel,
        grid=(n_blk, nc // tc),
        in_specs=[tok, xblk, xblk, wspec(kdim), wspec(qbd.shape[1])],
        out_specs=tok,
        out_shape=jax.ShapeDtypeStruct((t, d), F32),
        compiler_params=_cparams("arbitrary", "arbitrary"),
        name="s5_chunk_outputs",
    )(h, xf, xb, wbd, qbd)


def _s5_glu_kernel(x_ref, y_ref, mod_ref, g_ref, dskip_ref, w_ref, o_ref):
    d = x_ref.shape[1]
    x = x_ref[...]
    h = _rms_mod(x, g_ref[...], mod_ref[0, 1:2, :], mod_ref[0, 0:1, :])
    z = jax.nn.gelu(y_ref[...] + dskip_ref[...] * h)
    vg = jnp.dot(z.astype(BF16), w_ref[...], preferred_element_type=F32)
    m = vg[:, :d] * jax.nn.sigmoid(vg[:, d:])
    o_ref[...] = x + mod_ref[0, 2:3, :] * m


def _s5_glu(x, y, modl, g, d_skip, w_bf, seg_len, tm):
    t, d = x.shape
    per_seg = seg_len // tm
    tok = pl.BlockSpec((tm, d), lambda i: (i, 0))
    vec = pl.BlockSpec((1, d), lambda i: (0, 0))
    return pl.pallas_call(
        _s5_glu_kernel,
        grid=(t // tm,),
        in_specs=[tok, tok, pl.BlockSpec((1, SUBLANES, d), lambda i: (i // per_seg, 0, 0)), vec, vec,
                  pl.BlockSpec((d, 2 * d), lambda i: (0, 0))],
        out_specs=tok,
        out_shape=jax.ShapeDtypeStruct((t, d), F32),
        compiler_params=_cparams("arbitrary"),
        name="s5_glu_residual",
    )(x, y, modl, g, d_skip, w_bf)


def _s5_layer(x, modl, g, params, seg_len, seq_first, seq_last, tm):
    a_re, a_im, log_dt, b_re, b_im, c_re, c_im, d_skip, w_glu = params
    t, d = x.shape
    L = SSM_CHUNK
    n_g = d // GROUP_CH
    nc = t // L
    wbd, pbd, qbd, mults = _s5_tables(a_re, a_im, log_dt, b_re, b_im, c_re, c_im)
    h = _prenorm(x, modl, g, seg_len, tm)
    tc = math.gcd(nc, SSM_ROW_TILE)
    sf, sb = _s5_states(h, pbd, tc)
    chunks_per_seg = seg_len // L
    reset_f = jnp.repeat(jnp.asarray(seq_first, jnp.int32), chunks_per_seg) * (
        jnp.tile(jnp.arange(chunks_per_seg) == 0, len(seq_first))).astype(jnp.int32)
    reset_b = jnp.repeat(jnp.asarray(seq_last, jnp.int32), chunks_per_seg) * (
        jnp.tile(jnp.arange(chunks_per_seg) == chunks_per_seg - 1, len(seq_last))).astype(jnp.int32)
    xf, xb = _s5_scan(sf, sb, mults, reset_f, reset_b, math.gcd(nc, SSM_SCAN_TILE))
    y = _s5_out(h, xf, xb, wbd, qbd, tc)
    return _s5_glu(x, y, modl, g, d_skip.reshape(1, d), w_glu.astype(BF16), seg_len, tm)


def _router_kernel(x_ref, mod_ref, g_ref, w_ref, b_ref, h_ref, idx_ref, gate_ref, rank_ref, cnt_ref,
                   carry_ref):
    tm = x_ref.shape[0]
    n_e = w_ref.shape[0]

    @pl.when(pl.program_id(0) == 0)
    def _():
        carry_ref[...] = jnp.zeros(carry_ref.shape, F32)

    h = _rms_mod(x_ref[...], g_ref[...], mod_ref[0, 4:5, :], mod_ref[0, 3:4, :])
    h_ref[...] = h
    logits = lax.dot_general(w_ref[...], h, (((1,), (1,)), ((), ())), precision=HIGHEST,
                             preferred_element_type=F32) + b_ref[...]
    e_iota = lax.broadcasted_iota(jnp.int32, (n_e, tm), 0)
    work = logits
    chosen = jnp.zeros((n_e, tm), F32)
    vals, idxs = [], []
    for _ in range(TOP_K):
        m = jnp.max(work, axis=0, keepdims=True)
        ix = jnp.min(jnp.where(work == m, e_iota, n_e), axis=0, keepdims=True)
        hit = e_iota == ix
        work = jnp.where(hit, -jnp.inf, work)
        chosen = jnp.where(hit, 1.0, chosen)
        vals.append(m)
        idxs.append(ix)
    v = jnp.concatenate(vals, axis=0)
    ex = jnp.exp(v - v[0:1])
    gate_ref[...] = ex / jnp.sum(ex, axis=0, keepdims=True)
    idx_ref[...] = jnp.concatenate(idxs, axis=0)
    row = lax.broadcasted_iota(jnp.int32, (tm, tm), 0)
    col = lax.broadcasted_iota(jnp.int32, (tm, tm), 1)
    before = (row < col).astype(BF16)
    cum = jnp.dot(chosen.astype(BF16), before, preferred_element_type=F32) + carry_ref[...]
    ranks = [jnp.sum(jnp.where(e_iota == ix, cum, 0.0), axis=0, keepdims=True) for ix in idxs]
    rank_ref[...] = jnp.concatenate(ranks, axis=0).astype(jnp.int32)
    carry_ref[...] = carry_ref[...] + jnp.sum(chosen, axis=1, keepdims=True)
    cnt_ref[...] = jnp.broadcast_to(carry_ref[...], cnt_ref.shape).astype(jnp.int32)


def _router(x, modl, g, w_router_t, b_router, seg_len, tm):
    t, d = x.shape
    n_e = w_router_t.shape[0]
    per_seg = seg_len // tm
    sel = pl.BlockSpec((TOP_K, tm), lambda i: (0, i))
    return pl.pallas_call(
        _router_kernel,
        grid=(t // tm,),
        in_specs=[pl.BlockSpec((tm, d), lambda i: (i, 0)),
                  pl.BlockSpec((1, SUBLANES, d), lambda i: (i // per_seg, 0, 0)),
                  pl.BlockSpec((1, d), lambda i: (0, 0)),
                  pl.BlockSpec((n_e, d), lambda i: (0, 0)),
                  pl.BlockSpec((n_e, 1), lambda i: (0, 0))],
        out_specs=[pl.BlockSpec((tm, d), lambda i: (i, 0)), sel, sel, sel,
                   pl.BlockSpec((n_e, LANES), lambda i: (0, 0))],
        out_shape=[jax.ShapeDtypeStruct((t, d), F32),
                   jax.ShapeDtypeStruct((TOP_K, t), jnp.int32),
                   jax.ShapeDtypeStruct((TOP_K, t), F32),
                   jax.ShapeDtypeStruct((TOP_K, t), jnp.int32),
                   jax.ShapeDtypeStruct((n_e, LANES), jnp.int32)],
        scratch_shapes=[pltpu.VMEM((n_e, 1), F32)],
        compiler_params=_cparams("arbitrary"),
        name="moe_router",
    )(x, modl, g, w_router_t, b_router)


def _dispatch_kernel(pad_from_ref, pad_n_ref, dest_ref, h_ref, xs_hbm, zrow_ref, sem, zsem):
    tm = h_ref.shape[0]

    def zero_row(r):
        return pltpu.make_async_copy(zrow_ref, xs_hbm.at[pl.ds(r, 1), :], zsem)

    @pl.when(pl.program_id(0) == 0)
    def _():
        zrow_ref[...] = jnp.zeros(zrow_ref.shape, F32)
        for e in range(pad_n_ref.shape[0]):
            def fill(r, carry, e=e):
                zero_row(pad_from_ref[e] + r).start()
                return carry
            lax.fori_loop(0, pad_n_ref[e], fill, 0)
        for e in range(pad_n_ref.shape[0]):
            def drain(r, carry):
                zero_row(0).wait()
                return carry
            lax.fori_loop(0, pad_n_ref[e], drain, 0)

    def issue(j, carry):
        for k in range(TOP_K):
            pltpu.make_async_copy(h_ref.at[pl.ds(j, 1), :],
                                  xs_hbm.at[pl.ds(dest_ref[0, 0, j * TOP_K + k], 1), :], sem).start()
        return carry

    lax.fori_loop(0, tm, issue, 0, unroll=DMA_ISSUE_UNROLL)
    for _ in range(TOP_K):
        pltpu.make_async_copy(h_ref, xs_hbm.at[pl.ds(0, tm), :], sem).wait()


def _dispatch(h, dest_tk, pad_from, pad_n, n_rows, tm):
    t, d = h.shape
    return pl.pallas_call(
        _dispatch_kernel,
        grid_spec=pltpu.PrefetchScalarGridSpec(
            num_scalar_prefetch=2, grid=(t // tm,),
            in_specs=[pl.BlockSpec((1, 1, tm * TOP_K), lambda i, pf, pn: (i, 0, 0), memory_space=pltpu.SMEM),
                      pl.BlockSpec((tm, d), lambda i, pf, pn: (i, 0))],
            out_specs=pl.BlockSpec(memory_space=pl.ANY),
            scratch_shapes=[pltpu.VMEM((1, d), F32), pltpu.SemaphoreType.DMA(()),
                            pltpu.SemaphoreType.DMA(())]),
        out_shape=jax.ShapeDtypeStruct((n_rows, d), F32),
        compiler_params=_cparams("arbitrary"),
        name="moe_dispatch",
    )(pad_from, pad_n, dest_tk.reshape(t // tm, 1, tm * TOP_K), h)


def _expert_kernel(be_ref, bi_ref, nu_ref, x_ref, wgu_ref, bgu_ref, wd_ref, bd_ref, y_ref,
                   wgu_bf_ref, wd_bf_ref):
    del bi_ref
    n_ff = wd_ref.shape[2]
    b = pl.program_id(0)

    @pl.when((b == 0) | (be_ref[b] != be_ref[jnp.maximum(b - 1, 0)]))
    def _():
        wgu_bf_ref[...] = wgu_ref[0, 0].astype(BF16)
        wd_bf_ref[...] = wd_ref[0, 0].astype(BF16)

    @pl.when(b < nu_ref[0])
    def _():
        x = x_ref[...].astype(BF16)
        acc = jnp.zeros(y_ref.shape, F32) + bd_ref[0]
        for c in range(n_ff // FF_CHUNK):
            lo = c * FF_CHUNK
            g_part = (jnp.dot(x, wgu_bf_ref[:, lo:lo + FF_CHUNK], preferred_element_type=F32)
                      + bgu_ref[0, :, lo:lo + FF_CHUNK])
            u_part = (jnp.dot(x, wgu_bf_ref[:, n_ff + lo:n_ff + lo + FF_CHUNK], preferred_element_type=F32)
                      + bgu_ref[0, :, n_ff + lo:n_ff + lo + FF_CHUNK])
            g_part = jnp.minimum(g_part, SWIGLU_LIMIT)
            u_part = jnp.clip(u_part, -SWIGLU_LIMIT, SWIGLU_LIMIT)
            act = (u_part + 1.0) * g_part * jax.nn.sigmoid(SWIGLU_ALPHA * g_part)
            acc = acc + jnp.dot(act.astype(BF16), wd_bf_ref[lo:lo + FF_CHUNK, :],
                                preferred_element_type=F32)
        y_ref[...] = acc


def _experts(xs, blk_expert, blk_idx, n_used, layer, w_gu, b_gu, w_down, b_down, tme):
    nr, d = xs.shape
    _, n_e, _, two_f = w_gu.shape
    n_ff = two_f // 2
    nblk = nr // tme
    rows = pl.BlockSpec((tme, d), lambda b, be, bi, nu: (bi[b], 0))
    b_gu = b_gu[layer]
    b_down = b_down[layer]
    return pl.pallas_call(
        _expert_kernel,
        grid_spec=pltpu.PrefetchScalarGridSpec(
            num_scalar_prefetch=3, grid=(nblk,),
            in_specs=[rows,
                      pl.BlockSpec((1, 1, d, two_f), lambda b, be, bi, nu: (layer, be[b], 0, 0)),
                      pl.BlockSpec((1, 1, two_f), lambda b, be, bi, nu: (be[b], 0, 0)),
                      pl.BlockSpec((1, 1, n_ff, d), lambda b, be, bi, nu: (layer, be[b], 0, 0)),
                      pl.BlockSpec((1, 1, d), lambda b, be, bi, nu: (be[b], 0, 0))],
            out_specs=rows,
            scratch_shapes=[pltpu.VMEM((d, two_f), BF16), pltpu.VMEM((n_ff, d), BF16)]),
        out_shape=jax.ShapeDtypeStruct(xs.shape, F32),
        compiler_params=_cparams("arbitrary"),
        name="moe_experts",
    )(blk_expert, blk_idx, n_used, xs, w_gu, b_gu.reshape(n_e, 1, two_f), w_down,
      b_down.reshape(n_e, 1, d))


def _combine_kernel(dest_ref, x_ref, gate_ref, mod_ref, g_ref, ys_hbm, o_ref, buf, sem, *, final_norm):
    tm = x_ref.shape[0]

    def issue(j, carry):
        for k in range(TOP_K):
            pltpu.make_async_copy(ys_hbm.at[pl.ds(dest_ref[0, 0, j * TOP_K + k], 1), :],
                                  buf.at[k, pl.ds(j, 1), :], sem).start()
        return carry

    lax.fori_loop(0, tm, issue, 0, unroll=DMA_ISSUE_UNROLL)
    for k in range(TOP_K):
        pltpu.make_async_copy(ys_hbm.at[pl.ds(0, tm), :], buf.at[k], sem).wait()
    gates = gate_ref[...]
    moe = gates[:, 0:1] * buf[0]
    for k in range(1, TOP_K):
        moe = moe + gates[:, k:k + 1] * buf[k]
    x = x_ref[...] + mod_ref[0, 5:6, :] * moe
    if final_norm:
        ms = jnp.mean(x * x, axis=-1, keepdims=True)
        x = x * lax.rsqrt(ms + NORM_EPS) * g_ref[...]
    o_ref[...] = x


def _combine(x, ys, dest_tk, gates, modl, final_g, seg_len, tm, final_norm, row0=0, rows=None):
    t, d = x.shape
    rows = t if rows is None else rows
    per_seg = seg_len // tm
    b0 = row0 // tm
    return pl.pallas_call(
        functools.partial(_combine_kernel, final_norm=final_norm),
        grid=(rows // tm,),
        in_specs=[pl.BlockSpec((1, 1, tm * TOP_K), lambda i: (i + b0, 0, 0), memory_space=pltpu.SMEM),
                  pl.BlockSpec((tm, d), lambda i: (i + b0, 0)),
                  pl.BlockSpec((tm, TOP_K), lambda i: (i + b0, 0)),
                  pl.BlockSpec((1, SUBLANES, d), lambda i: ((i + b0) // per_seg, 0, 0)),
                  pl.BlockSpec((1, d), lambda i: (0, 0)),
                  pl.BlockSpec(memory_space=pl.ANY)],
        out_specs=pl.BlockSpec((tm, d), lambda i: (i, 0)),
        out_shape=jax.ShapeDtypeStruct((rows, d), F32),
        scratch_shapes=[pltpu.VMEM((TOP_K, tm, d), F32), pltpu.SemaphoreType.DMA(())],
        compiler_params=_cparams("arbitrary"),
        name="moe_combine",
    )(dest_tk.reshape(t // tm, 1, tm * TOP_K), x, gates, modl, final_g, ys)


def _moe_dispatch_and_experts(x, modl, g, w_router, b_router, layer, w_gu, b_gu, w_down, b_down, seg_len, tm):
    t, d = x.shape
    n_e = w_router.shape[1]
    tme = math.gcd(t * TOP_K, EXPERT_ROW_TILE)
    h, idx_t, gate_t, rank_t, cnt = _router(x, modl, g, w_router.T, b_router.reshape(n_e, 1), seg_len, tm)
    counts = cnt[:, 0]
    padded = (counts + tme - 1) // tme * tme
    pad_end = jnp.cumsum(padded)
    pad_start = pad_end - padded
    nblk = (t * TOP_K) // tme + n_e
    n_used = (pad_end[-1] // tme).astype(jnp.int32).reshape(1)
    blk_idx = jnp.minimum(jnp.arange(nblk, dtype=jnp.int32), n_used[0] - 1)
    blk_expert = jnp.minimum(jnp.sum((pad_end[None, :] <= (blk_idx * tme)[:, None]).astype(jnp.int32), axis=1),
                             n_e - 1)
    start_of = jnp.sum(jnp.where(idx_t[None] == jnp.arange(n_e, dtype=jnp.int32)[:, None, None],
                                 pad_start[:, None, None], 0), axis=0)
    dest = (start_of + rank_t).astype(jnp.int32)
    dest_tk = dest.T.reshape(-1)
    xs = _dispatch(h, dest_tk, (pad_start + counts).astype(jnp.int32), (padded - counts).astype(jnp.int32),
                   nblk * tme, math.gcd(seg_len, ROW_DMA_TILE))
    ys = _experts(xs, blk_expert, blk_idx, n_used, layer, w_gu, b_gu, w_down, b_down, tme)
    return ys, dest_tk, gate_t.T


def kernel(x_prompt, x_sample, c_prompt, c_sample, norm1_g, norm2_g, final_g, w_ada, b_ada, w_qkv, w_o,
           lam_q1, lam_k1, lam_q2, lam_k2, subln_g, ssm_a_re, ssm_a_im, ssm_log_dt, ssm_b_re, ssm_b_im,
           ssm_c_re, ssm_c_im, ssm_d, ssm_w_glu, w_router, b_router, w_gu, b_gu, w_down, b_down):
    n_b, seg_len, d = x_prompt.shape
    n_bs, s_len, _ = x_sample.shape
    depth = w_ada.shape[0]
    assert s_len % seg_len == 0 and s_len // seg_len in (1, 2)
    per_sample = s_len // seg_len
    n_seg = n_b + n_bs * per_sample
    t = n_seg * seg_len
    tm = math.gcd(seg_len, TOKEN_TILE)

    seg_seq = list(range(n_b)) + [n_b + j for j in range(n_bs) for _ in range(per_sample)]
    seg_pos = [0] * n_b + [r for _ in range(n_bs) for r in range(per_sample)]
    seg_a = list(range(n_b)) + [n_b + j * per_sample for j in range(n_bs) for _ in range(per_sample)]
    seg_b = list(range(n_b)) + [n_b + j * per_sample + per_sample - 1 for j in range(n_bs) for _ in range(per_sample)]
    two = [0] * n_b + [int(per_sample == 2)] * (n_bs * per_sample)
    seq_first = [1] * n_b + [int(r == 0) for _ in range(n_bs) for r in range(per_sample)]
    seq_last = [1] * n_b + [int(r == per_sample - 1) for _ in range(n_bs) for r in range(per_sample)]
    per_seg = seg_len // tm
    pos_blk = jnp.asarray([seg_pos[s] * per_seg + r for s in range(n_seg) for r in range(per_seg)], jnp.int32)

    x = None
    x_parts = (x_prompt.reshape(n_b * seg_len, d), x_sample.reshape(n_bs * s_len, d))

    n_c = n_b + n_bs
    c_rows = -(-n_c // SUBLANES) * SUBLANES
    c_all = jnp.concatenate([c_prompt, c_sample, jnp.zeros((c_rows - n_c, d), F32)], axis=0)
    mod = _ada(c_all, w_ada, b_ada)
    mod = mod[:, jnp.asarray(seg_seq)].reshape(depth, n_seg, 6, d)
    mod = jnp.concatenate([mod, jnp.zeros((depth, n_seg, SUBLANES - 6, d), F32)], axis=2)

    inv_freq = ROPE_THETA ** (-jnp.arange(0, HEAD_DIM, 2, dtype=F32) / HEAD_DIM)
    ang = jnp.arange(s_len, dtype=F32)[:, None] * inv_freq[None, :]
    cos_t = jnp.tile(jnp.cos(ang), (1, 2 * LANES // HEAD_DIM))
    sin_t = jnp.tile(jnp.concatenate([-jnp.sin(ang), jnp.sin(ang)], axis=1), (1, LANES // HEAD_DIM))

    out_p = out_s = None
    for i in range(depth):
        modl = mod[i]
        g1 = norm1_g[i].reshape(1, d)
        g2 = norm2_g[i].reshape(1, d)
        j = i // 2
        if i % 2 == 0:
            lambda_init = 0.8 - 0.6 * math.exp(-0.3 * i)
            xa, xb = x_parts if x is None else (x, None)
            q, k, vt = _qkv(xa, xb, modl, g1, w_qkv[j], cos_t, sin_t, pos_blk, seg_len, tm)
            lam_pack = jnp.zeros((SUBLANES, 2 * HEAD_DIM), F32).at[0:4, 0:HEAD_DIM].set(
                jnp.stack([lam_q1[j], lam_k1[j], lam_q2[j], lam_k2[j]]).astype(F32))
            o = _flash(q, k, vt, lam_pack, subln_g[j].reshape(2 * HEAD_DIM, 1).astype(F32),
                       jnp.asarray(seg_a, jnp.int32), jnp.asarray(seg_b, jnp.int32),
                       jnp.asarray(two, jnp.int32), seg_len, lambda_init)
            x = _proj_res(o, w_o[j].astype(BF16), xa, xb, modl, seg_len, tm, gate_row=2)
        else:
            if x is None:
                x = jnp.concatenate(x_parts, axis=0)
            params = (ssm_a_re[j], ssm_a_im[j], ssm_log_dt[j], ssm_b_re[j], ssm_b_im[j], ssm_c_re[j],
                      ssm_c_im[j], ssm_d[j], ssm_w_glu[j])
            x = _s5_layer(x, modl, g1, params, seg_len, seq_first, seq_last, tm)
        ys, dest_tk, gates = _moe_dispatch_and_experts(x, modl, g2, w_router[i], b_router[i], i, w_gu, b_gu,
                                                       w_down, b_down, seg_len, tm)
        fg = final_g.reshape(1, d)
        tr = math.gcd(seg_len, ROW_DMA_TILE)
        if i == depth - 1:
            out_p = _combine(x, ys, dest_tk, gates, modl, fg, seg_len, tr, True, 0, n_b * seg_len)
            out_s = _combine(x, ys, dest_tk, gates, modl, fg, seg_len, tr, True, n_b * seg_len, n_bs * s_len)
        else:
            x = _combine(x, ys, dest_tk, gates, modl, fg, seg_len, tr, False)
    return out_p.reshape(n_b, seg_len, d), out_s.reshape(n_bs, s_len, d)
```

```python
import functools
import math

import jax
import jax.numpy as jnp
from jax import lax
from jax.experimental import pallas as pl
from jax.experimental.pallas import tpu as pltpu

F32 = jnp.float32
BF16 = jnp.bfloat16
HIGHEST = lax.Precision.HIGHEST

N_DIFF_HEADS = 8
HEAD_DIM = 64
ROPE_THETA = 10000.0
SUBLN_EPS = 1e-5
NORM_EPS = 1e-6
GROUP_CH = 16
STATE_DIM = 64
N_EXPERTS = 32
TOP_K = 4
SWIGLU_ALPHA = 1.702
SWIGLU_LIMIT = 7.0

LANES = 128
SUBLANES = 8
VMEM_LIMIT_BYTES = 56 * 1024 * 1024

TOKEN_TILE = 512
ATT_Q_TILE = 512
ATT_KV_TILE = 512
ATT_UNROLL = 16
VT_ONES_ROWS = 16
LOG2_E = math.log2(math.e)
SSM_CHUNK = 16
SSM_GROUP_TILE = 8
SSM_ROW_TILE = 256
SSM_SCAN_TILE = 64
EXPERT_ROW_TILE = 512
FF_CHUNK = 512
ROW_DMA_TILE = 1024
DMA_ISSUE_UNROLL = 8


def _cparams(*sem):
    return pltpu.CompilerParams(dimension_semantics=sem, vmem_limit_bytes=VMEM_LIMIT_BYTES)


def _rms_mod(x, g, scale, shift):
    ms = jnp.mean(x * x, axis=-1, keepdims=True)
    return x * lax.rsqrt(ms + NORM_EPS) * g * (1.0 + scale) + shift


def _ada_kernel(c_ref, w_ref, b_ref, o_ref):
    c = c_ref[...]
    cond = c * jax.nn.sigmoid(c)
    o_ref[0] = jnp.dot(cond, w_ref[0], precision=HIGHEST, preferred_element_type=F32) + b_ref[0]


def _ada(c_pad, w_ada, b_ada):
    depth, d, n6 = w_ada.shape
    rows = c_pad.shape[0]
    tn = 1536 if n6 % 1536 == 0 else n6
    return pl.pallas_call(
        _ada_kernel,
        grid=(depth, n6 // tn),
        in_specs=[pl.BlockSpec((rows, d), lambda i, j: (0, 0)),
                  pl.BlockSpec((1, d, tn), lambda i, j: (i, 0, j)),
                  pl.BlockSpec((1, 1, tn), lambda i, j: (i, 0, j))],
        out_specs=pl.BlockSpec((1, rows, tn), lambda i, j: (i, 0, j)),
        out_shape=jax.ShapeDtypeStruct((depth, rows, n6), F32),
        compiler_params=_cparams("arbitrary", "arbitrary"),
        name="ada_mod",
    )(c_pad, w_ada, b_ada.reshape(depth, 1, n6))


def _token_rows(xa_ref, xb_ref, n_a):
    return jnp.where(pl.program_id(0) < n_a, xa_ref[...], xb_ref[...])


def _token_specs(xa, xb, tm, index_args):
    n_a = xa.shape[0] // tm
    d = xa.shape[1]
    if index_args == 1:
        return n_a, [pl.BlockSpec((tm, d), lambda i: (jnp.minimum(i, n_a - 1), 0)),
                     pl.BlockSpec((tm, d), lambda i: (jnp.maximum(i - n_a, 0), 0))]
    return n_a, [pl.BlockSpec((tm, d), lambda i, p: (jnp.minimum(i, n_a - 1), 0)),
                 pl.BlockSpec((tm, d), lambda i, p: (jnp.maximum(i - n_a, 0), 0))]


def _qkv_kernel(pos_ref, xa_ref, xb_ref, mod_ref, g_ref, w_ref, wvt_ref, cos_ref, sin_ref, q_ref, k_ref,
                vt_ref, *, n_a):
    del pos_ref
    d = xa_ref.shape[1]
    h = _rms_mod(_token_rows(xa_ref, xb_ref, n_a), g_ref[...], mod_ref[0, 1:2, :],
                 mod_ref[0, 0:1, :]).astype(BF16)
    qkv = jnp.dot(h, w_ref[...], preferred_element_type=F32)
    vt = lax.dot_general(wvt_ref[...], h, (((1,), (1,)), ((), ())), preferred_element_type=F32)
    ones = jnp.ones((VT_ONES_ROWS, vt.shape[1]), BF16)
    for hd in range(vt_ref.shape[0]):
        vt_ref[hd, 0, 0:LANES, :] = vt[hd * LANES:(hd + 1) * LANES, :].astype(BF16)
        vt_ref[hd, 0, LANES:LANES + VT_ONES_ROWS, :] = ones
    cos = cos_ref[...]
    sin = sin_ref[...]
    lane = lax.broadcasted_iota(jnp.int32, cos.shape, 1)
    first_half = (lane % HEAD_DIM) < (HEAD_DIM // 2)

    def rotary(t):
        partner = jnp.where(first_half,
                            pltpu.roll(t, LANES - HEAD_DIM // 2, 1),
                            pltpu.roll(t, HEAD_DIM // 2, 1))
        return t * cos + partner * sin

    for j in range(d // LANES):
        sl = slice(j * LANES, (j + 1) * LANES)
        q_ref[:, sl] = (rotary(qkv[:, sl]) * (HEAD_DIM ** -0.5 * LOG2_E)).astype(BF16)
        k_ref[:, sl] = rotary(qkv[:, d + j * LANES:d + (j + 1) * LANES]).astype(BF16)


def _qkv(xa, xb, modl, g, w_qkv, cos_t, sin_t, pos_blk, seg_len, tm):
    d = xa.shape[1]
    t = xa.shape[0] + (0 if xb is None else xb.shape[0])
    xb = xa if xb is None else xb
    n_a, x_specs = _token_specs(xa, xb, tm, 2)
    per_seg = seg_len // tm
    n_vh = d // LANES
    tk = math.gcd(seg_len, ATT_KV_TILE)
    per_kv = tk // tm
    tok = lambda i, p: (i, 0)
    out = jax.ShapeDtypeStruct((t, d), BF16)
    w_qk = w_qkv[:, :2 * d].astype(BF16)
    w_vt = w_qkv[:, 2 * d:].T.astype(BF16)
    return pl.pallas_call(
        functools.partial(_qkv_kernel, n_a=n_a),
        grid_spec=pltpu.PrefetchScalarGridSpec(
            num_scalar_prefetch=1, grid=(t // tm,),
            in_specs=x_specs + [
                      pl.BlockSpec((1, SUBLANES, d), lambda i, p: (i // per_seg, 0, 0)),
                      pl.BlockSpec((1, d), lambda i, p: (0, 0)),
                      pl.BlockSpec((d, 2 * d), lambda i, p: (0, 0)),
                      pl.BlockSpec((d, d), lambda i, p: (0, 0)),
                      pl.BlockSpec((tm, LANES), lambda i, p: (p[i], 0)),
                      pl.BlockSpec((tm, LANES), lambda i, p: (p[i], 0))],
            out_specs=[pl.BlockSpec((tm, d), tok), pl.BlockSpec((tm, d), tok),
                       pl.BlockSpec((n_vh, 1, LANES + VT_ONES_ROWS, tm),
                                    lambda i, p: (0, i // per_kv, 0, i % per_kv))]),
        out_shape=[out, out, jax.ShapeDtypeStruct((n_vh, t // tk, LANES + VT_ONES_ROWS, tk), BF16)],
        compiler_params=_cparams("arbitrary"),
        name="qkv_rotary",
    )(pos_blk, xa, xb, modl, g, w_qk, w_vt, cos_t, sin_t)


def _flash_kernel(sa_ref, sb_ref, two_ref, q_ref, ka_ref, vta_ref, kb_ref, vtb_ref, lam_ref, g_ref,
                  o_ref, qt_ref, m_ref, acc_ref, *, tk, unroll, lambda_init):
    del sa_ref, sb_ref
    seg = pl.program_id(0)
    qt = q_ref[...].astype(F32).T
    chan = lax.broadcasted_iota(jnp.int32, qt.shape, 0)
    qt_ref[0] = jnp.where(chan < HEAD_DIM, qt, 0.0).astype(BF16)
    qt_ref[1] = jnp.where(chan >= HEAD_DIM, qt, 0.0).astype(BF16)
    m_ref[...] = jnp.full(m_ref.shape, -jnp.inf, F32)
    acc_ref[...] = jnp.zeros(acc_ref.shape, F32)

    def sweep(k_ref, vt_ref):
        units = [(tile, g) for tile in range(unroll) for g in range(2)]

        half = tk // 2

        def half_scores(jj, u, r):
            tile, g = units[u]
            off = pl.multiple_of((jj * unroll + tile) * tk, tk)
            return jnp.dot(k_ref[pl.ds(off + r * half, half), :], qt_ref[g], preferred_element_type=F32)

        def values(jj, u, p, alpha):
            tile, g = units[u]
            acc_ref[g] = alpha * acc_ref[g] + jnp.dot(vt_ref[0, jj * unroll + tile], p,
                                                      preferred_element_type=F32)

        def body(jj, carry):
            s_next = [half_scores(jj, 0, 0), half_scores(jj, 0, 1)]
            pending = None
            for u in range(len(units)):
                g = units[u][1]
                s_cur = s_next
                more = u + 1 < len(units)
                if more:
                    s_next = [half_scores(jj, u + 1, 0)]
                m_prev = m_ref[g]
                m_new = m_prev
                for s in s_cur:
                    m_new = jnp.maximum(m_new, jnp.max(s, axis=0, keepdims=True))
                m_ref[g] = m_new
                if pending is not None:
                    values(jj, *pending)
                if more:
                    s_next.append(half_scores(jj, u + 1, 1))
                p = jnp.concatenate([jnp.exp2((s - m_new).astype(BF16)) for s in s_cur], axis=0)
                pending = (u, p, jnp.exp2(m_prev - m_new))
            values(jj, *pending)
            return carry
        lax.fori_loop(0, k_ref.shape[0] // (tk * unroll), body, 0)

    sweep(ka_ref, vta_ref)

    @pl.when(two_ref[seg] == 1)
    def _():
        sweep(kb_ref, vtb_ref)

    o1 = acc_ref[0, 0:LANES, :] / acc_ref[0, LANES:LANES + 1, :]
    o2 = acc_ref[1, 0:LANES, :] / acc_ref[1, LANES:LANES + 1, :]
    lam = (jnp.exp(jnp.sum(lam_ref[0:1, :] * lam_ref[1:2, :], axis=-1, keepdims=True))
           - jnp.exp(jnp.sum(lam_ref[2:3, :] * lam_ref[3:4, :], axis=-1, keepdims=True))
           + lambda_init)
    ot = o1 - lam * o2
    ms = jnp.mean(ot * ot, axis=0, keepdims=True)
    ot = ot * lax.rsqrt(ms + SUBLN_EPS) * g_ref[...] * (1.0 - lambda_init)
    o_ref[...] = ot.T.astype(BF16)


def _flash(q, k, vt, lam_pack, subln_g_col, seg_a, seg_b, two, seg_len, lambda_init):
    t, d = q.shape
    nseg = t // seg_len
    n_vh, _, v_rows, tk = vt.shape
    width = 2 * HEAD_DIM
    per_seg = seg_len // tk
    tq = min(ATT_Q_TILE, seg_len)
    nq = seg_len // tq
    kern = functools.partial(_flash_kernel, tk=tk, unroll=min(ATT_UNROLL, per_seg), lambda_init=lambda_init)
    k_a = pl.BlockSpec((seg_len, width), lambda s, h, i, sa, sb, tw: (sa[s], h))
    k_b = pl.BlockSpec((seg_len, width), lambda s, h, i, sa, sb, tw: (sb[s], h))
    vt_a = pl.BlockSpec((1, per_seg, v_rows, tk), lambda s, h, i, sa, sb, tw: (h, sa[s], 0, 0))
    vt_b = pl.BlockSpec((1, per_seg, v_rows, tk), lambda s, h, i, sa, sb, tw: (h, sb[s], 0, 0))
    qo = pl.BlockSpec((tq, width), lambda s, h, i, sa, sb, tw: (s * nq + i, h))
    return pl.pallas_call(
        kern,
        grid_spec=pltpu.PrefetchScalarGridSpec(
            num_scalar_prefetch=3, grid=(nseg, n_vh, nq),
            in_specs=[qo, k_a, vt_a, k_b, vt_b,
                      pl.BlockSpec((SUBLANES, width), lambda s, h, i, sa, sb, tw: (0, 0)),
                      pl.BlockSpec((width, 1), lambda s, h, i, sa, sb, tw: (0, 0))],
            out_specs=qo,
            scratch_shapes=[pltpu.VMEM((2, width, tq), BF16),
                            pltpu.VMEM((2, 1, tq), F32),
                            pltpu.VMEM((2, v_rows, tq), F32)]),
        out_shape=jax.ShapeDtypeStruct((t, d), BF16),
        compiler_params=_cparams("arbitrary", "arbitrary", "arbitrary"),
        name="diff_flash_attention",
    )(seg_a, seg_b, two, q, k, vt, k, vt, lam_pack, subln_g_col)


def _proj_res_kernel(a_ref, w_ref, xa_ref, xb_ref, mod_ref, o_ref, *, gate_row, n_a):
    y = jnp.dot(a_ref[...], w_ref[...], preferred_element_type=F32)
    o_ref[...] = _token_rows(xa_ref, xb_ref, n_a) + mod_ref[0, gate_row:gate_row + 1, :] * y


def _proj_res(a, w_bf, xa, xb, modl, seg_len, tm, gate_row):
    d = xa.shape[1]
    t = xa.shape[0] + (0 if xb is None else xb.shape[0])
    xb = xa if xb is None else xb
    n_a, x_specs = _token_specs(xa, xb, tm, 1)
    per_seg = seg_len // tm
    return pl.pallas_call(
        functools.partial(_proj_res_kernel, gate_row=gate_row, n_a=n_a),
        grid=(t // tm,),
        in_specs=[pl.BlockSpec((tm, a.shape[1]), lambda i: (i, 0)),
                  pl.BlockSpec(w_bf.shape, lambda i: (0, 0))] + x_specs + [
                  pl.BlockSpec((1, SUBLANES, d), lambda i: (i // per_seg, 0, 0))],
        out_specs=pl.BlockSpec((tm, d), lambda i: (i, 0)),
        out_shape=jax.ShapeDtypeStruct((t, d), F32),
        compiler_params=_cparams("arbitrary"),
        name="proj_residual",
    )(a, w_bf, xa, xb, modl)


def _prenorm_kernel(x_ref, mod_ref, g_ref, o_ref):
    o_ref[...] = _rms_mod(x_ref[...], g_ref[...], mod_ref[0, 1:2, :], mod_ref[0, 0:1, :])


def _prenorm(x, modl, g, seg_len, tm):
    t, d = x.shape
    per_seg = seg_len // tm
    return pl.pallas_call(
        _prenorm_kernel,
        grid=(t // tm,),
        in_specs=[pl.BlockSpec((tm, d), lambda i: (i, 0)),
                  pl.BlockSpec((1, SUBLANES, d), lambda i: (i // per_seg, 0, 0)),
                  pl.BlockSpec((1, d), lambda i: (0, 0))],
        out_specs=pl.BlockSpec((tm, d), lambda i: (i, 0)),
        out_shape=jax.ShapeDtypeStruct((t, d), F32),
        compiler_params=_cparams("arbitrary"),
        name="s5_prenorm",
    )(x, modl, g)


def _block_diag_kernel(a_ref, o_ref, *, rows_interleaved, cols_interleaved):
    gt, n_r, n_c = a_ref.shape
    gc = GROUP_CH
    o_ref[...] = jnp.zeros(o_ref.shape, o_ref.dtype)
    src = lax.broadcasted_iota(jnp.int32, (n_c, gt * n_c), 0)
    dst = lax.broadcasted_iota(jnp.int32, (n_c, gt * n_c), 1)
    for r in range(gt):
        a = a_ref[r]
        if cols_interleaved:
            place = (dst == (src // gc) * (gt * gc) + r * gc + src % gc).astype(a.dtype)
            a = jnp.dot(a, place, preferred_element_type=F32).astype(o_ref.dtype)
            cols = slice(0, gt * n_c)
        else:
            cols = slice(r * n_c, (r + 1) * n_c)
        if rows_interleaved:
            for s in range(n_r // gc):
                o_ref[0, (s * gt + r) * gc:(s * gt + r + 1) * gc, cols] = a[s * gc:(s + 1) * gc, :]
        else:
            o_ref[0, r * n_r:(r + 1) * n_r, cols] = a


def _block_diag(tab, rows_interleaved, cols_interleaved):
    n_g, n_r, n_c = tab.shape
    gt = SSM_GROUP_TILE
    return pl.pallas_call(
        functools.partial(_block_diag_kernel, rows_interleaved=rows_interleaved,
                          cols_interleaved=cols_interleaved),
        grid=(n_g // gt,),
        in_specs=[pl.BlockSpec((gt, n_r, n_c), lambda j: (j, 0, 0))],
        out_specs=pl.BlockSpec((1, gt * n_r, gt * n_c), lambda j: (j, 0, 0)),
        out_shape=jax.ShapeDtypeStruct((n_g // gt, gt * n_r, gt * n_c), tab.dtype),
        compiler_params=_cparams("arbitrary"),
        name="s5_block_diag",
    )(tab)


def _s5_tables(a_re, a_im, log_dt, b_re, b_im, c_re, c_im):
    f32 = F32
    L = SSM_CHUNK
    ar, ai = a_re.astype(f32), a_im.astype(f32)
    n_g, n_p = ar.shape[1], ar.shape[2]
    n_h = b_re.shape[-1]
    dt = jnp.exp(log_dt.astype(f32))[..., None]
    mag = jnp.exp(ar * dt)
    lr, li = mag * jnp.cos(ai * dt), mag * jnp.sin(ai * dt)
    den = ar * ar + ai * ai
    fr = ((lr - 1.0) * ar + li * ai) / den
    fi = (li * ar - (lr - 1.0) * ai) / den
    br, bi = b_re.astype(f32), b_im.astype(f32)
    bbr = fr[..., None] * br - fi[..., None] * bi
    bbi = fr[..., None] * bi + fi[..., None] * br
    cr, ci = c_re.astype(f32), c_im.astype(f32)
    kk = jnp.arange(L + 1, dtype=f32)
    pmag = jnp.exp(ar[..., None] * dt[..., None] * kk)
    pang = ai[..., None] * dt[..., None] * kk
    pr, pi = pmag * jnp.cos(pang), pmag * jnp.sin(pang)
    er = cr[..., None] * pr[:, :, None] - ci[..., None] * pi[:, :, None]
    ei = cr[..., None] * pi[:, :, None] + ci[..., None] * pr[:, :, None]
    klag = (jnp.einsum('dgopk,dgpi->dgkoi', er[..., :L], bbr, precision=HIGHEST)
            - jnp.einsum('dgopk,dgpi->dgkoi', ei[..., :L], bbi, precision=HIGHEST))
    kf, kb = klag[0], klag[1]
    kfull = jnp.concatenate([kb[:, 1:][:, ::-1], (kf[:, 0] + kb[:, 0])[:, None], kf[:, 1:]], axis=1)
    s_idx = jnp.arange(L)[:, None]
    t_idx = jnp.arange(L)[None, :]
    w = kfull[:, t_idx - s_idx + L - 1]
    w = jnp.transpose(w, (0, 1, 4, 2, 3)).reshape(n_g, L * n_h, L * n_h)
    def in_to_state(d, pw_idx):
        pwr = jnp.take(pr[d], pw_idx, axis=-1)
        pwi = jnp.take(pi[d], pw_idx, axis=-1)
        re = pwr[..., None] * bbr[d][:, :, None, :] - pwi[..., None] * bbi[d][:, :, None, :]
        im = pwr[..., None] * bbi[d][:, :, None, :] + pwi[..., None] * bbr[d][:, :, None, :]
        m = jnp.concatenate([re, im], axis=1)
        return jnp.transpose(m, (0, 2, 3, 1)).reshape(n_g, L * n_h, 2 * n_p)
    pcat = jnp.concatenate([in_to_state(0, L - 1 - jnp.arange(L)), in_to_state(1, jnp.arange(L))], axis=-1)
    def state_to_out(d, pw_idx):
        e_r = jnp.take(er[d], pw_idx, axis=-1)
        e_i = jnp.take(ei[d], pw_idx, axis=-1)
        m = jnp.concatenate([e_r, -e_i], axis=2)
        return jnp.transpose(m, (0, 2, 3, 1)).reshape(n_g, 2 * n_p, L * n_h)
    qcat = jnp.concatenate([state_to_out(0, jnp.arange(L) + 1), state_to_out(1, L - jnp.arange(L))], axis=1)

    def scan_mult(d):
        a_r, a_i = pr[d][..., L], pi[d][..., L]
        m1 = jnp.concatenate([a_r, a_r], axis=-1)
        m2 = jnp.concatenate([-a_i, a_i], axis=-1)
        lay = lambda m: jnp.transpose(m.reshape(n_g // SSM_GROUP_TILE, SSM_GROUP_TILE, 2 * n_p),
                                      (1, 0, 2)).reshape(SSM_GROUP_TILE, -1)
        return lay(m1), lay(m2)
    af1, af2 = scan_mult(0)
    ab1, ab2 = scan_mult(1)
    wbd = _block_diag(w.astype(BF16), True, True)
    pbd = _block_diag(pcat.astype(BF16), True, False)
    qbd = _block_diag(qcat.astype(BF16), False, True)
    return wbd, pbd, qbd, (af1, af2, ab1, ab2)


def _chunk_rows(h_ref, tc):
    return jnp.concatenate([h_ref[pl.ds(s, tc, stride=SSM_CHUNK), :] for s in range(SSM_CHUNK)],
                           axis=1).astype(BF16)


def _s5_states_kernel(h_ref, p_ref, sf_ref, sb_ref):
    tc, gt, half = sf_ref.shape
    st = jnp.dot(_chunk_rows(h_ref, tc), p_ref[0], preferred_element_type=F32)
    for r in range(gt):
        sf_ref[:, r, :] = st[:, r * 2 * half:r * 2 * half + half]
        sb_ref[:, r, :] = st[:, r * 2 * half + half:(r + 1) * 2 * half]


def _s5_states(h, pbd, tc):
    t, d = h.shape
    n_blk, kdim, ncol = pbd.shape
    gt = SSM_GROUP_TILE
    half = ncol // (2 * gt)
    nc = t // SSM_CHUNK
    out = jax.ShapeDtypeStruct((nc, gt, n_blk * half), F32)
    return pl.pallas_call(
        _s5_states_kernel,
        grid=(n_blk, nc // tc),
        in_specs=[pl.BlockSpec((tc * SSM_CHUNK, LANES), lambda j, c: (c, j)),
                  pl.BlockSpec((1, kdim, ncol), lambda j, c: (j, 0, 0))],
        out_specs=[pl.BlockSpec((tc, gt, half), lambda j, c: (c, 0, j))] * 2,
        out_shape=[out, out],
        compiler_params=_cparams("arbitrary", "arbitrary"),
        name="s5_chunk_states",
    )(h, pbd)


def _s5_scan_kernel(rf_ref, rb_ref, sf_ref, sb_ref, af1_ref, af2_ref, ab1_ref, ab2_ref,
                    xf_ref, xb_ref, cf_ref, cb_ref, *, tcs, nblk):
    i = pl.program_id(0)

    @pl.when(i == 0)
    def _():
        cf_ref[...] = jnp.zeros(cf_ref.shape, F32)
        cb_ref[...] = jnp.zeros(cb_ref.shape, F32)

    width = cf_ref.shape[-1]

    def swap(x):
        return jnp.concatenate(
            [pltpu.roll(x[:, j * LANES:(j + 1) * LANES], LANES // 2, 1) for j in range(width // LANES)],
            axis=1)

    af1, af2 = af1_ref[...], af2_ref[...]
    ab1, ab2 = ab1_ref[...], ab2_ref[...]

    def step(r, carry):
        cf, cfs, cb, cbs = carry
        keep = (1 - rf_ref[i * tcs + r]).astype(F32)
        cf, cfs = cf * keep, cfs * keep
        xf_ref[r] = cf
        s = sf_ref[r]
        cf, cfs = af1 * cf + af2 * cfs + s, af1 * cfs - af2 * cf + swap(s)
        rb = tcs - 1 - r
        keep = (1 - rb_ref[(nblk - 1 - i) * tcs + rb]).astype(F32)
        cb, cbs = cb * keep, cbs * keep
        xb_ref[rb] = cb
        s = sb_ref[rb]
        cb, cbs = ab1 * cb + ab2 * cbs + s, ab1 * cbs - ab2 * cb + swap(s)
        return cf, cfs, cb, cbs

    cf0, cb0 = cf_ref[...], cb_ref[...]
    cf, _, cb, _ = lax.fori_loop(0, tcs, step, (cf0, swap(cf0), cb0, swap(cb0)), unroll=2)
    cf_ref[...] = cf
    cb_ref[...] = cb


def _s5_scan(sf, sb, mults, reset_f, reset_b, tcs):
    nc, gt, width = sf.shape
    nblk = nc // tcs
    fblk = pl.BlockSpec((tcs, gt, width), lambda i, a, b: (i, 0, 0))
    bblk = pl.BlockSpec((tcs, gt, width), lambda i, a, b: (nblk - 1 - i, 0, 0))
    tab = pl.BlockSpec((gt, width), lambda i, a, b: (0, 0))
    out = jax.ShapeDtypeStruct((nc, gt, width), F32)
    return pl.pallas_call(
        functools.partial(_s5_scan_kernel, tcs=tcs, nblk=nblk),
        grid_spec=pltpu.PrefetchScalarGridSpec(
            num_scalar_prefetch=2, grid=(nblk,),
            in_specs=[fblk, bblk, tab, tab, tab, tab],
            out_specs=[fblk, bblk],
            scratch_shapes=[pltpu.VMEM((gt, width), F32), pltpu.VMEM((gt, width), F32)]),
        out_shape=[out, out],
        compiler_params=_cparams("arbitrary"),
        name="s5_chunk_scan",
    )(reset_f, reset_b, sf, sb, *mults)


def _s5_out_kernel(h_ref, xf_ref, xb_ref, w_ref, q_ref, y_ref):
    tc, gt, _ = xf_ref.shape
    xcat = jnp.concatenate([ref[:, r, :] for r in range(gt) for ref in (xf_ref, xb_ref)],
                           axis=1).astype(BF16)
    y = (jnp.dot(_chunk_rows(h_ref, tc), w_ref[0], preferred_element_type=F32)
         + jnp.dot(xcat, q_ref[0], preferred_element_type=F32))
    for s in range(SSM_CHUNK):
        y_ref[pl.ds(s, tc, stride=SSM_CHUNK), :] = y[:, s * LANES:(s + 1) * LANES]


def _s5_out(h, xf, xb, wbd, qbd, tc):
    t, d = h.shape
    n_blk, kdim, ncol = wbd.shape
    gt = SSM_GROUP_TILE
    half = xf.shape[-1] // n_blk
    nc = t // SSM_CHUNK
    xblk = pl.BlockSpec((tc, gt, half), lambda j, c: (c, 0, j))
    tok = pl.BlockSpec((tc * SSM_CHUNK, LANES), lambda j, c: (c, j))
    wspec = lambda rows: pl.BlockSpec((1, rows, ncol), lambda j, c: (j, 0, 0), pipeline_mode=pl.Buffered(1))
    return pl.pallas_call(
        _s5_out_kernel,
        grid=(n_blk, nc // tc),
        in_specs=[tok, xblk, xblk, wspec(kdim), wspec(qbd.shape[1])],
        out_specs=tok,
        out_shape=jax.ShapeDtypeStruct((t, d), F32),
        compiler_params=_cparams("arbitrary", "arbitrary"),
        name="s5_chunk_outputs",
    )(h, xf, xb, wbd, qbd)


def _s5_glu_kernel(x_ref, y_ref, mod_ref, g_ref, dskip_ref, w_ref, o_ref):
    d = x_ref.shape[1]
    x = x_ref[...]
    h = _rms_mod(x, g_ref[...], mod_ref[0, 1:2, :], mod_ref[0, 0:1, :])
    z = jax.nn.gelu(y_ref[...] + dskip_ref[...] * h)
    vg = jnp.dot(z.astype(BF16), w_ref[...], preferred_element_type=F32)
    m = vg[:, :d] * jax.nn.sigmoid(vg[:, d:])
    o_ref[...] = x + mod_ref[0, 2:3, :] * m


def _s5_glu(x, y, modl, g, d_skip, w_bf, seg_len, tm):
    t, d = x.shape
    per_seg = seg_len // tm
    tok = pl.BlockSpec((tm, d), lambda i: (i, 0))
    vec = pl.BlockSpec((1, d), lambda i: (0, 0))
    return pl.pallas_call(
        _s5_glu_kernel,
        grid=(t // tm,),
        in_specs=[tok, tok, pl.BlockSpec((1, SUBLANES, d), lambda i: (i // per_seg, 0, 0)), vec, vec,
                  pl.BlockSpec((d, 2 * d), lambda i: (0, 0))],
        out_specs=tok,
        out_shape=jax.ShapeDtypeStruct((t, d), F32),
        compiler_params=_cparams("arbitrary"),
        name="s5_glu_residual",
    )(x, y, modl, g, d_skip, w_bf)


def _s5_layer(x, modl, g, params, seg_len, seq_first, seq_last, tm):
    a_re, a_im, log_dt, b_re, b_im, c_re, c_im, d_skip, w_glu = params
    t, d = x.shape
    L = SSM_CHUNK
    n_g = d // GROUP_CH
    nc = t // L
    wbd, pbd, qbd, mults = _s5_tables(a_re, a_im, log_dt, b_re, b_im, c_re, c_im)
    h = _prenorm(x, modl, g, seg_len, tm)
    tc = math.gcd(nc, SSM_ROW_TILE)
    sf, sb = _s5_states(h, pbd, tc)
    chunks_per_seg = seg_len // L
    reset_f = jnp.repeat(jnp.asarray(seq_first, jnp.int32), chunks_per_seg) * (
        jnp.tile(jnp.arange(chunks_per_seg) == 0, len(seq_first))).astype(jnp.int32)
    reset_b = jnp.repeat(jnp.asarray(seq_last, jnp.int32), chunks_per_seg) * (
        jnp.tile(jnp.arange(chunks_per_seg) == chunks_per_seg - 1, len(seq_last))).astype(jnp.int32)
    xf, xb = _s5_scan(sf, sb, mults, reset_f, reset_b, math.gcd(nc, SSM_SCAN_TILE))
    y = _s5_out(h, xf, xb, wbd, qbd, tc)
    return _s5_glu(x, y, modl, g, d_skip.reshape(1, d), w_glu.astype(BF16), seg_len, tm)


def _router_kernel(x_ref, mod_ref, g_ref, w_ref, b_ref, h_ref, idx_ref, gate_ref, rank_ref, cnt_ref,
                   carry_ref):
    tm = x_ref.shape[0]
    n_e = w_ref.shape[0]

    @pl.when(pl.program_id(0) == 0)
    def _():
        carry_ref[...] = jnp.zeros(carry_ref.shape, F32)

    h = _rms_mod(x_ref[...], g_ref[...], mod_ref[0, 4:5, :], mod_ref[0, 3:4, :])
    h_ref[...] = h
    logits = lax.dot_general(w_ref[...], h, (((1,), (1,)), ((), ())), precision=HIGHEST,
                             preferred_element_type=F32) + b_ref[...]
    e_iota = lax.broadcasted_iota(jnp.int32, (n_e, tm), 0)
    work = logits
    chosen = jnp.zeros((n_e, tm), F32)
    vals, idxs = [], []
    for _ in range(TOP_K):
        m = jnp.max(work, axis=0, keepdims=True)
        ix = jnp.min(jnp.where(work == m, e_iota, n_e), axis=0, keepdims=True)
        hit = e_iota == ix
        work = jnp.where(hit, -jnp.inf, work)
        chosen = jnp.where(hit, 1.0, chosen)
        vals.append(m)
        idxs.append(ix)
    v = jnp.concatenate(vals, axis=0)
    ex = jnp.exp(v - v[0:1])
    gate_ref[...] = ex / jnp.sum(ex, axis=0, keepdims=True)
    idx_ref[...] = jnp.concatenate(idxs, axis=0)
    row = lax.broadcasted_iota(jnp.int32, (tm, tm), 0)
    col = lax.broadcasted_iota(jnp.int32, (tm, tm), 1)
    before = (row < col).astype(BF16)
    cum = jnp.dot(chosen.astype(BF16), before, preferred_element_type=F32) + carry_ref[...]
    ranks = [jnp.sum(jnp.where(e_iota == ix, cum, 0.0), axis=0, keepdims=True) for ix in idxs]
    rank_ref[...] = jnp.concatenate(ranks, axis=0).astype(jnp.int32)
    carry_ref[...] = carry_ref[...] + jnp.sum(chosen, axis=1, keepdims=True)
    cnt_ref[...] = jnp.broadcast_to(carry_ref[...], cnt_ref.shape).astype(jnp.int32)


def _router(x, modl, g, w_router_t, b_router, seg_len, tm):
    t, d = x.shape
    n_e = w_router_t.shape[0]
    per_seg = seg_len // tm
    sel = pl.BlockSpec((TOP_K, tm), lambda i: (0, i))
    return pl.pallas_call(
        _router_kernel,
        grid=(t // tm,),
        in_specs=[pl.BlockSpec((tm, d), lambda i: (i, 0)),
                  pl.BlockSpec((1, SUBLANES, d), lambda i: (i // per_seg, 0, 0)),
                  pl.BlockSpec((1, d), lambda i: (0, 0)),
                  pl.BlockSpec((n_e, d), lambda i: (0, 0)),
                  pl.BlockSpec((n_e, 1), lambda i: (0, 0))],
        out_specs=[pl.BlockSpec((tm, d), lambda i: (i, 0)), sel, sel, sel,
                   pl.BlockSpec((n_e, LANES), lambda i: (0, 0))],
        out_shape=[jax.ShapeDtypeStruct((t, d), F32),
                   jax.ShapeDtypeStruct((TOP_K, t), jnp.int32),
                   jax.ShapeDtypeStruct((TOP_K, t), F32),
                   jax.ShapeDtypeStruct((TOP_K, t), jnp.int32),
                   jax.ShapeDtypeStruct((n_e, LANES), jnp.int32)],
        scratch_shapes=[pltpu.VMEM((n_e, 1), F32)],
        compiler_params=_cparams("arbitrary"),
        name="moe_router",
    )(x, modl, g, w_router_t, b_router)


def _dispatch_kernel(pad_from_ref, pad_n_ref, dest_ref, h_ref, xs_hbm, zrow_ref, sem, zsem):
    tm = h_ref.shape[0]

    def zero_row(r):
        return pltpu.make_async_copy(zrow_ref, xs_hbm.at[pl.ds(r, 1), :], zsem)

    @pl.when(pl.program_id(0) == 0)
    def _():
        zrow_ref[...] = jnp.zeros(zrow_ref.shape, F32)
        for e in range(pad_n_ref.shape[0]):
            def fill(r, carry, e=e):
                zero_row(pad_from_ref[e] + r).start()
                return carry
            lax.fori_loop(0, pad_n_ref[e], fill, 0)
        for e in range(pad_n_ref.shape[0]):
            def drain(r, carry):
                zero_row(0).wait()
                return carry
            lax.fori_loop(0, pad_n_ref[e], drain, 0)

    def issue(j, carry):
        for k in range(TOP_K):
            pltpu.make_async_copy(h_ref.at[pl.ds(j, 1), :],
                                  xs_hbm.at[pl.ds(dest_ref[0, 0, j * TOP_K + k], 1), :], sem).start()
        return carry

    lax.fori_loop(0, tm, issue, 0, unroll=DMA_ISSUE_UNROLL)
    for _ in range(TOP_K):
        pltpu.make_async_copy(h_ref, xs_hbm.at[pl.ds(0, tm), :], sem).wait()


def _dispatch(h, dest_tk, pad_from, pad_n, n_rows, tm):
    t, d = h.shape
    return pl.pallas_call(
        _dispatch_kernel,
        grid_spec=pltpu.PrefetchScalarGridSpec(
            num_scalar_prefetch=2, grid=(t // tm,),
            in_specs=[pl.BlockSpec((1, 1, tm * TOP_K), lambda i, pf, pn: (i, 0, 0), memory_space=pltpu.SMEM),
                      pl.BlockSpec((tm, d), lambda i, pf, pn: (i, 0))],
            out_specs=pl.BlockSpec(memory_space=pl.ANY),
            scratch_shapes=[pltpu.VMEM((1, d), F32), pltpu.SemaphoreType.DMA(()),
                            pltpu.SemaphoreType.DMA(())]),
        out_shape=jax.ShapeDtypeStruct((n_rows, d), F32),
        compiler_params=_cparams("arbitrary"),
        name="moe_dispatch",
    )(pad_from, pad_n, dest_tk.reshape(t // tm, 1, tm * TOP_K), h)


def _expert_kernel(be_ref, bi_ref, nu_ref, x_ref, wgu_ref, bgu_ref, wd_ref, bd_ref, y_ref,
                   wgu_bf_ref, wd_bf_ref):
    del bi_ref
    n_ff = wd_ref.shape[2]
    b = pl.program_id(0)

    @pl.when((b == 0) | (be_ref[b] != be_ref[jnp.maximum(b - 1, 0)]))
    def _():
        wgu_bf_ref[...] = wgu_ref[0, 0].astype(BF16)
        wd_bf_ref[...] = wd_ref[0, 0].astype(BF16)

    @pl.when(b < nu_ref[0])
    def _():
        x = x_ref[...].astype(BF16)
        acc = jnp.zeros(y_ref.shape, F32) + bd_ref[0]
        for c in range(n_ff // FF_CHUNK):
            lo = c * FF_CHUNK
            g_part = (jnp.dot(x, wgu_bf_ref[:, lo:lo + FF_CHUNK], preferred_element_type=F32)
                      + bgu_ref[0, :, lo:lo + FF_CHUNK])
            u_part = (jnp.dot(x, wgu_bf_ref[:, n_ff + lo:n_ff + lo + FF_CHUNK], preferred_element_type=F32)
                      + bgu_ref[0, :, n_ff + lo:n_ff + lo + FF_CHUNK])
            g_part = jnp.minimum(g_part, SWIGLU_LIMIT)
            u_part = jnp.clip(u_part, -SWIGLU_LIMIT, SWIGLU_LIMIT)
            act = (u_part + 1.0) * g_part * jax.nn.sigmoid(SWIGLU_ALPHA * g_part)
            acc = acc + jnp.dot(act.astype(BF16), wd_bf_ref[lo:lo + FF_CHUNK, :],
                                preferred_element_type=F32)
        y_ref[...] = acc


def _experts(xs, blk_expert, blk_idx, n_used, layer, w_gu, b_gu, w_down, b_down, tme):
    nr, d = xs.shape
    _, n_e, _, two_f = w_gu.shape
    n_ff = two_f // 2
    nblk = nr // tme
    rows = pl.BlockSpec((tme, d), lambda b, be, bi, nu: (bi[b], 0))
    b_gu = b_gu[layer]
    b_down = b_down[layer]
    return pl.pallas_call(
        _expert_kernel,
        grid_spec=pltpu.PrefetchScalarGridSpec(
            num_scalar_prefetch=3, grid=(nblk,),
            in_specs=[rows,
                      pl.BlockSpec((1, 1, d, two_f), lambda b, be, bi, nu: (layer, be[b], 0, 0)),
                      pl.BlockSpec((1, 1, two_f), lambda b, be, bi, nu: (be[b], 0, 0)),
                      pl.BlockSpec((1, 1, n_ff, d), lambda b, be, bi, nu: (layer, be[b], 0, 0)),
                      pl.BlockSpec((1, 1, d), lambda b, be, bi, nu: (be[b], 0, 0))],
            out_specs=rows,
            scratch_shapes=[pltpu.VMEM((d, two_f), BF16), pltpu.VMEM((n_ff, d), BF16)]),
        out_shape=jax.ShapeDtypeStruct(xs.shape, F32),
        compiler_params=_cparams("arbitrary"),
        name="moe_experts",
    )(blk_expert, blk_idx, n_used, xs, w_gu, b_gu.reshape(n_e, 1, two_f), w_down,
      b_down.reshape(n_e, 1, d))


def _combine_kernel(dest_ref, x_ref, gate_ref, mod_ref, g_ref, ys_hbm, o_ref, buf, sem, *, final_norm):
    tm = x_ref.shape[0]

    def issue(j, carry):
        for k in range(TOP_K):
            pltpu.make_async_copy(ys_hbm.at[pl.ds(dest_ref[0, 0, j * TOP_K + k], 1), :],
                                  buf.at[k, pl.ds(j, 1), :], sem).start()
        return carry

    lax.fori_loop(0, tm, issue, 0, unroll=DMA_ISSUE_UNROLL)
    for k in range(TOP_K):
        pltpu.make_async_copy(ys_hbm.at[pl.ds(0, tm), :], buf.at[k], sem).wait()
    gates = gate_ref[...]
    moe = gates[:, 0:1] * buf[0]
    for k in range(1, TOP_K):
        moe = moe + gates[:, k:k + 1] * buf[k]
    x = x_ref[...] + mod_ref[0, 5:6, :] * moe
    if final_norm:
        ms = jnp.mean(x * x, axis=-1, keepdims=True)
        x = x * lax.rsqrt(ms + NORM_EPS) * g_ref[...]
    o_ref[...] = x


def _combine(x, ys, dest_tk, gates, modl, final_g, seg_len, tm, final_norm, row0=0, rows=None):
    t, d = x.shape
    rows = t if rows is None else rows
    per_seg = seg_len // tm
    b0 = row0 // tm
    return pl.pallas_call(
        functools.partial(_combine_kernel, final_norm=final_norm),
        grid=(rows // tm,),
        in_specs=[pl.BlockSpec((1, 1, tm * TOP_K), lambda i: (i + b0, 0, 0), memory_space=pltpu.SMEM),
                  pl.BlockSpec((tm, d), lambda i: (i + b0, 0)),
                  pl.BlockSpec((tm, TOP_K), lambda i: (i + b0, 0)),
                  pl.BlockSpec((1, SUBLANES, d), lambda i: ((i + b0) // per_seg, 0, 0)),
                  pl.BlockSpec((1, d), lambda i: (0, 0)),
                  pl.BlockSpec(memory_space=pl.ANY)],
        out_specs=pl.BlockSpec((tm, d), lambda i: (i, 0)),
        out_shape=jax.ShapeDtypeStruct((rows, d), F32),
        scratch_shapes=[pltpu.VMEM((TOP_K, tm, d), F32), pltpu.SemaphoreType.DMA(())],
        compiler_params=_cparams("arbitrary"),
        name="moe_combine",
    )(dest_tk.reshape(t // tm, 1, tm * TOP_K), x, gates, modl, final_g, ys)


def _moe_dispatch_and_experts(x, modl, g, w_router, b_router, layer, w_gu, b_gu, w_down, b_down, seg_len, tm):
    t, d = x.shape
    n_e = w_router.shape[1]
    tme = math.gcd(t * TOP_K, EXPERT_ROW_TILE)
    h, idx_t, gate_t, rank_t, cnt = _router(x, modl, g, w_router.T, b_router.reshape(n_e, 1), seg_len, tm)
    counts = cnt[:, 0]
    padded = (counts + tme - 1) // tme * tme
    pad_end = jnp.cumsum(padded)
    pad_start = pad_end - padded
    nblk = (t * TOP_K) // tme + n_e
    n_used = (pad_end[-1] // tme).astype(jnp.int32).reshape(1)
    blk_idx = jnp.minimum(jnp.arange(nblk, dtype=jnp.int32), n_used[0] - 1)
    blk_expert = jnp.minimum(jnp.sum((pad_end[None, :] <= (blk_idx * tme)[:, None]).astype(jnp.int32), axis=1),
                             n_e - 1)
    start_of = jnp.sum(jnp.where(idx_t[None] == jnp.arange(n_e, dtype=jnp.int32)[:, None, None],
                                 pad_start[:, None, None], 0), axis=0)
    dest = (start_of + rank_t).astype(jnp.int32)
    dest_tk = dest.T.reshape(-1)
    xs = _dispatch(h, dest_tk, (pad_start + counts).astype(jnp.int32), (padded - counts).astype(jnp.int32),
                   nblk * tme, math.gcd(seg_len, ROW_DMA_TILE))
    ys = _experts(xs, blk_expert, blk_idx, n_used, layer, w_gu, b_gu, w_down, b_down, tme)
    return ys, dest_tk, gate_t.T


def kernel(x_prompt, x_sample, c_prompt, c_sample, norm1_g, norm2_g, final_g, w_ada, b_ada, w_qkv, w_o,
           lam_q1, lam_k1, lam_q2, lam_k2, subln_g, ssm_a_re, ssm_a_im, ssm_log_dt, ssm_b_re, ssm_b_im,
           ssm_c_re, ssm_c_im, ssm_d, ssm_w_glu, w_router, b_router, w_gu, b_gu, w_down, b_down):
    n_b, seg_len, d = x_prompt.shape
    n_bs, s_len, _ = x_sample.shape
    depth = w_ada.shape[0]
    assert s_len % seg_len == 0 and s_len // seg_len in (1, 2)
    per_sample = s_len // seg_len
    n_seg = n_b + n_bs * per_sample
    t = n_seg * seg_len
    tm = math.gcd(seg_len, TOKEN_TILE)

    seg_seq = list(range(n_b)) + [n_b + j for j in range(n_bs) for _ in range(per_sample)]
    seg_pos = [0] * n_b + [r for _ in range(n_bs) for r in range(per_sample)]
    seg_a = list(range(n_b)) + [n_b + j * per_sample for j in range(n_bs) for _ in range(per_sample)]
    seg_b = list(range(n_b)) + [n_b + j * per_sample + per_sample - 1 for j in range(n_bs) for _ in range(per_sample)]
    two = [0] * n_b + [int(per_sample == 2)] * (n_bs * per_sample)
    seq_first = [1] * n_b + [int(r == 0) for _ in range(n_bs) for r in range(per_sample)]
    seq_last = [1] * n_b + [int(r == per_sample - 1) for _ in range(n_bs) for r in range(per_sample)]
    per_seg = seg_len // tm
    pos_blk = jnp.asarray([seg_pos[s] * per_seg + r for s in range(n_seg) for r in range(per_seg)], jnp.int32)

    x = None
    x_parts = (x_prompt.reshape(n_b * seg_len, d), x_sample.reshape(n_bs * s_len, d))

    n_c = n_b + n_bs
    c_rows = -(-n_c // SUBLANES) * SUBLANES
    c_all = jnp.concatenate([c_prompt, c_sample, jnp.zeros((c_rows - n_c, d), F32)], axis=0)
    mod = _ada(c_all, w_ada, b_ada)
    mod = mod[:, jnp.asarray(seg_seq)].reshape(depth, n_seg, 6, d)
    mod = jnp.concatenate([mod, jnp.zeros((depth, n_seg, SUBLANES - 6, d), F32)], axis=2)

    inv_freq = ROPE_THETA ** (-jnp.arange(0, HEAD_DIM, 2, dtype=F32) / HEAD_DIM)
    ang = jnp.arange(s_len, dtype=F32)[:, None] * inv_freq[None, :]
    cos_t = jnp.tile(jnp.cos(ang), (1, 2 * LANES // HEAD_DIM))
    sin_t = jnp.tile(jnp.concatenate([-jnp.sin(ang), jnp.sin(ang)], axis=1), (1, LANES // HEAD_DIM))

    out_p = out_s = None
    for i in range(depth):
        modl = mod[i]
        g1 = norm1_g[i].reshape(1, d)
        g2 = norm2_g[i].reshape(1, d)
        j = i // 2
        if i % 2 == 0:
            lambda_init = 0.8 - 0.6 * math.exp(-0.3 * i)
            xa, xb = x_parts if x is None else (x, None)
            q, k, vt = _qkv(xa, xb, modl, g1, w_qkv[j], cos_t, sin_t, pos_blk, seg_len, tm)
            lam_pack = jnp.zeros((SUBLANES, 2 * HEAD_DIM), F32).at[0:4, 0:HEAD_DIM].set(
                jnp.stack([lam_q1[j], lam_k1[j], lam_q2[j], lam_k2[j]]).astype(F32))
            o = _flash(q, k, vt, lam_pack, subln_g[j].reshape(2 * HEAD_DIM, 1).astype(F32),
                       jnp.asarray(seg_a, jnp.int32), jnp.asarray(seg_b, jnp.int32),
                       jnp.asarray(two, jnp.int32), seg_len, lambda_init)
            x = _proj_res(o, w_o[j].astype(BF16), xa, xb, modl, seg_len, tm, gate_row=2)
        else:
            if x is None:
                x = jnp.concatenate(x_parts, axis=0)
            params = (ssm_a_re[j], ssm_a_im[j], ssm_log_dt[j], ssm_b_re[j], ssm_b_im[j], ssm_c_re[j],
                      ssm_c_im[j], ssm_d[j], ssm_w_glu[j])
            x = _s5_layer(x, modl, g1, params, seg_len, seq_first, seq_last, tm)
        ys, dest_tk, gates = _moe_dispatch_and_experts(x, modl, g2, w_router[i], b_router[i], i, w_gu, b_gu,
                                                       w_down, b_down, seg_len, tm)
        fg = final_g.reshape(1, d)
        tr = math.gcd(seg_len, ROW_DMA_TILE)
        if i == depth - 1:
            out_p = _combine(x, ys, dest_tk, gates, modl, fg, seg_len, tr, True, 0, n_b * seg_len)
            out_s = _combine(x, ys, dest_tk, gates, modl, fg, seg_len, tr, True, n_b * seg_len, n_bs * s_len)
        else:
            x = _combine(x, ys, dest_tk, gates, modl, fg, seg_len, tr, False)
    return out_p.reshape(n_b, seg_len, d), out_s.reshape(n_bs, s_len, d)
```
